```python
import jax, jax.numpy as jnp
from jax import lax
import numpy as np

D_MODEL = 2048
BATCH = 4
SEQ = 2048
DEPTH = 1

MIX_WIDTH = D_MODEL
RET_HEADS = 4
RET_QK_DIM = D_MODEL // 8
RET_V_DIM = D_MODEL // 8
RET_QK_WIDTH = RET_HEADS * RET_QK_DIM
RET_WIDTH = RET_HEADS * RET_V_DIM
ROPE_BASE = 10000.0
SSD_INNER = MIX_WIDTH - RET_WIDTH
SSD_HEAD_DIM = 64
SSD_HEADS = SSD_INNER // SSD_HEAD_DIM
SSD_GROUPS = 2
SSD_STATE = 128
SSD_CONV = 4
SSD_CONV_DIM = SSD_INNER + 2 * SSD_GROUPS * SSD_STATE
CHUNK = 128
IN_WIDTH = 2 * RET_QK_WIDTH + 2 * RET_WIDTH + SSD_INNER + SSD_CONV_DIM + SSD_HEADS
D_FF = 4 * D_MODEL
EPS = 1e-6

kernel_name = "hymba_retention_ssd_hybrid_layer"


def rmsnorm(x, w):
    xf = x.astype(jnp.float32)
    y = xf * lax.rsqrt(jnp.mean(xf * xf, axis=-1, keepdims=True) + EPS)
    return (y * w.astype(jnp.float32)).astype(x.dtype)


def rope(x, pos):
    half = x.shape[-1] // 2
    inv_freq = ROPE_BASE ** (-jnp.arange(half, dtype=jnp.float32) / half)
    ang = pos[:, None] * inv_freq[None, :]
    cos = jnp.cos(ang)[None, :, None, :]
    sin = jnp.sin(ang)[None, :, None, :]
    x1, x2 = x[..., :half], x[..., half:]
    return jnp.concatenate([x1 * cos - x2 * sin, x1 * sin + x2 * cos], axis=-1)


def retention_chunkwise(q, k, v):
    b, s, h, dk = q.shape
    dv = v.shape[-1]
    n = s // CHUNK
    log_gamma = jnp.log(1.0 - 2.0 ** (-5.0 - jnp.arange(h, dtype=jnp.float32)))
    idx = jnp.arange(CHUNK, dtype=jnp.float32)
    rel = idx[:, None] - idx[None, :]
    causal = rel >= 0
    decay_intra = jnp.where(causal[None],
                            jnp.exp(jnp.where(causal, rel, 0.0)[None] * log_gamma[:, None, None]),
                            0.0)
    k = k * (dk ** -0.5)
    qc = q.reshape(b, n, CHUNK, h, dk)
    kc = k.reshape(b, n, CHUNK, h, dk)
    vc = v.reshape(b, n, CHUNK, h, dv)
    scores = jnp.einsum('bnchd,bnmhd->bnhcm', qc, kc) * decay_intra
    y_intra = jnp.einsum('bnhcm,bnmhe->bnche', scores, vc)
    k_decay = jnp.exp((CHUNK - 1.0 - idx)[None, :] * log_gamma[:, None])
    kv = jnp.einsum('bnmhd,bnmhe,hm->nbhde', kc, vc, k_decay)
    chunk_decay = jnp.exp(CHUNK * log_gamma)[None, :, None, None]

    def step(state, kv_n):
        return chunk_decay * state + kv_n, state

    _, prev = lax.scan(step, jnp.zeros((b, h, dk, dv), kv.dtype), kv)
    q_decay = jnp.exp((idx + 1.0)[None, :] * log_gamma[:, None])
    y_cross = jnp.einsum('bnchd,nbhde,hc->bnche', qc, prev, q_decay)
    return (y_intra + y_cross).reshape(b, s, h, dv)


def ssd_chunked(x, dt, a, bmat, cmat):
    b, s, H, p = x.shape
    G, N = bmat.shape[2], bmat.shape[3]
    E = H // G
    n = s // CHUNK
    xc = (x * dt[..., None]).reshape(b, n, CHUNK, G, E, p)
    ac = (dt * a).reshape(b, n, CHUNK, G, E)
    bc = bmat.reshape(b, n, CHUNK, G, N)
    cc = cmat.reshape(b, n, CHUNK, G, N)
    a_cs = jnp.cumsum(ac, axis=2)
    seg = a_cs[:, :, :, None] - a_cs[:, :, None, :]
    causal = (jnp.arange(CHUNK)[:, None] >= jnp.arange(CHUNK)[None, :])[:, :, None, None]
    L = jnp.exp(jnp.where(causal, seg, -jnp.inf))
    cb = jnp.einsum('bnlgk,bnsgk->bnlsg', cc, bc)
    y_diag = jnp.einsum('bnlsg,bnlsge,bnsgep->bnlgep', cb, L, xc)
    decay_states = jnp.exp(a_cs[:, :, -1:] - a_cs)
    states = jnp.einsum('bnlgk,bnlge,bnlgep->nbgepk', bc, decay_states, xc)
    chunk_decay = jnp.moveaxis(jnp.exp(a_cs[:, :, -1]), 1, 0)

    def step(state, inp):
        st, dec = inp
        return dec[..., None, None] * state + st, state

    _, prev = lax.scan(step, jnp.zeros((b, G, E, p, N), states.dtype), (states, chunk_decay))
    y_off = jnp.einsum('bnlgk,nbgepk,bnlge->bnlgep', cc, prev, jnp.exp(a_cs))
    return (y_diag + y_off).reshape(b, s, H, p)


def causal_depthwise_conv(u, w, bias):
    K, c = w.shape
    out = lax.conv_general_dilated(u, w[:, None, :], window_strides=(1,), padding=[(K - 1, 0)],
                                   dimension_numbers=('NWC', 'WIO', 'NWC'), feature_group_count=c)
    return out + bias


def hybrid_mixer(h, w_in, ret_norm_w, conv_w, conv_b, dt_bias, a_log, d_skip, ssd_norm_w, w_out):
    b, s, _ = h.shape
    f32 = jnp.float32
    proj = jnp.einsum('bsd,df->bsf', h, w_in)
    o1 = RET_QK_WIDTH
    o2 = o1 + RET_QK_WIDTH
    o3 = o2 + RET_WIDTH
    o4 = o3 + RET_WIDTH
    o5 = o4 + SSD_INNER
    o6 = o5 + SSD_CONV_DIM
    q, k, v, g, z, xbc, dt = jnp.split(proj, [o1, o2, o3, o4, o5, o6], axis=-1)

    pos = jnp.arange(s, dtype=f32)
    q = rope(q.astype(f32).reshape(b, s, RET_HEADS, RET_QK_DIM), pos)
    k = rope(k.astype(f32).reshape(b, s, RET_HEADS, RET_QK_DIM), pos)
    v = v.astype(f32).reshape(b, s, RET_HEADS, RET_V_DIM)
    y_ret = retention_chunkwise(q, k, v)
    y_ret = y_ret * lax.rsqrt(jnp.mean(y_ret * y_ret, axis=-1, keepdims=True) + EPS)
    y_ret = y_ret * ret_norm_w.astype(f32).reshape(RET_HEADS, RET_V_DIM)
    y_ret = y_ret.reshape(b, s, RET_WIDTH) * jax.nn.silu(g.astype(f32))

    xbc = jax.nn.silu(causal_depthwise_conv(xbc.astype(f32), conv_w.astype(f32), conv_b.astype(f32)))
    xs, bm, cm = jnp.split(xbc, [SSD_INNER, SSD_INNER + SSD_GROUPS * SSD_STATE], axis=-1)
    xs = xs.reshape(b, s, SSD_HEADS, SSD_HEAD_DIM)
    bm = bm.reshape(b, s, SSD_GROUPS, SSD_STATE)
    cm = cm.reshape(b, s, SSD_GROUPS, SSD_STATE)
    dt = jax.nn.softplus(dt.astype(f32) + dt_bias.astype(f32))
    a = -jnp.exp(a_log.astype(f32))
    y_ssd = ssd_chunked(xs, dt, a, bm, cm) + d_skip.astype(f32)[:, None] * xs
    y_ssd = y_ssd.reshape(b, s, SSD_INNER) * jax.nn.silu(z.astype(f32))
    y_ssd = y_ssd.reshape(b, s, SSD_GROUPS, SSD_INNER // SSD_GROUPS)
    y_ssd = y_ssd * lax.rsqrt(jnp.mean(y_ssd * y_ssd, axis=-1, keepdims=True) + EPS)
    y_ssd = y_ssd.reshape(b, s, SSD_INNER) * ssd_norm_w.astype(f32)

    mix = jnp.concatenate([y_ret, y_ssd], axis=-1).astype(h.dtype)
    return jnp.einsum('bsf,fd->bsd', mix, w_out)


def squared_relu_mlp(h, w_up, w_down):
    u = jax.nn.relu(jnp.einsum('bsd,df->bsf', h, w_up))
    return jnp.einsum('bsf,fd->bsd', u * u, w_down)


def setup_inputs(seed: int = 0) -> dict:
    key = jax.random.key(seed)
    ks = jax.random.split(key, 16)
    f32 = jnp.float32
    x = jax.random.normal(ks[0], (BATCH, SEQ, D_MODEL), f32)
    norm_mix_w = 1.0 + 0.01 * jax.random.normal(ks[1], (D_MODEL,), f32)
    w_in = jax.random.normal(ks[2], (D_MODEL, IN_WIDTH), f32) * D_MODEL ** -0.5
    ret_norm_w = 1.0 + 0.01 * jax.random.normal(ks[3], (RET_WIDTH,), f32)
    conv_w = jax.random.normal(ks[4], (SSD_CONV, SSD_CONV_DIM), f32) * SSD_CONV ** -0.5
    conv_b = 0.01 * jax.random.normal(ks[5], (SSD_CONV_DIM,), f32)
    dt0 = jnp.exp(jax.random.uniform(ks[6], (SSD_HEADS,), f32, jnp.log(1e-3), jnp.log(1e-1)))
    dt_bias = dt0 + jnp.log(-jnp.expm1(-dt0))
    a_log = jnp.log(jax.random.uniform(ks[7], (SSD_HEADS,), f32, 1.0, 16.0))
    d_skip = 1.0 + 0.01 * jax.random.normal(ks[8], (SSD_HEADS,), f32)
    ssd_norm_w = 1.0 + 0.01 * jax.random.normal(ks[9], (SSD_INNER,), f32)
    w_out = jax.random.normal(ks[10], (MIX_WIDTH, D_MODEL), f32) * MIX_WIDTH ** -0.5
    norm_mlp_w = 1.0 + 0.01 * jax.random.normal(ks[11], (D_MODEL,), f32)
    w_up = jax.random.normal(ks[12], (D_MODEL, D_FF), f32) * D_MODEL ** -0.5
    w_down = jax.random.normal(ks[13], (D_FF, D_MODEL), f32) * D_FF ** -0.5
    norm_final_w = 1.0 + 0.01 * jax.random.normal(ks[14], (D_MODEL,), f32)
    return {"x": x, "norm_mix_w": norm_mix_w, "w_in": w_in, "ret_norm_w": ret_norm_w,
            "conv_w": conv_w, "conv_b": conv_b, "dt_bias": dt_bias, "a_log": a_log,
            "d_skip": d_skip, "ssd_norm_w": ssd_norm_w, "w_out": w_out,
            "norm_mlp_w": norm_mlp_w, "w_up": w_up, "w_down": w_down,
            "norm_final_w": norm_final_w}


def reference(x, norm_mix_w, w_in, ret_norm_w, conv_w, conv_b, dt_bias, a_log, d_skip,
              ssd_norm_w, w_out, norm_mlp_w, w_up, w_down, norm_final_w):
    h = x
    for _ in range(DEPTH):
        h = h + hybrid_mixer(rmsnorm(h, norm_mix_w), w_in, ret_norm_w, conv_w, conv_b,
                             dt_bias, a_log, d_skip, ssd_norm_w, w_out)
        h = h + squared_relu_mlp(rmsnorm(h, norm_mlp_w), w_up, w_down)
    return rmsnorm(h, norm_final_w)
```

```python
import functools

import numpy as np
import jax
import jax.numpy as jnp
from jax import lax
from jax.experimental import pallas as pl
from jax.experimental.pallas import tpu as pltpu

F32 = jnp.float32
BF16 = jnp.bfloat16

D_MODEL = 2048
BATCH = 4
SEQ = 2048
TOKENS = BATCH * SEQ
RET_HEADS = 4
RET_DIM = 256
RET_WIDTH = RET_HEADS * RET_DIM
ROPE_BASE = 10000.0
SSD_INNER = 1024
SSD_HEAD_DIM = 64
SSD_HEADS = 16
SSD_GROUPS = 2
SSD_STATE = 128
SSD_CONV = 4
SSD_CONV_DIM = SSD_INNER + 2 * SSD_GROUPS * SSD_STATE
CHUNK = 128
NUM_CHUNKS = SEQ // CHUNK
PROJ_WIDTH = 4 * RET_WIDTH + SSD_INNER + SSD_CONV_DIM
DT_PAD = 128
D_FF = 4 * D_MODEL
EPS = 1e-6

VMEM_LIMIT = 56 * 1024 * 1024


def _rms_scale(x):
    return lax.rsqrt(jnp.mean(x * x, axis=-1, keepdims=True) + EPS)


IN_TM = 1024
IN_TN = 512
IN_ROWS = 128


def _inproj_kernel(x_ref, nw_ref, w_ref, wdt_ref, proj_ref, dt_ref, hn_ref):
    @pl.when(pl.program_id(1) == 0)
    def _():
        def body(i, carry):
            r = pl.multiple_of(i * IN_ROWS, IN_ROWS)
            x = x_ref[pl.ds(r, IN_ROWS), :]
            hn_ref[pl.ds(r, IN_ROWS), :] = (x * _rms_scale(x) * nw_ref[...]).astype(BF16)
            return carry
        lax.fori_loop(0, IN_TM // IN_ROWS, body, 0)
        dt_ref[...] = jnp.dot(hn_ref[...], wdt_ref[...], preferred_element_type=F32)

    proj_ref[...] = jnp.dot(hn_ref[...], w_ref[...],
                            preferred_element_type=F32).astype(BF16)


def _inproj(x2d, norm_w, w_main, w_dt):
    grid = (TOKENS // IN_TM, PROJ_WIDTH // IN_TN)
    return pl.pallas_call(
        _inproj_kernel,
        grid=grid,
        in_specs=[
            pl.BlockSpec((IN_TM, D_MODEL), lambda m, n: (m, 0)),
            pl.BlockSpec((1, D_MODEL), lambda m, n: (0, 0)),
            pl.BlockSpec((D_MODEL, IN_TN), lambda m, n: (0, n)),
            pl.BlockSpec((D_MODEL, DT_PAD), lambda m, n: (0, 0)),
        ],
        out_specs=[
            pl.BlockSpec((IN_TM, IN_TN), lambda m, n: (m, n)),
            pl.BlockSpec((IN_TM, DT_PAD), lambda m, n: (m, 0)),
        ],
        out_shape=[
            jax.ShapeDtypeStruct((TOKENS, PROJ_WIDTH), BF16),
            jax.ShapeDtypeStruct((TOKENS, DT_PAD), F32),
        ],
        scratch_shapes=[pltpu.VMEM((IN_TM, D_MODEL), BF16)],
        compiler_params=pltpu.CompilerParams(
            dimension_semantics=("arbitrary", "arbitrary"),
            vmem_limit_bytes=VMEM_LIMIT),
        name="inproj",
    )(x2d, norm_w, w_main, w_dt)


_NT = (((1,), (1,)), ((), ()))
_TN = (((0,), (0,)), ((), ()))


def _ret_gammas():
    return 1.0 - 2.0 ** (-5.0 - np.arange(RET_HEADS, dtype=np.float64))


def _ret_tables():
    lg = np.log(_ret_gammas())
    idx = np.arange(CHUNK, dtype=np.float64)
    rel = idx[:, None] - idx[None, :]
    causal = rel >= 0
    dintra = np.where(causal[None], np.exp(np.where(causal, rel, 0.0)[None] * lg[:, None, None]), 0.0)
    qdec = np.exp((idx + 1.0)[:, None] * lg[None, :])
    kdec = np.exp((CHUNK - 1.0 - idx)[:, None] * lg[None, :])
    qdec = np.repeat(qdec, RET_DIM, axis=1)
    kdec = np.repeat(kdec, RET_DIM, axis=1)
    return (jnp.asarray(dintra, F32), jnp.asarray(qdec, F32), jnp.asarray(kdec, F32))


def _rope_tables():
    half = RET_DIM // 2
    inv_freq = ROPE_BASE ** (-jnp.arange(half, dtype=F32) / half)
    ang = jnp.arange(SEQ, dtype=F32)[:, None] * inv_freq[None, :]
    return jnp.cos(ang), jnp.sin(ang)


def _cumsum_lanes(x):
    lane = lax.broadcasted_iota(jnp.int32, x.shape, 1)
    k = 1
    while k < x.shape[1]:
        x = x + jnp.where(lane >= k, pltpu.roll(x, k, axis=1), 0.0)
        k *= 2
    return x


def _mixer_kernel(q_ref, k_ref, v_ref, g_ref, z_ref, xs_ref, bc_ref, dt_ref, cos_ref, sin_ref,
                  dintra_ref, qdec_ref, kdec_ref, rnw_ref, cw_ref, cb_ref, dtb_ref, alog_ref,
                  dskip_ref, snw_ref, out_ref, rstate, sstate, xp):
    @pl.when(pl.program_id(1) == 0)
    def _():
        rstate[...] = jnp.zeros_like(rstate)
        sstate[...] = jnp.zeros_like(sstate)
        xp[0:8, :] = jnp.zeros((8, SSD_CONV_DIM), F32)

    cos = cos_ref[...]
    sin = sin_ref[...]
    half = RET_DIM // 2

    def rope(x):
        x1, x2 = x[:, :half], x[:, half:]
        return jnp.concatenate([x1 * cos - x2 * sin, x1 * sin + x2 * cos], axis=-1)

    chunk_decay = _ret_gammas() ** CHUNK
    for h in range(RET_HEADS):
        sl = slice(h * RET_DIM, (h + 1) * RET_DIM)
        qr = rope(q_ref[:, sl].astype(F32))
        kr = rope(k_ref[:, sl].astype(F32)) * (RET_DIM ** -0.5)
        vh = v_ref[:, sl]
        scores = lax.dot_general(qr.astype(BF16), kr.astype(BF16), _NT,
                                 preferred_element_type=F32)
        p = (scores * dintra_ref[h]).astype(BF16)
        y = jnp.dot(p, vh, preferred_element_type=F32)
        y = y + jnp.dot((qr * qdec_ref[:, sl]).astype(BF16), rstate[h].astype(BF16),
                        preferred_element_type=F32)
        kv = lax.dot_general((kr * kdec_ref[:, sl]).astype(BF16), vh, _TN,
                             preferred_element_type=F32)
        rstate[h] = float(chunk_decay[h]) * rstate[h] + kv
        yn = y * _rms_scale(y) * rnw_ref[:, sl]
        gh = g_ref[:, sl].astype(F32)
        out_ref[:, sl] = (yn * jax.nn.silu(gh)).astype(BF16)

    xp[8:8 + CHUNK, 0:SSD_INNER] = xs_ref[...].astype(F32)
    xp[8:8 + CHUNK, SSD_INNER:SSD_CONV_DIM] = bc_ref[...].astype(F32)
    conv = cb_ref[...] + cw_ref[SSD_CONV - 1:SSD_CONV, :] * xp[8:8 + CHUNK, :]
    for j in range(SSD_CONV - 1):
        off = 8 - (SSD_CONV - 1) + j
        conv = conv + cw_ref[j:j + 1, :] * xp[off:off + CHUNK, :]
    xp[0:8, :] = xp[CHUNK:CHUNK + 8, :]
    act = jax.nn.silu(conv)
    xs = act[:, :SSD_INNER]
    xs_t = xs.T

    dt_t = jax.nn.softplus(dt_ref[...].T[0:SSD_HEADS, :] + dtb_ref[...])
    acs_t = _cumsum_lanes(dt_t * (-jnp.exp(alog_ref[...])))
    a_last = acs_t[:, CHUNK - 1:CHUNK]
    w_t = jnp.exp(a_last - acs_t) * dt_t
    ea_t = jnp.exp(acs_t)
    cdec = jnp.broadcast_to(jnp.exp(a_last), (SSD_HEADS, CHUNK))
    acs_pad = jnp.concatenate([acs_t, jnp.zeros((CHUNK - SSD_HEADS, CHUNK), F32)], axis=0)
    acs_col = acs_pad.T
    dskip = dskip_ref[...]

    row = lax.broadcasted_iota(jnp.int32, (CHUNK, CHUNK), 0)
    col = lax.broadcasted_iota(jnp.int32, (CHUNK, CHUNK), 1)
    causal_t = col >= row

    hpg = SSD_HEADS // SSD_GROUPS
    gw = hpg * SSD_HEAD_DIM
    y_t_parts = []
    for g in range(SSD_GROUPS):
        bg = act[:, SSD_INNER + g * SSD_STATE:SSD_INNER + (g + 1) * SSD_STATE].astype(BF16)
        cbase = SSD_INNER + SSD_GROUPS * SSD_STATE
        cg = act[:, cbase + g * SSD_STATE:cbase + (g + 1) * SSD_STATE].astype(BF16)
        cb_t = lax.dot_general(bg, cg, _NT, preferred_element_type=F32)
        s_prev = sstate[g * gw:(g + 1) * gw, :]
        yo_t = lax.dot_general(s_prev.astype(BF16), cg, _NT, preferred_element_type=F32)

        def rows_of(t, hh):
            return jnp.broadcast_to(t[hh:hh + 1, :], (SSD_HEAD_DIM, CHUNK))

        xw_parts, cd_parts = [], []
        for e in range(hpg):
            hh = g * hpg + e
            xs_h = xs_t[hh * SSD_HEAD_DIM:(hh + 1) * SSD_HEAD_DIM, :]
            seg = (jnp.broadcast_to(acs_t[hh:hh + 1, :], (CHUNK, CHUNK))
                   - jnp.broadcast_to(acs_col[:, hh:hh + 1], (CHUNK, CHUNK)))
            l_t = jnp.exp(jnp.where(causal_t, seg, -jnp.inf))
            m_t = (cb_t * l_t).astype(BF16)
            xdt = (xs_h * rows_of(dt_t, hh)).astype(BF16)
            yd = jnp.dot(xdt, m_t, preferred_element_type=F32)
            yo = yo_t[e * SSD_HEAD_DIM:(e + 1) * SSD_HEAD_DIM, :] * rows_of(ea_t, hh)
            y_t_parts.append(yd + yo + rows_of(dskip, hh) * xs_h)
            xw_parts.append((xs_h * rows_of(w_t, hh)).astype(BF16))
            cd_parts.append(rows_of(cdec, hh))
        xw = jnp.concatenate(xw_parts, axis=0)
        cd = jnp.concatenate(cd_parts, axis=0)
        sstate[g * gw:(g + 1) * gw, :] = cd * s_prev + jnp.dot(xw, bg, preferred_element_type=F32)

    y = jnp.concatenate(y_t_parts, axis=0).T
    y = y * jax.nn.silu(z_ref[...].astype(F32))
    for g in range(SSD_GROUPS):
        sl = slice(g * gw, (g + 1) * gw)
        yg = y[:, sl]
        out_ref[:, RET_WIDTH + g * gw:RET_WIDTH + (g + 1) * gw] = (
            yg * _rms_scale(yg) * snw_ref[:, sl]).astype(BF16)


def _mixer(proj, dt, params):
    (ret_norm_w, conv_w, conv_b, dt_bias, a_log, d_skip, ssd_norm_w) = params
    cos, sin = _rope_tables()
    dintra, qdec, kdec = _ret_tables()

    def rows(b, c):
        return b * NUM_CHUNKS + c

    def col_block(j, width):
        return pl.BlockSpec((CHUNK, width), lambda b, c: (rows(b, c), j))

    def full(shape):
        return pl.BlockSpec(shape, lambda b, c: (0,) * len(shape))

    bcast = lambda v: jnp.broadcast_to(v.astype(F32)[:, None], (SSD_HEADS, CHUNK))
    in_specs = [
        col_block(0, RET_WIDTH), col_block(1, RET_WIDTH), col_block(2, RET_WIDTH),
        col_block(3, RET_WIDTH), col_block(4, SSD_INNER), col_block(5, SSD_INNER),
        col_block(12, 2 * SSD_GROUPS * SSD_STATE),
        pl.BlockSpec((CHUNK, DT_PAD), lambda b, c: (rows(b, c), 0)),
        pl.BlockSpec((CHUNK, RET_DIM // 2), lambda b, c: (c, 0)),
        pl.BlockSpec((CHUNK, RET_DIM // 2), lambda b, c: (c, 0)),
        full((RET_HEADS, CHUNK, CHUNK)), full((CHUNK, RET_WIDTH)), full((CHUNK, RET_WIDTH)),
        full((1, RET_WIDTH)), full((SSD_CONV, SSD_CONV_DIM)), full((1, SSD_CONV_DIM)),
        full((SSD_HEADS, CHUNK)), full((SSD_HEADS, CHUNK)), full((SSD_HEADS, CHUNK)),
        full((1, SSD_INNER)),
    ]
    return pl.pallas_call(
        _mixer_kernel,
        grid=(BATCH, NUM_CHUNKS),
        in_specs=in_specs,
        out_specs=pl.BlockSpec((CHUNK, D_MODEL), lambda b, c: (rows(b, c), 0)),
        out_shape=jax.ShapeDtypeStruct((TOKENS, D_MODEL), BF16),
        scratch_shapes=[
            pltpu.VMEM((RET_HEADS, RET_DIM, RET_DIM), F32),
            pltpu.VMEM((SSD_INNER, SSD_STATE), F32),
            pltpu.VMEM((CHUNK + 8, SSD_CONV_DIM), F32),
        ],
        compiler_params=pltpu.CompilerParams(
            dimension_semantics=("arbitrary", "arbitrary"),
            vmem_limit_bytes=VMEM_LIMIT),
        name="mixer",
    )(proj, proj, proj, proj, proj, proj, proj, dt, cos, sin, dintra, qdec, kdec,
      ret_norm_w.astype(F32)[None, :], conv_w.astype(F32), conv_b.astype(F32)[None, :],
      bcast(dt_bias), bcast(a_log), bcast(d_skip), ssd_norm_w.astype(F32)[None, :])


OUT_TM = 512


def _outproj_kernel(mix_ref, w_ref, x_ref, nw_ref, h_ref, hn_ref):
    h = x_ref[...] + jnp.dot(mix_ref[...], w_ref[...], preferred_element_type=F32)
    h_ref[...] = h
    hn_ref[...] = (h * _rms_scale(h) * nw_ref[...]).astype(BF16)


def _outproj(mix, w_out, x2d, norm_w):
    return pl.pallas_call(
        _outproj_kernel,
        grid=(TOKENS // OUT_TM,),
        in_specs=[
            pl.BlockSpec((OUT_TM, D_MODEL), lambda m: (m, 0)),
            pl.BlockSpec((D_MODEL, D_MODEL), lambda m: (0, 0)),
            pl.BlockSpec((OUT_TM, D_MODEL), lambda m: (m, 0)),
            pl.BlockSpec((1, D_MODEL), lambda m: (0, 0)),
        ],
        out_specs=[
            pl.BlockSpec((OUT_TM, D_MODEL), lambda m: (m, 0)),
            pl.BlockSpec((OUT_TM, D_MODEL), lambda m: (m, 0)),
        ],
        out_shape=[
            jax.ShapeDtypeStruct((TOKENS, D_MODEL), F32),
            jax.ShapeDtypeStruct((TOKENS, D_MODEL), BF16),
        ],
        compiler_params=pltpu.CompilerParams(
            dimension_semantics=("arbitrary",),
            vmem_limit_bytes=VMEM_LIMIT),
        name="outproj",
    )(mix, w_out, x2d, norm_w)


MLP_TM = 512
MLP_TF = 512


def _mlp_kernel(hn_ref, wup_ref, wdn_ref, h_ref, nw_ref, out_ref, acc_ref):
    f = pl.program_id(1)
    u = jnp.maximum(jnp.dot(hn_ref[...], wup_ref[...], preferred_element_type=F32), 0.0)
    d = jnp.dot((u * u).astype(BF16), wdn_ref[...], preferred_element_type=F32)

    @pl.when(f == 0)
    def _():
        acc_ref[...] = d

    @pl.when(f > 0)
    def _():
        acc_ref[...] += d

    @pl.when(f == pl.num_programs(1) - 1)
    def _():
        h = h_ref[...] + acc_ref[...]
        out_ref[...] = h * _rms_scale(h) * nw_ref[...]


def _mlp(hn, w_up, w_down, h, norm_w):
    return pl.pallas_call(
        _mlp_kernel,
        grid=(TOKENS // MLP_TM, D_FF // MLP_TF),
        in_specs=[
            pl.BlockSpec((MLP_TM, D_MODEL), lambda m, f: (m, 0)),
            pl.BlockSpec((D_MODEL, MLP_TF), lambda m, f: (0, f)),
            pl.BlockSpec((MLP_TF, D_MODEL), lambda m, f: (f, 0)),
            pl.BlockSpec((MLP_TM, D_MODEL), lambda m, f: (m, 0)),
            pl.BlockSpec((1, D_MODEL), lambda m, f: (0, 0)),
        ],
        out_specs=pl.BlockSpec((MLP_TM, D_MODEL), lambda m, f: (m, 0)),
        out_shape=jax.ShapeDtypeStruct((TOKENS, D_MODEL), F32),
        scratch_shapes=[pltpu.VMEM((MLP_TM, D_MODEL), F32)],
        compiler_params=pltpu.CompilerParams(
            dimension_semantics=("arbitrary", "arbitrary"),
            vmem_limit_bytes=VMEM_LIMIT),
        name="mlp",
    )(hn, w_up, w_down, h, norm_w)


def kernel(x, norm_mix_w, w_in, ret_norm_w, conv_w, conv_b, dt_bias, a_log, d_skip, ssd_norm_w,
           w_out, norm_mlp_w, w_up, w_down, norm_final_w):
    x2d = x.reshape(TOKENS, D_MODEL)
    w_main = w_in[:, :PROJ_WIDTH].astype(BF16)
    w_dt = jnp.pad(w_in[:, PROJ_WIDTH:], ((0, 0), (0, DT_PAD - SSD_HEADS))).astype(BF16)
    proj, dt = _inproj(x2d, norm_mix_w.astype(F32)[None, :], w_main, w_dt)
    mix = _mixer(proj, dt, (ret_norm_w, conv_w, conv_b, dt_bias, a_log, d_skip, ssd_norm_w))
    h, hn = _outproj(mix, w_out.astype(BF16), x2d, norm_mlp_w.astype(F32)[None, :])
    out = _mlp(hn, w_up.astype(BF16), w_down.astype(BF16), h, norm_final_w.astype(F32)[None, :])
    return out.reshape(BATCH, SEQ, D_MODEL)
```

```python
import functools

import numpy as np
import jax
import jax.numpy as jnp
from jax import lax
from jax.experimental import pallas as pl
from jax.experimental.pallas import tpu as pltpu

F32 = jnp.float32
BF16 = jnp.bfloat16

D_MODEL = 2048
BATCH = 4
SEQ = 2048
TOKENS = BATCH * SEQ
RET_HEADS = 4
RET_DIM = 256
RET_WIDTH = RET_HEADS * RET_DIM
ROPE_BASE = 10000.0
SSD_INNER = 1024
SSD_HEAD_DIM = 64
SSD_HEADS = 16
SSD_GROUPS = 2
SSD_STATE = 128
SSD_CONV = 4
SSD_CONV_DIM = SSD_INNER + 2 * SSD_GROUPS * SSD_STATE
CHUNK = 128
NUM_CHUNKS = SEQ // CHUNK
PROJ_WIDTH = 4 * RET_WIDTH + SSD_INNER + SSD_CONV_DIM
DT_PAD = 128
D_FF = 4 * D_MODEL
EPS = 1e-6

VMEM_LIMIT = 56 * 1024 * 1024


def _rms_scale(x):
    return lax.rsqrt(jnp.mean(x * x, axis=-1, keepdims=True) + EPS)


IN_TM = 1024
IN_TN = 512
IN_ROWS = 128


def _inproj_kernel(x_ref, nw_ref, w_ref, wdt_ref, proj_ref, dt_ref, hn_ref):
    @pl.when(pl.program_id(1) == 0)
    def _():
        def body(i, carry):
            r = pl.multiple_of(i * IN_ROWS, IN_ROWS)
            x = x_ref[pl.ds(r, IN_ROWS), :]
            hn_ref[pl.ds(r, IN_ROWS), :] = (x * _rms_scale(x) * nw_ref[...]).astype(BF16)
            return carry
        lax.fori_loop(0, IN_TM // IN_ROWS, body, 0)
        dt_ref[...] = jnp.dot(hn_ref[...], wdt_ref[...], preferred_element_type=F32)

    proj_ref[...] = jnp.dot(hn_ref[...], w_ref[...],
                            preferred_element_type=F32).astype(BF16)


def _inproj(x2d, norm_w, w_main, w_dt):
    grid = (TOKENS // IN_TM, PROJ_WIDTH // IN_TN)
    return pl.pallas_call(
        _inproj_kernel,
        grid=grid,
        in_specs=[
            pl.BlockSpec((IN_TM, D_MODEL), lambda m, n: (m, 0)),
            pl.BlockSpec((1, D_MODEL), lambda m, n: (0, 0)),
            pl.BlockSpec((D_MODEL, IN_TN), lambda m, n: (0, n)),
            pl.BlockSpec((D_MODEL, DT_PAD), lambda m, n: (0, 0)),
        ],
        out_specs=[
            pl.BlockSpec((IN_TM, IN_TN), lambda m, n: (m, n)),
            pl.BlockSpec((IN_TM, DT_PAD), lambda m, n: (m, 0)),
        ],
        out_shape=[
            jax.ShapeDtypeStruct((TOKENS, PROJ_WIDTH), BF16),
            jax.ShapeDtypeStruct((TOKENS, DT_PAD), F32),
        ],
        scratch_shapes=[pltpu.VMEM((IN_TM, D_MODEL), BF16)],
        compiler_params=pltpu.CompilerParams(
            dimension_semantics=("arbitrary", "arbitrary"),
            vmem_limit_bytes=VMEM_LIMIT),
        name="inproj",
    )(x2d, norm_w, w_main, w_dt)


_NT = (((1,), (1,)), ((), ()))
_TN = (((0,), (0,)), ((), ()))


def _ret_gammas():
    return 1.0 - 2.0 ** (-5.0 - np.arange(RET_HEADS, dtype=np.float64))


def _ret_tables():
    lg = np.log(_ret_gammas())
    idx = np.arange(CHUNK, dtype=np.float64)
    rel = idx[:, None] - idx[None, :]
    causal = rel >= 0
    dintra = np.where(causal[None], np.exp(np.where(causal, rel, 0.0)[None] * lg[:, None, None]), 0.0)
    qdec = np.exp((idx + 1.0)[:, None] * lg[None, :])
    kdec = np.exp((CHUNK - 1.0 - idx)[:, None] * lg[None, :])
    qdec = np.repeat(qdec, RET_DIM, axis=1)
    kdec = np.repeat(kdec, RET_DIM, axis=1)
    return (jnp.asarray(dintra, F32), jnp.asarray(qdec, F32), jnp.asarray(kdec, F32))


def _rope_tables():
    half = RET_DIM // 2
    inv_freq = ROPE_BASE ** (-jnp.arange(half, dtype=F32) / half)
    ang = jnp.arange(SEQ, dtype=F32)[:, None] * inv_freq[None, :]
    return jnp.cos(ang), jnp.sin(ang)


def _cumsum_lanes(x):
    lane = lax.broadcasted_iota(jnp.int32, x.shape, 1)
    k = 1
    while k < x.shape[1]:
        x = x + jnp.where(lane >= k, pltpu.roll(x, k, axis=1), 0.0)
        k *= 2
    return x


def _mixer_kernel(q_ref, k_ref, v_ref, g_ref, z_ref, xs_ref, bc_ref, dt_ref, cos_ref, sin_ref,
                  dintra_ref, qdec_ref, kdec_ref, rnw_ref, cw_ref, cb_ref, dtb_ref, alog_ref,
                  dskip_ref, snw_ref, out_ref, rstate, sstate, xp):
    @pl.when(pl.program_id(1) == 0)
    def _():
        rstate[...] = jnp.zeros_like(rstate)
        sstate[...] = jnp.zeros_like(sstate)
        xp[0:8, :] = jnp.zeros((8, SSD_CONV_DIM), F32)

    cos = cos_ref[...]
    sin = sin_ref[...]
    half = RET_DIM // 2

    def rope(x):
        x1, x2 = x[:, :half], x[:, half:]
        return jnp.concatenate([x1 * cos - x2 * sin, x1 * sin + x2 * cos], axis=-1)

    chunk_decay = _ret_gammas() ** CHUNK
    for h in range(RET_HEADS):
        sl = slice(h * RET_DIM, (h + 1) * RET_DIM)
        qr = rope(q_ref[:, sl].astype(F32))
        kr = rope(k_ref[:, sl].astype(F32)) * (RET_DIM ** -0.5)
        vh = v_ref[:, sl]
        scores = lax.dot_general(qr.astype(BF16), kr.astype(BF16), _NT,
                                 preferred_element_type=F32)
        p = (scores * dintra_ref[h]).astype(BF16)
        y = jnp.dot(p, vh, preferred_element_type=F32)
        y = y + jnp.dot((qr * qdec_ref[:, sl]).astype(BF16), rstate[h].astype(BF16),
                        preferred_element_type=F32)
        kv = lax.dot_general((kr * kdec_ref[:, sl]).astype(BF16), vh, _TN,
                             preferred_element_type=F32)
        rstate[h] = float(chunk_decay[h]) * rstate[h] + kv
        yn = y * _rms_scale(y) * rnw_ref[:, sl]
        gh = g_ref[:, sl].astype(F32)
        out_ref[:, sl] = (yn * jax.nn.silu(gh)).astype(BF16)

    xp[8:8 + CHUNK, 0:SSD_INNER] = xs_ref[...].astype(F32)
    xp[8:8 + CHUNK, SSD_INNER:SSD_CONV_DIM] = bc_ref[...].astype(F32)
    conv = cb_ref[...] + cw_ref[SSD_CONV - 1:SSD_CONV, :] * xp[8:8 + CHUNK, :]
    for j in range(SSD_CONV - 1):
        off = 8 - (SSD_CONV - 1) + j
        conv = conv + cw_ref[j:j + 1, :] * xp[off:off + CHUNK, :]
    xp[0:8, :] = xp[CHUNK:CHUNK + 8, :]
    act = jax.nn.silu(conv)
    xs = act[:, :SSD_INNER]
    xs_t = xs.T

    dt_t = jax.nn.softplus(dt_ref[...].T[0:SSD_HEADS, :] + dtb_ref[...])
    acs_t = _cumsum_lanes(dt_t * (-jnp.exp(alog_ref[...])))
    a_last = acs_t[:, CHUNK - 1:CHUNK]
    w_t = jnp.exp(a_last - acs_t) * dt_t
    ea_t = jnp.exp(acs_t)
    cdec = jnp.broadcast_to(jnp.exp(a_last), (SSD_HEADS, CHUNK))
    acs_pad = jnp.concatenate([acs_t, jnp.zeros((CHUNK - SSD_HEADS, CHUNK), F32)], axis=0)
    acs_col = acs_pad.T
    dskip = dskip_ref[...]

    row = lax.broadcasted_iota(jnp.int32, (CHUNK, CHUNK), 0)
    col = lax.broadcasted_iota(jnp.int32, (CHUNK, CHUNK), 1)
    causal_t = col >= row

    hpg = SSD_HEADS // SSD_GROUPS
    gw = hpg * SSD_HEAD_DIM
    y_t_parts = []
    for g in range(SSD_GROUPS):
        bg = act[:, SSD_INNER + g * SSD_STATE:SSD_INNER + (g + 1) * SSD_STATE].astype(BF16)
        cbase = SSD_INNER + SSD_GROUPS * SSD_STATE
        cg = act[:, cbase + g * SSD_STATE:cbase + (g + 1) * SSD_STATE].astype(BF16)
        cb_t = lax.dot_general(bg, cg, _NT, preferred_element_type=F32)
        s_prev = sstate[g * gw:(g + 1) * gw, :]
        yo_t = lax.dot_general(s_prev.astype(BF16), cg, _NT, preferred_element_type=F32)

        def rows_of(t, hh):
            return jnp.broadcast_to(t[hh:hh + 1, :], (SSD_HEAD_DIM, CHUNK))

        xw_parts, cd_parts = [], []
        for e in range(hpg):
            hh = g * hpg + e
            xs_h = xs_t[hh * SSD_HEAD_DIM:(hh + 1) * SSD_HEAD_DIM, :]
            seg = (jnp.broadcast_to(acs_t[hh:hh + 1, :], (CHUNK, CHUNK))
                   - jnp.broadcast_to(acs_col[:, hh:hh + 1], (CHUNK, CHUNK)))
            l_t = jnp.exp(jnp.where(causal_t, seg, -jnp.inf))
            m_t = (cb_t * l_t).astype(BF16)
            xdt = (xs_h * rows_of(dt_t, hh)).astype(BF16)
            yd = jnp.dot(xdt, m_t, preferred_element_type=F32)
            yo = yo_t[e * SSD_HEAD_DIM:(e + 1) * SSD_HEAD_DIM, :] * rows_of(ea_t, hh)
            y_t_parts.append(yd + yo + rows_of(dskip, hh) * xs_h)
            xw_parts.append((xs_h * rows_of(w_t, hh)).astype(BF16))
            cd_parts.append(rows_of(cdec, hh))
        xw = jnp.concatenate(xw_parts, axis=0)
        cd = jnp.concatenate(cd_parts, axis=0)
        sstate[g * gw:(g + 1) * gw, :] = cd * s_prev + jnp.dot(xw, bg, preferred_element_type=F32)

    y = jnp.concatenate(y_t_parts, axis=0).T
    y = y * jax.nn.silu(z_ref[...].astype(F32))
    for g in range(SSD_GROUPS):
        sl = slice(g * gw, (g + 1) * gw)
        yg = y[:, sl]
        out_ref[:, RET_WIDTH + g * gw:RET_WIDTH + (g + 1) * gw] = (
            yg * _rms_scale(yg) * snw_ref[:, sl]).astype(BF16)


def _mixer(proj, dt, params):
    (ret_norm_w, conv_w, conv_b, dt_bias, a_log, d_skip, ssd_norm_w) = params
    cos, sin = _rope_tables()
    dintra, qdec, kdec = _ret_tables()

    def rows(b, c):
        return b * NUM_CHUNKS + c

    def col_block(j, width):
        return pl.BlockSpec((CHUNK, width), lambda b, c: (rows(b, c), j))

    def full(shape):
        return pl.BlockSpec(shape, lambda b, c: (0,) * len(shape))

    bcast = lambda v: jnp.broadcast_to(v.astype(F32)[:, None], (SSD_HEADS, CHUNK))
    in_specs = [
        col_block(0, RET_WIDTH), col_block(1, RET_WIDTH), col_block(2, RET_WIDTH),
        col_block(3, RET_WIDTH), col_block(4, SSD_INNER), col_block(5, SSD_INNER),
        col_block(12, 2 * SSD_GROUPS * SSD_STATE),
        pl.BlockSpec((CHUNK, DT_PAD), lambda b, c: (rows(b, c), 0)),
        pl.BlockSpec((CHUNK, RET_DIM // 2), lambda b, c: (c, 0)),
        pl.BlockSpec((CHUNK, RET_DIM // 2), lambda b, c: (c, 0)),
        full((RET_HEADS, CHUNK, CHUNK)), full((CHUNK, RET_WIDTH)), full((CHUNK, RET_WIDTH)),
        full((1, RET_WIDTH)), full((SSD_CONV, SSD_CONV_DIM)), full((1, SSD_CONV_DIM)),
        full((SSD_HEADS, CHUNK)), full((SSD_HEADS, CHUNK)), full((SSD_HEADS, CHUNK)),
        full((1, SSD_INNER)),
    ]
    return pl.pallas_call(
        _mixer_kernel,
        grid=(BATCH, NUM_CHUNKS),
        in_specs=in_specs,
        out_specs=pl.BlockSpec((CHUNK, D_MODEL), lambda b, c: (rows(b, c), 0)),
        out_shape=jax.ShapeDtypeStruct((TOKENS, D_MODEL), BF16),
        scratch_shapes=[
            pltpu.VMEM((RET_HEADS, RET_DIM, RET_DIM), F32),
            pltpu.VMEM((SSD_INNER, SSD_STATE), F32),
            pltpu.VMEM((CHUNK + 8, SSD_CONV_DIM), F32),
        ],
        compiler_params=pltpu.CompilerParams(
            dimension_semantics=("arbitrary", "arbitrary"),
            vmem_limit_bytes=VMEM_LIMIT),
        name="mixer",
    )(proj, proj, proj, proj, proj, proj, proj, dt, cos, sin, dintra, qdec, kdec,
      ret_norm_w.astype(F32)[None, :], conv_w.astype(F32), conv_b.astype(F32)[None, :],
      bcast(dt_bias), bcast(a_log), bcast(d_skip), ssd_norm_w.astype(F32)[None, :])


OUT_TM = 512


def _outproj_kernel(mix_ref, w_ref, x_ref, nw_ref, h_ref, hn_ref):
    h = x_ref[...] + jnp.dot(mix_ref[...], w_ref[...], preferred_element_type=F32)
    h_ref[...] = h
    hn_ref[...] = (h * _rms_scale(h) * nw_ref[...]).astype(BF16)


def _outproj(mix, w_out, x2d, norm_w):
    return pl.pallas_call(
        _outproj_kernel,
        grid=(TOKENS // OUT_TM,),
        in_specs=[
            pl.BlockSpec((OUT_TM, D_MODEL), lambda m: (m, 0)),
            pl.BlockSpec((D_MODEL, D_MODEL), lambda m: (0, 0)),
            pl.BlockSpec((OUT_TM, D_MODEL), lambda m: (m, 0)),
            pl.BlockSpec((1, D_MODEL), lambda m: (0, 0)),
        ],
        out_specs=[
            pl.BlockSpec((OUT_TM, D_MODEL), lambda m: (m, 0)),
            pl.BlockSpec((OUT_TM, D_MODEL), lambda m: (m, 0)),
        ],
        out_shape=[
            jax.ShapeDtypeStruct((TOKENS, D_MODEL), F32),
            jax.ShapeDtypeStruct((TOKENS, D_MODEL), BF16),
        ],
        compiler_params=pltpu.CompilerParams(
            dimension_semantics=("arbitrary",),
            vmem_limit_bytes=VMEM_LIMIT),
        name="outproj",
    )(mix, w_out, x2d, norm_w)


MLP_TM = 1024
MLP_TF = 512
MLP_ROWS = 128


def _mlp_kernel(hn_ref, wup_ref, wdn_ref, h_ref, nw_ref, out_ref):
    f = pl.program_id(1)

    @pl.when(f == 0)
    def _():
        out_ref[...] = h_ref[...]

    u = jnp.maximum(jnp.dot(hn_ref[...], wup_ref[...], preferred_element_type=F32), 0.0)
    out_ref[...] += jnp.dot((u * u).astype(BF16), wdn_ref[...], preferred_element_type=F32)

    @pl.when(f == pl.num_programs(1) - 1)
    def _():
        def body(i, carry):
            r = pl.multiple_of(i * MLP_ROWS, MLP_ROWS)
            h = out_ref[pl.ds(r, MLP_ROWS), :]
            out_ref[pl.ds(r, MLP_ROWS), :] = h * _rms_scale(h) * nw_ref[...]
            return carry
        lax.fori_loop(0, MLP_TM // MLP_ROWS, body, 0)


def _mlp(hn, w_up, w_down, h, norm_w):
    return pl.pallas_call(
        _mlp_kernel,
        grid=(TOKENS // MLP_TM, D_FF // MLP_TF),
        in_specs=[
            pl.BlockSpec((MLP_TM, D_MODEL), lambda m, f: (m, 0)),
            pl.BlockSpec((D_MODEL, MLP_TF), lambda m, f: (0, f)),
            pl.BlockSpec((MLP_TF, D_MODEL), lambda m, f: (f, 0)),
            pl.BlockSpec((MLP_TM, D_MODEL), lambda m, f: (m, 0), pipeline_mode=pl.Buffered(1)),
            pl.BlockSpec((1, D_MODEL), lambda m, f: (0, 0)),
        ],
        out_specs=pl.BlockSpec((MLP_TM, D_MODEL), lambda m, f: (m, 0)),
        out_shape=jax.ShapeDtypeStruct((TOKENS, D_MODEL), F32),
        compiler_params=pltpu.CompilerParams(
            dimension_semantics=("arbitrary", "arbitrary"),
            vmem_limit_bytes=VMEM_LIMIT),
        name="mlp",
    )(hn, w_up, w_down, h, norm_w)


def kernel(x, norm_mix_w, w_in, ret_norm_w, conv_w, conv_b, dt_bias, a_log, d_skip, ssd_norm_w,
           w_out, norm_mlp_w, w_up, w_down, norm_final_w):
    x2d = x.reshape(TOKENS, D_MODEL)
    w_main = w_in.astype(BF16)
    w_dt = jnp.pad(w_in[:, PROJ_WIDTH:], ((0, 0), (0, DT_PAD - SSD_HEADS))).astype(BF16)
    proj, dt = _inproj(x2d, norm_mix_w.astype(F32)[None, :], w_main, w_dt)
    mix = _mixer(proj, dt, (ret_norm_w, conv_w, conv_b, dt_bias, a_log, d_skip, ssd_norm_w))
    h, hn = _outproj(mix, w_out.astype(BF16), x2d, norm_mlp_w.astype(F32)[None, :])
    out = _mlp(hn, w_up.astype(BF16), w_down.astype(BF16), h, norm_final_w.astype(F32)[None, :])
    return out.reshape(BATCH, SEQ, D_MODEL)
```

```python
import functools

import numpy as np
import jax
import jax.numpy as jnp
from jax import lax
from jax.experimental import pallas as pl
from jax.experimental.pallas import tpu as pltpu

F32 = jnp.float32
BF16 = jnp.bfloat16

D_MODEL = 2048
BATCH = 4
SEQ = 2048
TOKENS = BATCH * SEQ
RET_HEADS = 4
RET_DIM = 256
RET_WIDTH = RET_HEADS * RET_DIM
ROPE_BASE = 10000.0
SSD_INNER = 1024
SSD_HEAD_DIM = 64
SSD_HEADS = 16
SSD_GROUPS = 2
SSD_STATE = 128
SSD_CONV = 4
SSD_CONV_DIM = SSD_INNER + 2 * SSD_GROUPS * SSD_STATE
CHUNK = 128
NUM_CHUNKS = SEQ // CHUNK
PROJ_WIDTH = 4 * RET_WIDTH + SSD_INNER + SSD_CONV_DIM
DT_PAD = 128
D_FF = 4 * D_MODEL
EPS = 1e-6

VMEM_LIMIT = 56 * 1024 * 1024


def _rms_scale(x):
    return lax.rsqrt(jnp.mean(x * x, axis=-1, keepdims=True) + EPS)


IN_TM = 1024
IN_TN = 512
IN_ROWS = 128


def _inproj_kernel(x_ref, nw_ref, w_ref, wdt_ref, proj_ref, dt_ref, hn_ref):
    @pl.when(pl.program_id(1) == 0)
    def _():
        def body(i, carry):
            r = pl.multiple_of(i * IN_ROWS, IN_ROWS)
            x = x_ref[pl.ds(r, IN_ROWS), :]
            hn_ref[pl.ds(r, IN_ROWS), :] = (x * _rms_scale(x) * nw_ref[...]).astype(BF16)
            return carry
        lax.fori_loop(0, IN_TM // IN_ROWS, body, 0)
        dt_ref[...] = jnp.dot(hn_ref[...], wdt_ref[...].astype(BF16),
                              preferred_element_type=F32)

    proj_ref[...] = jnp.dot(hn_ref[...], w_ref[...].astype(BF16),
                            preferred_element_type=F32).astype(BF16)


def _inproj(x2d, norm_w, w_main, w_dt):
    grid = (TOKENS // IN_TM, PROJ_WIDTH // IN_TN)
    return pl.pallas_call(
        _inproj_kernel,
        grid=grid,
        in_specs=[
            pl.BlockSpec((IN_TM, D_MODEL), lambda m, n: (m, 0)),
            pl.BlockSpec((1, D_MODEL), lambda m, n: (0, 0)),
            pl.BlockSpec((D_MODEL, IN_TN), lambda m, n: (0, n)),
            pl.BlockSpec((D_MODEL, DT_PAD), lambda m, n: (0, 0)),
        ],
        out_specs=[
            pl.BlockSpec((IN_TM, IN_TN), lambda m, n: (m, n)),
            pl.BlockSpec((IN_TM, DT_PAD), lambda m, n: (m, 0)),
        ],
        out_shape=[
            jax.ShapeDtypeStruct((TOKENS, PROJ_WIDTH), BF16),
            jax.ShapeDtypeStruct((TOKENS, DT_PAD), F32),
        ],
        scratch_shapes=[pltpu.VMEM((IN_TM, D_MODEL), BF16)],
        compiler_params=pltpu.CompilerParams(
            dimension_semantics=("arbitrary", "arbitrary"),
            vmem_limit_bytes=VMEM_LIMIT),
        name="inproj",
    )(x2d, norm_w, w_main, w_dt)


_NT = (((1,), (1,)), ((), ()))
_TN = (((0,), (0,)), ((), ()))


def _ret_gammas():
    return 1.0 - 2.0 ** (-5.0 - np.arange(RET_HEADS, dtype=np.float64))


def _ret_tables():
    lg = np.log(_ret_gammas())
    idx = np.arange(CHUNK, dtype=np.float64)
    rel = idx[:, None] - idx[None, :]
    causal = rel >= 0
    dintra = np.where(causal[None], np.exp(np.where(causal, rel, 0.0)[None] * lg[:, None, None]), 0.0)
    qdec = np.exp((idx + 1.0)[:, None] * lg[None, :])
    kdec = np.exp((CHUNK - 1.0 - idx)[:, None] * lg[None, :])
    qdec = np.repeat(qdec, RET_DIM, axis=1)
    kdec = np.repeat(kdec, RET_DIM, axis=1)
    return (jnp.asarray(dintra, F32), jnp.asarray(qdec, F32), jnp.asarray(kdec, F32))


def _rope_tables():
    half = RET_DIM // 2
    inv_freq = ROPE_BASE ** (-jnp.arange(half, dtype=F32) / half)
    ang = jnp.arange(SEQ, dtype=F32)[:, None] * inv_freq[None, :]
    return jnp.cos(ang), jnp.sin(ang)


def _cumsum_lanes(x):
    lane = lax.broadcasted_iota(jnp.int32, x.shape, 1)
    k = 1
    while k < x.shape[1]:
        x = x + jnp.where(lane >= k, pltpu.roll(x, k, axis=1), 0.0)
        k *= 2
    return x


def _mixer_kernel(q_ref, k_ref, v_ref, g_ref, z_ref, xs_ref, bc_ref, dt_ref, cos_ref, sin_ref,
                  dintra_ref, qdec_ref, kdec_ref, rnw_ref, cw_ref, cb_ref, dtb_ref, alog_ref,
                  dskip_ref, snw_ref, out_ref, rstate, sstate, xp):
    @pl.when(pl.program_id(1) == 0)
    def _():
        rstate[...] = jnp.zeros_like(rstate)
        sstate[...] = jnp.zeros_like(sstate)
        xp[0:8, :] = jnp.zeros((8, SSD_CONV_DIM), F32)

    cos = cos_ref[...]
    sin = sin_ref[...]
    half = RET_DIM // 2

    def rope(x):
        x1, x2 = x[:, :half], x[:, half:]
        return jnp.concatenate([x1 * cos - x2 * sin, x1 * sin + x2 * cos], axis=-1)

    chunk_decay = _ret_gammas() ** CHUNK
    for h in range(RET_HEADS):
        sl = slice(h * RET_DIM, (h + 1) * RET_DIM)
        qr = rope(q_ref[:, sl].astype(F32))
        kr = rope(k_ref[:, sl].astype(F32)) * (RET_DIM ** -0.5)
        vh = v_ref[:, sl]
        scores = lax.dot_general(qr.astype(BF16), kr.astype(BF16), _NT,
                                 preferred_element_type=F32)
        p = (scores * dintra_ref[h]).astype(BF16)
        y = jnp.dot(p, vh, preferred_element_type=F32)
        y = y + jnp.dot((qr * qdec_ref[:, sl]).astype(BF16), rstate[h].astype(BF16),
                        preferred_element_type=F32)
        kv = lax.dot_general((kr * kdec_ref[:, sl]).astype(BF16), vh, _TN,
                             preferred_element_type=F32)
        rstate[h] = float(chunk_decay[h]) * rstate[h] + kv
        yn = y * _rms_scale(y) * rnw_ref[:, sl]
        gh = g_ref[:, sl].astype(F32)
        out_ref[:, sl] = (yn * jax.nn.silu(gh)).astype(BF16)

    xp[8:8 + CHUNK, 0:SSD_INNER] = xs_ref[...].astype(F32)
    xp[8:8 + CHUNK, SSD_INNER:SSD_CONV_DIM] = bc_ref[...].astype(F32)
    conv = cb_ref[...] + cw_ref[SSD_CONV - 1:SSD_CONV, :] * xp[8:8 + CHUNK, :]
    for j in range(SSD_CONV - 1):
        off = 8 - (SSD_CONV - 1) + j
        conv = conv + cw_ref[j:j + 1, :] * xp[off:off + CHUNK, :]
    xp[0:8, :] = xp[CHUNK:CHUNK + 8, :]
    act = jax.nn.silu(conv)
    xs = act[:, :SSD_INNER]
    xs_t = xs.T

    dt_t = jax.nn.softplus(dt_ref[...].T[0:SSD_HEADS, :] + dtb_ref[...])
    acs_t = _cumsum_lanes(dt_t * (-jnp.exp(alog_ref[...])))
    a_last = acs_t[:, CHUNK - 1:CHUNK]
    w_t = jnp.exp(a_last - acs_t) * dt_t
    ea_t = jnp.exp(acs_t)
    cdec = jnp.broadcast_to(jnp.exp(a_last), (SSD_HEADS, CHUNK))
    acs_pad = jnp.concatenate([acs_t, jnp.zeros((CHUNK - SSD_HEADS, CHUNK), F32)], axis=0)
    acs_col = acs_pad.T
    dskip = dskip_ref[...]

    row = lax.broadcasted_iota(jnp.int32, (CHUNK, CHUNK), 0)
    col = lax.broadcasted_iota(jnp.int32, (CHUNK, CHUNK), 1)
    causal_t = col >= row

    hpg = SSD_HEADS // SSD_GROUPS
    gw = hpg * SSD_HEAD_DIM
    y_t_parts = []
    for g in range(SSD_GROUPS):
        bg = act[:, SSD_INNER + g * SSD_STATE:SSD_INNER + (g + 1) * SSD_STATE].astype(BF16)
        cbase = SSD_INNER + SSD_GROUPS * SSD_STATE
        cg = act[:, cbase + g * SSD_STATE:cbase + (g + 1) * SSD_STATE].astype(BF16)
        cb_t = lax.dot_general(bg, cg, _NT, preferred_element_type=F32)
        s_prev = sstate[g * gw:(g + 1) * gw, :]
        yo_t = lax.dot_general(s_prev.astype(BF16), cg, _NT, preferred_element_type=F32)

        def rows_of(t, hh):
            return jnp.broadcast_to(t[hh:hh + 1, :], (SSD_HEAD_DIM, CHUNK))

        xw_parts, cd_parts = [], []
        for e in range(hpg):
            hh = g * hpg + e
            xs_h = xs_t[hh * SSD_HEAD_DIM:(hh + 1) * SSD_HEAD_DIM, :]
            seg = (jnp.broadcast_to(acs_t[hh:hh + 1, :], (CHUNK, CHUNK))
                   - jnp.broadcast_to(acs_col[:, hh:hh + 1], (CHUNK, CHUNK)))
            l_t = jnp.exp(jnp.where(causal_t, seg, -jnp.inf))
            m_t = (cb_t * l_t).astype(BF16)
            xdt = (xs_h * rows_of(dt_t, hh)).astype(BF16)
            yd = jnp.dot(xdt, m_t, preferred_element_type=F32)
            yo = yo_t[e * SSD_HEAD_DIM:(e + 1) * SSD_HEAD_DIM, :] * rows_of(ea_t, hh)
            y_t_parts.append(yd + yo + rows_of(dskip, hh) * xs_h)
            xw_parts.append((xs_h * rows_of(w_t, hh)).astype(BF16))
            cd_parts.append(rows_of(cdec, hh))
        xw = jnp.concatenate(xw_parts, axis=0)
        cd = jnp.concatenate(cd_parts, axis=0)
        sstate[g * gw:(g + 1) * gw, :] = cd * s_prev + jnp.dot(xw, bg, preferred_element_type=F32)

    y = jnp.concatenate(y_t_parts, axis=0).T
    y = y * jax.nn.silu(z_ref[...].astype(F32))
    for g in range(SSD_GROUPS):
        sl = slice(g * gw, (g + 1) * gw)
        yg = y[:, sl]
        out_ref[:, RET_WIDTH + g * gw:RET_WIDTH + (g + 1) * gw] = (
            yg * _rms_scale(yg) * snw_ref[:, sl]).astype(BF16)


def _mixer(proj, dt, params):
    (ret_norm_w, conv_w, conv_b, dt_bias, a_log, d_skip, ssd_norm_w) = params
    cos, sin = _rope_tables()
    dintra, qdec, kdec = _ret_tables()

    def rows(b, c):
        return b * NUM_CHUNKS + c

    def col_block(j, width):
        return pl.BlockSpec((CHUNK, width), lambda b, c: (rows(b, c), j))

    def full(shape):
        return pl.BlockSpec(shape, lambda b, c: (0,) * len(shape))

    bcast = lambda v: jnp.broadcast_to(v.astype(F32)[:, None], (SSD_HEADS, CHUNK))
    in_specs = [
        col_block(0, RET_WIDTH), col_block(1, RET_WIDTH), col_block(2, RET_WIDTH),
        col_block(3, RET_WIDTH), col_block(4, SSD_INNER), col_block(5, SSD_INNER),
        col_block(12, 2 * SSD_GROUPS * SSD_STATE),
        pl.BlockSpec((CHUNK, DT_PAD), lambda b, c: (rows(b, c), 0)),
        pl.BlockSpec((CHUNK, RET_DIM // 2), lambda b, c: (c, 0)),
        pl.BlockSpec((CHUNK, RET_DIM // 2), lambda b, c: (c, 0)),
        full((RET_HEADS, CHUNK, CHUNK)), full((CHUNK, RET_WIDTH)), full((CHUNK, RET_WIDTH)),
        full((1, RET_WIDTH)), full((SSD_CONV, SSD_CONV_DIM)), full((1, SSD_CONV_DIM)),
        full((SSD_HEADS, CHUNK)), full((SSD_HEADS, CHUNK)), full((SSD_HEADS, CHUNK)),
        full((1, SSD_INNER)),
    ]
    return pl.pallas_call(
        _mixer_kernel,
        grid=(BATCH, NUM_CHUNKS),
        in_specs=in_specs,
        out_specs=pl.BlockSpec((CHUNK, D_MODEL), lambda b, c: (rows(b, c), 0)),
        out_shape=jax.ShapeDtypeStruct((TOKENS, D_MODEL), BF16),
        scratch_shapes=[
            pltpu.VMEM((RET_HEADS, RET_DIM, RET_DIM), F32),
            pltpu.VMEM((SSD_INNER, SSD_STATE), F32),
            pltpu.VMEM((CHUNK + 8, SSD_CONV_DIM), F32),
        ],
        compiler_params=pltpu.CompilerParams(
            dimension_semantics=("arbitrary", "arbitrary"),
            vmem_limit_bytes=VMEM_LIMIT),
        name="mixer",
    )(proj, proj, proj, proj, proj, proj, proj, dt, cos, sin, dintra, qdec, kdec,
      ret_norm_w.astype(F32)[None, :], conv_w.astype(F32), conv_b.astype(F32)[None, :],
      bcast(dt_bias), bcast(a_log), bcast(d_skip), ssd_norm_w.astype(F32)[None, :])


OUT_TM = 512


def _outproj_kernel(mix_ref, w_ref, x_ref, nw_ref, h_ref, hn_ref):
    h = x_ref[...] + jnp.dot(mix_ref[...], w_ref[...], preferred_element_type=F32)
    h_ref[...] = h
    hn_ref[...] = (h * _rms_scale(h) * nw_ref[...]).astype(BF16)


def _outproj(mix, w_out, x2d, norm_w):
    return pl.pallas_call(
        _outproj_kernel,
        grid=(TOKENS // OUT_TM,),
        in_specs=[
            pl.BlockSpec((OUT_TM, D_MODEL), lambda m: (m, 0)),
            pl.BlockSpec((D_MODEL, D_MODEL), lambda m: (0, 0)),
            pl.BlockSpec((OUT_TM, D_MODEL), lambda m: (m, 0)),
            pl.BlockSpec((1, D_MODEL), lambda m: (0, 0)),
        ],
        out_specs=[
            pl.BlockSpec((OUT_TM, D_MODEL), lambda m: (m, 0)),
            pl.BlockSpec((OUT_TM, D_MODEL), lambda m: (m, 0)),
        ],
        out_shape=[
            jax.ShapeDtypeStruct((TOKENS, D_MODEL), F32),
            jax.ShapeDtypeStruct((TOKENS, D_MODEL), BF16),
        ],
        compiler_params=pltpu.CompilerParams(
            dimension_semantics=("arbitrary",),
            vmem_limit_bytes=VMEM_LIMIT),
        name="outproj",
    )(mix, w_out, x2d, norm_w)


MLP_TM = 1024
MLP_TF = 512
MLP_NF = D_FF // MLP_TF
MLP_HROWS = MLP_TM // MLP_NF
MLP_ROWS = 128


def _mlp_kernel(hn_ref, wup_ref, wdn_ref, h_ref, nw_ref, out_ref):
    f = pl.program_id(1)

    @pl.when(f == 0)
    def _():
        out_ref[...] = jnp.zeros_like(out_ref)

    u = jnp.maximum(jnp.dot(hn_ref[...], wup_ref[...].astype(BF16),
                            preferred_element_type=F32), 0.0)
    out_ref[...] += jnp.dot((u * u).astype(BF16), wdn_ref[...].astype(BF16),
                            preferred_element_type=F32)
    r = pl.multiple_of(f * MLP_HROWS, MLP_HROWS)
    out_ref[pl.ds(r, MLP_HROWS), :] += h_ref[...]

    @pl.when(f == MLP_NF - 1)
    def _():
        def body(i, carry):
            r = pl.multiple_of(i * MLP_ROWS, MLP_ROWS)
            h = out_ref[pl.ds(r, MLP_ROWS), :]
            out_ref[pl.ds(r, MLP_ROWS), :] = h * _rms_scale(h) * nw_ref[...]
            return carry
        lax.fori_loop(0, MLP_TM // MLP_ROWS, body, 0)


def _mlp(hn, w_up, w_down, h, norm_w):
    return pl.pallas_call(
        _mlp_kernel,
        grid=(TOKENS // MLP_TM, MLP_NF),
        in_specs=[
            pl.BlockSpec((MLP_TM, D_MODEL), lambda m, f: (m, 0)),
            pl.BlockSpec((D_MODEL, MLP_TF), lambda m, f: (0, f)),
            pl.BlockSpec((MLP_TF, D_MODEL), lambda m, f: (f, 0)),
            pl.BlockSpec((MLP_HROWS, D_MODEL), lambda m, f: (m * MLP_NF + f, 0)),
            pl.BlockSpec((1, D_MODEL), lambda m, f: (0, 0)),
        ],
        out_specs=pl.BlockSpec((MLP_TM, D_MODEL), lambda m, f: (m, 0)),
        out_shape=jax.ShapeDtypeStruct((TOKENS, D_MODEL), F32),
        compiler_params=pltpu.CompilerParams(
            dimension_semantics=("arbitrary", "arbitrary"),
            vmem_limit_bytes=VMEM_LIMIT),
        name="mlp",
    )(hn, w_up, w_down, h, norm_w)


def kernel(x, norm_mix_w, w_in, ret_norm_w, conv_w, conv_b, dt_bias, a_log, d_skip, ssd_norm_w,
           w_out, norm_mlp_w, w_up, w_down, norm_final_w):
    x2d = x.reshape(TOKENS, D_MODEL)
    w_dt = jnp.pad(w_in[:, PROJ_WIDTH:], ((0, 0), (0, DT_PAD - SSD_HEADS)))
    proj, dt = _inproj(x2d, norm_mix_w.astype(F32)[None, :], w_in, w_dt)
    mix = _mixer(proj, dt, (ret_norm_w, conv_w, conv_b, dt_bias, a_log, d_skip, ssd_norm_w))
    h, hn = _outproj(mix, w_out.astype(BF16), x2d, norm_mlp_w.astype(F32)[None, :])
    out = _mlp(hn, w_up, w_down, h, norm_final_w.astype(F32)[None, :])
    return out.reshape(BATCH, SEQ, D_MODEL)
```

```python
import functools

import numpy as np
import jax
import jax.numpy as jnp
from jax import lax
from jax.experimental import pallas as pl
from jax.experimental.pallas import tpu as pltpu

F32 = jnp.float32
BF16 = jnp.bfloat16

D_MODEL = 2048
BATCH = 4
SEQ = 2048
TOKENS = BATCH * SEQ
RET_HEADS = 4
RET_DIM = 256
RET_WIDTH = RET_HEADS * RET_DIM
ROPE_BASE = 10000.0
SSD_INNER = 1024
SSD_HEAD_DIM = 64
SSD_HEADS = 16
SSD_GROUPS = 2
SSD_STATE = 128
SSD_CONV = 4
SSD_CONV_DIM = SSD_INNER + 2 * SSD_GROUPS * SSD_STATE
CHUNK = 128
NUM_CHUNKS = SEQ // CHUNK
PROJ_WIDTH = 4 * RET_WIDTH + SSD_INNER + SSD_CONV_DIM
DT_PAD = 128
D_FF = 4 * D_MODEL
EPS = 1e-6

VMEM_LIMIT = 56 * 1024 * 1024

_NT = (((1,), (1,)), ((), ()))
_TN = (((0,), (0,)), ((), ()))


def _rms_scale(x):
    return lax.rsqrt(jnp.mean(x * x, axis=-1, keepdims=True) + EPS)


IN_TM = 1024
IN_TN = 512
IN_ROWS = 128


def _inproj_kernel(x_ref, nw_ref, w_ref, wdt_ref, proj_ref, dt_ref, hn_ref):
    @pl.when(pl.program_id(1) == 0)
    def _():
        def body(i, carry):
            r = pl.multiple_of(i * IN_ROWS, IN_ROWS)
            x = x_ref[pl.ds(r, IN_ROWS), :]
            hn_ref[pl.ds(r, IN_ROWS), :] = (x * _rms_scale(x) * nw_ref[...]).astype(BF16)
            return carry
        lax.fori_loop(0, IN_TM // IN_ROWS, body, 0)
        dt_ref[...] = lax.dot_general(hn_ref[...], wdt_ref[...].astype(BF16), _NT,
                                      preferred_element_type=F32)

    proj_ref[...] = lax.dot_general(hn_ref[...], w_ref[...].astype(BF16), _NT,
                                    preferred_element_type=F32).astype(BF16)


def _inproj(x2d, norm_w, w_main, w_dt):
    grid = (TOKENS // IN_TM, PROJ_WIDTH // IN_TN)
    return pl.pallas_call(
        _inproj_kernel,
        grid=grid,
        in_specs=[
            pl.BlockSpec((IN_TM, D_MODEL), lambda m, n: (m, 0)),
            pl.BlockSpec((1, D_MODEL), lambda m, n: (0, 0)),
            pl.BlockSpec((IN_TN, D_MODEL), lambda m, n: (n, 0)),
            pl.BlockSpec((DT_PAD, D_MODEL), lambda m, n: (0, 0)),
        ],
        out_specs=[
            pl.BlockSpec((IN_TM, IN_TN), lambda m, n: (m, n)),
            pl.BlockSpec((IN_TM, DT_PAD), lambda m, n: (m, 0)),
        ],
        out_shape=[
            jax.ShapeDtypeStruct((TOKENS, PROJ_WIDTH), BF16),
            jax.ShapeDtypeStruct((TOKENS, DT_PAD), F32),
        ],
        scratch_shapes=[pltpu.VMEM((IN_TM, D_MODEL), BF16)],
        compiler_params=pltpu.CompilerParams(
            dimension_semantics=("arbitrary", "arbitrary"),
            vmem_limit_bytes=VMEM_LIMIT),
        name="inproj",
    )(x2d, norm_w, w_main, w_dt)


def _ret_gammas():
    return 1.0 - 2.0 ** (-5.0 - np.arange(RET_HEADS, dtype=np.float64))


def _ret_tables():
    lg = np.log(_ret_gammas())
    idx = np.arange(CHUNK, dtype=np.float64)
    rel = idx[:, None] - idx[None, :]
    causal = rel >= 0
    dintra = np.where(causal[None], np.exp(np.where(causal, rel, 0.0)[None] * lg[:, None, None]), 0.0)
    qdec = np.exp((idx + 1.0)[:, None] * lg[None, :])
    kdec = np.exp((CHUNK - 1.0 - idx)[:, None] * lg[None, :])
    qdec = np.repeat(qdec, RET_DIM, axis=1)
    kdec = np.repeat(kdec, RET_DIM, axis=1)
    return (jnp.asarray(dintra, F32), jnp.asarray(qdec, F32), jnp.asarray(kdec, F32))


def _rope_tables():
    half = RET_DIM // 2
    inv_freq = ROPE_BASE ** (-jnp.arange(half, dtype=F32) / half)
    ang = jnp.arange(SEQ, dtype=F32)[:, None] * inv_freq[None, :]
    return jnp.cos(ang), jnp.sin(ang)


def _cumsum_lanes(x):
    lane = lax.broadcasted_iota(jnp.int32, x.shape, 1)
    k = 1
    while k < x.shape[1]:
        x = x + jnp.where(lane >= k, pltpu.roll(x, k, axis=1), 0.0)
        k *= 2
    return x


def _mixer_kernel(q_ref, k_ref, v_ref, g_ref, z_ref, xs_ref, bc_ref, dt_ref, cos_ref, sin_ref,
                  dintra_ref, qdec_ref, kdec_ref, rnw_ref, cw_ref, cb_ref, dtb_ref, alog_ref,
                  dskip_ref, snw_ref, out_ref, rstate, sstate, xp):
    @pl.when(pl.program_id(1) == 0)
    def _():
        rstate[...] = jnp.zeros_like(rstate)
        sstate[...] = jnp.zeros_like(sstate)
        xp[0:8, :] = jnp.zeros((8, SSD_CONV_DIM), F32)

    cos = cos_ref[...]
    sin = sin_ref[...]
    half = RET_DIM // 2

    def rope(x):
        x1, x2 = x[:, :half], x[:, half:]
        return jnp.concatenate([x1 * cos - x2 * sin, x1 * sin + x2 * cos], axis=-1)

    chunk_decay = _ret_gammas() ** CHUNK
    for h in range(RET_HEADS):
        sl = slice(h * RET_DIM, (h + 1) * RET_DIM)
        qr = rope(q_ref[:, sl].astype(F32))
        kr = rope(k_ref[:, sl].astype(F32)) * (RET_DIM ** -0.5)
        vh = v_ref[:, sl]
        scores = lax.dot_general(qr.astype(BF16), kr.astype(BF16), _NT,
                                 preferred_element_type=F32)
        p = (scores * dintra_ref[h]).astype(BF16)
        y = jnp.dot(p, vh, preferred_element_type=F32)
        y = y + jnp.dot((qr * qdec_ref[:, sl]).astype(BF16), rstate[h].astype(BF16),
                        preferred_element_type=F32)
        kv = lax.dot_general((kr * kdec_ref[:, sl]).astype(BF16), vh, _TN,
                             preferred_element_type=F32)
        rstate[h] = float(chunk_decay[h]) * rstate[h] + kv
        yn = y * _rms_scale(y) * rnw_ref[:, sl]
        gh = g_ref[:, sl].astype(F32)
        out_ref[:, sl] = (yn * jax.nn.silu(gh)).astype(BF16)

    xp[8:8 + CHUNK, 0:SSD_INNER] = xs_ref[...].astype(F32)
    xp[8:8 + CHUNK, SSD_INNER:SSD_CONV_DIM] = bc_ref[...].astype(F32)
    conv = cb_ref[...] + cw_ref[SSD_CONV - 1:SSD_CONV, :] * xp[8:8 + CHUNK, :]
    for j in range(SSD_CONV - 1):
        off = 8 - (SSD_CONV - 1) + j
        conv = conv + cw_ref[j:j + 1, :] * xp[off:off + CHUNK, :]
    xp[0:8, :] = xp[CHUNK:CHUNK + 8, :]
    act = jax.nn.silu(conv)
    xs = act[:, :SSD_INNER]
    xs_t = xs.T

    dt_t = jax.nn.softplus(dt_ref[...].T[0:SSD_HEADS, :] + dtb_ref[...])
    acs_t = _cumsum_lanes(dt_t * (-jnp.exp(alog_ref[...])))
    a_last = acs_t[:, CHUNK - 1:CHUNK]
    w_t = jnp.exp(a_last - acs_t) * dt_t
    ea_t = jnp.exp(acs_t)
    cdec = jnp.broadcast_to(jnp.exp(a_last), (SSD_HEADS, CHUNK))
    acs_pad = jnp.concatenate([acs_t, jnp.zeros((CHUNK - SSD_HEADS, CHUNK), F32)], axis=0)
    acs_col = acs_pad.T
    dskip = dskip_ref[...]

    row = lax.broadcasted_iota(jnp.int32, (CHUNK, CHUNK), 0)
    col = lax.broadcasted_iota(jnp.int32, (CHUNK, CHUNK), 1)
    causal_t = col >= row

    hpg = SSD_HEADS // SSD_GROUPS
    gw = hpg * SSD_HEAD_DIM
    y_t_parts = []
    for g in range(SSD_GROUPS):
        bg = act[:, SSD_INNER + g * SSD_STATE:SSD_INNER + (g + 1) * SSD_STATE].astype(BF16)
        cbase = SSD_INNER + SSD_GROUPS * SSD_STATE
        cg = act[:, cbase + g * SSD_STATE:cbase + (g + 1) * SSD_STATE].astype(BF16)
        cb_t = lax.dot_general(bg, cg, _NT, preferred_element_type=F32)
        s_prev = sstate[g * gw:(g + 1) * gw, :]
        yo_t = lax.dot_general(s_prev.astype(BF16), cg, _NT, preferred_element_type=F32)

        def rows_of(t, hh):
            return jnp.broadcast_to(t[hh:hh + 1, :], (SSD_HEAD_DIM, CHUNK))

        xw_parts, cd_parts = [], []
        for e in range(hpg):
            hh = g * hpg + e
            xs_h = xs_t[hh * SSD_HEAD_DIM:(hh + 1) * SSD_HEAD_DIM, :]
            seg = (jnp.broadcast_to(acs_t[hh:hh + 1, :], (CHUNK, CHUNK))
                   - jnp.broadcast_to(acs_col[:, hh:hh + 1], (CHUNK, CHUNK)))
            l_t = jnp.exp(jnp.where(causal_t, seg, -jnp.inf))
            m_t = (cb_t * l_t).astype(BF16)
            xdt = (xs_h * rows_of(dt_t, hh)).astype(BF16)
            yd = jnp.dot(xdt, m_t, preferred_element_type=F32)
            yo = yo_t[e * SSD_HEAD_DIM:(e + 1) * SSD_HEAD_DIM, :] * rows_of(ea_t, hh)
            y_t_parts.append(yd + yo + rows_of(dskip, hh) * xs_h)
            xw_parts.append((xs_h * rows_of(w_t, hh)).astype(BF16))
            cd_parts.append(rows_of(cdec, hh))
        xw = jnp.concatenate(xw_parts, axis=0)
        cd = jnp.concatenate(cd_parts, axis=0)
        sstate[g * gw:(g + 1) * gw, :] = cd * s_prev + jnp.dot(xw, bg, preferred_element_type=F32)

    y = jnp.concatenate(y_t_parts, axis=0).T
    y = y * jax.nn.silu(z_ref[...].astype(F32))
    for g in range(SSD_GROUPS):
        sl = slice(g * gw, (g + 1) * gw)
        yg = y[:, sl]
        out_ref[:, RET_WIDTH + g * gw:RET_WIDTH + (g + 1) * gw] = (
            yg * _rms_scale(yg) * snw_ref[:, sl]).astype(BF16)


def _mixer(proj, dt, params):
    (ret_norm_w, conv_w, conv_b, dt_bias, a_log, d_skip, ssd_norm_w) = params
    cos, sin = _rope_tables()
    dintra, qdec, kdec = _ret_tables()

    def rows(b, c):
        return b * NUM_CHUNKS + c

    def col_block(j, width):
        return pl.BlockSpec((CHUNK, width), lambda b, c: (rows(b, c), j))

    def full(shape):
        return pl.BlockSpec(shape, lambda b, c: (0,) * len(shape))

    bcast = lambda v: jnp.broadcast_to(v.astype(F32)[:, None], (SSD_HEADS, CHUNK))
    in_specs = [
        col_block(0, RET_WIDTH), col_block(1, RET_WIDTH), col_block(2, RET_WIDTH),
        col_block(3, RET_WIDTH), col_block(4, SSD_INNER), col_block(5, SSD_INNER),
        col_block(12, 2 * SSD_GROUPS * SSD_STATE),
        pl.BlockSpec((CHUNK, DT_PAD), lambda b, c: (rows(b, c), 0)),
        pl.BlockSpec((CHUNK, RET_DIM // 2), lambda b, c: (c, 0)),
        pl.BlockSpec((CHUNK, RET_DIM // 2), lambda b, c: (c, 0)),
        full((RET_HEADS, CHUNK, CHUNK)), full((CHUNK, RET_WIDTH)), full((CHUNK, RET_WIDTH)),
        full((1, RET_WIDTH)), full((SSD_CONV, SSD_CONV_DIM)), full((1, SSD_CONV_DIM)),
        full((SSD_HEADS, CHUNK)), full((SSD_HEADS, CHUNK)), full((SSD_HEADS, CHUNK)),
        full((1, SSD_INNER)),
    ]
    return pl.pallas_call(
        _mixer_kernel,
        grid=(BATCH, NUM_CHUNKS),
        in_specs=in_specs,
        out_specs=pl.BlockSpec((CHUNK, D_MODEL), lambda b, c: (rows(b, c), 0)),
        out_shape=jax.ShapeDtypeStruct((TOKENS, D_MODEL), BF16),
        scratch_shapes=[
            pltpu.VMEM((RET_HEADS, RET_DIM, RET_DIM), F32),
            pltpu.VMEM((SSD_INNER, SSD_STATE), F32),
            pltpu.VMEM((CHUNK + 8, SSD_CONV_DIM), F32),
        ],
        compiler_params=pltpu.CompilerParams(
            dimension_semantics=("arbitrary", "arbitrary"),
            vmem_limit_bytes=VMEM_LIMIT),
        name="mixer",
    )(proj, proj, proj, proj, proj, proj, proj, dt, cos, sin, dintra, qdec, kdec,
      ret_norm_w.astype(F32)[None, :], conv_w.astype(F32), conv_b.astype(F32)[None, :],
      bcast(dt_bias), bcast(a_log), bcast(d_skip), ssd_norm_w.astype(F32)[None, :])


OUT_TM = 512


def _outproj_kernel(mix_ref, w_ref, x_ref, nw_ref, h_ref, hn_ref):
    h = x_ref[...] + jnp.dot(mix_ref[...], w_ref[...].astype(BF16),
                             preferred_element_type=F32)
    h_ref[...] = h
    hn_ref[...] = (h * _rms_scale(h) * nw_ref[...]).astype(BF16)


def _outproj(mix, w_out, x2d, norm_w):
    return pl.pallas_call(
        _outproj_kernel,
        grid=(TOKENS // OUT_TM,),
        in_specs=[
            pl.BlockSpec((OUT_TM, D_MODEL), lambda m: (m, 0)),
            pl.BlockSpec((D_MODEL, D_MODEL), lambda m: (0, 0), pipeline_mode=pl.Buffered(1)),
            pl.BlockSpec((OUT_TM, D_MODEL), lambda m: (m, 0)),
            pl.BlockSpec((1, D_MODEL), lambda m: (0, 0)),
        ],
        out_specs=[
            pl.BlockSpec((OUT_TM, D_MODEL), lambda m: (m, 0)),
            pl.BlockSpec((OUT_TM, D_MODEL), lambda m: (m, 0)),
        ],
        out_shape=[
            jax.ShapeDtypeStruct((TOKENS, D_MODEL), F32),
            jax.ShapeDtypeStruct((TOKENS, D_MODEL), BF16),
        ],
        compiler_params=pltpu.CompilerParams(
            dimension_semantics=("arbitrary",),
            vmem_limit_bytes=VMEM_LIMIT),
        name="outproj",
    )(mix, w_out, x2d, norm_w)


MLP_TM = 1024
MLP_TF = 512
MLP_NF = D_FF // MLP_TF
MLP_HROWS = MLP_TM // MLP_NF
MLP_ROWS = 128


def _mlp_kernel(hn_ref, wup_ref, wdn_ref, h_ref, nw_ref, out_ref):
    f = pl.program_id(1)

    @pl.when(f == 0)
    def _():
        out_ref[...] = jnp.zeros_like(out_ref)

    u = jnp.maximum(jnp.dot(hn_ref[...], wup_ref[...].astype(BF16),
                            preferred_element_type=F32), 0.0)
    out_ref[...] += jnp.dot((u * u).astype(BF16), wdn_ref[...].astype(BF16),
                            preferred_element_type=F32)
    r = pl.multiple_of(f * MLP_HROWS, MLP_HROWS)
    out_ref[pl.ds(r, MLP_HROWS), :] += h_ref[...]

    @pl.when(f == MLP_NF - 1)
    def _():
        def body(i, carry):
            r = pl.multiple_of(i * MLP_ROWS, MLP_ROWS)
            h = out_ref[pl.ds(r, MLP_ROWS), :]
            out_ref[pl.ds(r, MLP_ROWS), :] = h * _rms_scale(h) * nw_ref[...]
            return carry
        lax.fori_loop(0, MLP_TM // MLP_ROWS, body, 0)


def _mlp(hn, w_up, w_down, h, norm_w):
    return pl.pallas_call(
        _mlp_kernel,
        grid=(TOKENS // MLP_TM, MLP_NF),
        in_specs=[
            pl.BlockSpec((MLP_TM, D_MODEL), lambda m, f: (m, 0)),
            pl.BlockSpec((D_MODEL, MLP_TF), lambda m, f: (0, f)),
            pl.BlockSpec((MLP_TF, D_MODEL), lambda m, f: (f, 0)),
            pl.BlockSpec((MLP_HROWS, D_MODEL), lambda m, f: (m * MLP_NF + f, 0)),
            pl.BlockSpec((1, D_MODEL), lambda m, f: (0, 0)),
        ],
        out_specs=pl.BlockSpec((MLP_TM, D_MODEL), lambda m, f: (m, 0)),
        out_shape=jax.ShapeDtypeStruct((TOKENS, D_MODEL), F32),
        compiler_params=pltpu.CompilerParams(
            dimension_semantics=("arbitrary", "arbitrary"),
            vmem_limit_bytes=VMEM_LIMIT),
        name="mlp",
    )(hn, w_up, w_down, h, norm_w)


def kernel(x, norm_mix_w, w_in, ret_norm_w, conv_w, conv_b, dt_bias, a_log, d_skip, ssd_norm_w,
           w_out, norm_mlp_w, w_up, w_down, norm_final_w):
    x2d = x.reshape(TOKENS, D_MODEL)
    w_in_t = w_in.T
    w_dt_t = jnp.pad(w_in_t[PROJ_WIDTH:, :], ((0, DT_PAD - SSD_HEADS), (0, 0)))
    proj, dt = _inproj(x2d, norm_mix_w.astype(F32)[None, :], w_in_t, w_dt_t)
    mix = _mixer(proj, dt, (ret_norm_w, conv_w, conv_b, dt_bias, a_log, d_skip, ssd_norm_w))
    h, hn = _outproj(mix, w_out, x2d, norm_mlp_w.astype(F32)[None, :])
    out = _mlp(hn, w_up, w_down, h, norm_final_w.astype(F32)[None, :])
    return out.reshape(BATCH, SEQ, D_MODEL)
```

```python
import functools

import numpy as np
import jax
import jax.numpy as jnp
from jax import lax
from jax.experimental import pallas as pl
from jax.experimental.pallas import tpu as pltpu

F32 = jnp.float32
BF16 = jnp.bfloat16

D_MODEL = 2048
BATCH = 4
SEQ = 2048
TOKENS = BATCH * SEQ
RET_HEADS = 4
RET_DIM = 256
RET_WIDTH = RET_HEADS * RET_DIM
ROPE_BASE = 10000.0
SSD_INNER = 1024
SSD_HEAD_DIM = 64
SSD_HEADS = 16
SSD_GROUPS = 2
SSD_STATE = 128
SSD_CONV = 4
SSD_CONV_DIM = SSD_INNER + 2 * SSD_GROUPS * SSD_STATE
CHUNK = 128
NUM_CHUNKS = SEQ // CHUNK
PROJ_WIDTH = 4 * RET_WIDTH + SSD_INNER + SSD_CONV_DIM
DT_PAD = 128
D_FF = 4 * D_MODEL
EPS = 1e-6

VMEM_LIMIT = 56 * 1024 * 1024

_NT = (((1,), (1,)), ((), ()))
_TN = (((0,), (0,)), ((), ()))


def _rms_scale(x):
    return lax.rsqrt(jnp.mean(x * x, axis=-1, keepdims=True) + EPS)


def _ret_gammas():
    return 1.0 - 2.0 ** (-5.0 - np.arange(RET_HEADS, dtype=np.float64))


def _ret_tables():
    lg = np.log(_ret_gammas())
    idx = np.arange(CHUNK, dtype=np.float64)
    rel = idx[:, None] - idx[None, :]
    causal = rel >= 0
    dintra = np.where(causal[None], np.exp(np.where(causal, rel, 0.0)[None] * lg[:, None, None]), 0.0)
    qdec = np.exp((idx + 1.0)[:, None] * lg[None, :])
    kdec = np.exp((CHUNK - 1.0 - idx)[:, None] * lg[None, :])
    qkdec = np.concatenate([np.repeat(qdec, RET_DIM, axis=1), np.repeat(kdec, RET_DIM, axis=1)], axis=1)
    return jnp.asarray(dintra, F32), jnp.asarray(qkdec, F32)


def _rope_tables():
    half = RET_DIM // 2
    inv_freq = ROPE_BASE ** (-jnp.arange(half, dtype=F32) / half)
    ang = jnp.arange(SEQ, dtype=F32)[:, None] * inv_freq[None, :]
    return jnp.cos(ang), jnp.sin(ang)


IN_TM = 1024
IN_TN = 512
IN_ROWS = 128
IN_N_ROPE = 2 * RET_WIDTH // IN_TN
IN_N_K = RET_WIDTH // IN_TN
IN_N_GATE = (3 * RET_WIDTH // IN_TN, (4 * RET_WIDTH + SSD_INNER) // IN_TN)


def _inproj_kernel(x_ref, nw_ref, w_ref, wdt_ref, cos_ref, sin_ref, dec_ref, cw_ref, cb_ref,
                   proj_ref, qkd_ref, dt_ref, hn_ref, cbuf_ref, carry_ref):
    n = pl.program_id(1)

    @pl.when(n == 0)
    def _():
        def body(i, carry):
            r = pl.multiple_of(i * IN_ROWS, IN_ROWS)
            x = x_ref[pl.ds(r, IN_ROWS), :]
            hn_ref[pl.ds(r, IN_ROWS), :] = (x * _rms_scale(x) * nw_ref[...]).astype(BF16)
            return carry
        lax.fori_loop(0, IN_TM // IN_ROWS, body, 0)
        dt_ref[...] = lax.dot_general(hn_ref[...], wdt_ref[...].astype(BF16), _NT,
                                      preferred_element_type=F32)

    def project():
        return lax.dot_general(hn_ref[...], w_ref[...].astype(BF16), _NT,
                               preferred_element_type=F32)

    @pl.when(n < IN_N_ROPE)
    def _():
        a = project() * jnp.where(n >= IN_N_K, RET_DIM ** -0.5, 1.0)
        cos, sin = cos_ref[...], sin_ref[...]
        half = RET_DIM // 2
        parts = []
        for hh in range(IN_TN // RET_DIM):
            x1 = a[:, hh * RET_DIM:hh * RET_DIM + half]
            x2 = a[:, hh * RET_DIM + half:(hh + 1) * RET_DIM]
            parts += [x1 * cos - x2 * sin, x1 * sin + x2 * cos]
        r = jnp.concatenate(parts, axis=-1)
        proj_ref[...] = r.astype(BF16)
        rd = r.reshape(IN_TM // CHUNK, CHUNK, IN_TN) * dec_ref[...][None]
        qkd_ref[...] = rd.reshape(IN_TM, IN_TN).astype(BF16)

    @pl.when((n >= IN_N_GATE[0]) & (n < IN_N_GATE[1]))
    def _():
        proj_ref[...] = jax.nn.silu(project()).astype(BF16)

    @pl.when((n >= IN_N_ROPE) & (n < IN_N_GATE[0]))
    def _():
        proj_ref[...] = project().astype(BF16)

    @pl.when(n >= IN_N_GATE[1])
    def _():
        a = project()
        j = n - IN_N_GATE[1]
        seq_start = (pl.program_id(0) % (SEQ // IN_TM)) == 0
        cbuf_ref[0:8, :] = jnp.where(seq_start, 0.0, carry_ref[j])
        cbuf_ref[8:8 + IN_TM, :] = a
        carry_ref[j] = a[IN_TM - 8:IN_TM, :]
        conv = cb_ref[...] + cw_ref[SSD_CONV - 1:SSD_CONV, :] * a
        for t in range(SSD_CONV - 1):
            off = 8 - (SSD_CONV - 1) + t
            conv = conv + cw_ref[t:t + 1, :] * cbuf_ref[off:off + IN_TM, :]
        proj_ref[...] = jax.nn.silu(conv).astype(BF16)


def _inproj(x2d, norm_w, w_main, w_dt, conv_w, conv_b):
    cos, sin = _rope_tables()
    _, qkdec = _ret_tables()
    grid = (TOKENS // IN_TM, PROJ_WIDTH // IN_TN)
    seq_tiles = SEQ // IN_TM
    n_conv = SSD_CONV_DIM // IN_TN
    rope_tile = lambda m, n: (m, jnp.minimum(n, IN_N_ROPE - 1))
    conv_tile = lambda m, n: (0, jnp.maximum(n - IN_N_GATE[1], 0))
    return pl.pallas_call(
        _inproj_kernel,
        grid=grid,
        in_specs=[
            pl.BlockSpec((IN_TM, D_MODEL), lambda m, n: (m, 0)),
            pl.BlockSpec((1, D_MODEL), lambda m, n: (0, 0)),
            pl.BlockSpec((IN_TN, D_MODEL), lambda m, n: (n, 0)),
            pl.BlockSpec((DT_PAD, D_MODEL), lambda m, n: (0, 0)),
            pl.BlockSpec((IN_TM, RET_DIM // 2), lambda m, n: (m % seq_tiles, 0)),
            pl.BlockSpec((IN_TM, RET_DIM // 2), lambda m, n: (m % seq_tiles, 0)),
            pl.BlockSpec((CHUNK, IN_TN), lambda m, n: (0, jnp.minimum(n, IN_N_ROPE - 1))),
            pl.BlockSpec((SSD_CONV, IN_TN), conv_tile),
            pl.BlockSpec((1, IN_TN), conv_tile),
        ],
        out_specs=[
            pl.BlockSpec((IN_TM, IN_TN), lambda m, n: (m, n)),
            pl.BlockSpec((IN_TM, IN_TN), rope_tile),
            pl.BlockSpec((IN_TM, DT_PAD), lambda m, n: (m, 0)),
        ],
        out_shape=[
            jax.ShapeDtypeStruct((TOKENS, PROJ_WIDTH), BF16),
            jax.ShapeDtypeStruct((TOKENS, 2 * RET_WIDTH), BF16),
            jax.ShapeDtypeStruct((TOKENS, DT_PAD), F32),
        ],
        scratch_shapes=[
            pltpu.VMEM((IN_TM, D_MODEL), BF16),
            pltpu.VMEM((IN_TM + 8, IN_TN), F32),
            pltpu.VMEM((n_conv, 8, IN_TN), F32),
        ],
        compiler_params=pltpu.CompilerParams(
            dimension_semantics=("arbitrary", "arbitrary"),
            vmem_limit_bytes=VMEM_LIMIT),
        name="inproj",
    )(x2d, norm_w, w_main, w_dt, cos, sin, qkdec, conv_w, conv_b)


def _cumsum_lanes(x):
    lane = lax.broadcasted_iota(jnp.int32, x.shape, 1)
    k = 1
    while k < x.shape[1]:
        x = x + jnp.where(lane >= k, pltpu.roll(x, k, axis=1), 0.0)
        k *= 2
    return x


MIX_NB = 2
MIX_N_BATCHED = 10


def _mixer_kernel(*refs):
    ins, consts = refs[:MIX_N_BATCHED], refs[MIX_N_BATCHED:-3]
    out_ref, rstate, sstate = refs[-3:]

    @pl.when(pl.program_id(1) == 0)
    def _():
        rstate[...] = jnp.zeros_like(rstate)
        sstate[...] = jnp.zeros_like(sstate)

    chains = [_mixer_stages(*[r.at[bi] for r in ins], *consts,
                            out_ref.at[bi], rstate.at[bi], sstate.at[bi])
              for bi in range(MIX_NB)]
    for _ in range(MIX_STAGES):
        for chain in chains:
            next(chain)


MIX_STAGES = 5


def _mixer_stages(q_ref, k_ref, v_ref, g_ref, z_ref, xs_ref, bc_ref, qd_ref, kd_ref, dt_ref,
                  dintra_ref, rnw_ref, dtb_ref, alog_ref,
                  dskip_ref, snw_ref, out_ref, rstate, sstate):
    hpg = SSD_HEADS // SSD_GROUPS
    gw = hpg * SSD_HEAD_DIM
    cbase = SSD_GROUPS * SSD_STATE
    heads = [slice(h * RET_DIM, (h + 1) * RET_DIM) for h in range(RET_HEADS)]
    chunk_decay = _ret_gammas() ** CHUNK

    def rows_of(t, hh):
        return jnp.broadcast_to(t[hh:hh + 1, :], (SSD_HEAD_DIM, CHUNK))

    dt_t = jax.nn.softplus(dt_ref[...].T[0:SSD_HEADS, :] + dtb_ref[...])
    acs_t = _cumsum_lanes(dt_t * (-jnp.exp(alog_ref[...])))
    a_last = acs_t[:, CHUNK - 1:CHUNK]
    w_t = jnp.exp(a_last - acs_t) * dt_t
    ea_t = jnp.exp(acs_t)
    cdec = jnp.broadcast_to(jnp.exp(a_last), (SSD_HEADS, CHUNK))
    acs_pad = jnp.concatenate([acs_t, jnp.zeros((CHUNK - SSD_HEADS, CHUNK), F32)], axis=0)
    acs_col = acs_pad.T
    dskip = dskip_ref[...]
    yield

    scores, ycross, kv = [], [], []
    for h, sl in enumerate(heads):
        scores.append(lax.dot_general(q_ref[:, sl], k_ref[:, sl], _NT, preferred_element_type=F32))
        ycross.append(jnp.dot(qd_ref[:, sl], rstate[h].astype(BF16), preferred_element_type=F32))
        kv.append(lax.dot_general(kd_ref[:, sl], v_ref[:, sl], _TN, preferred_element_type=F32))
    xs_t = xs_ref[...].astype(F32).T
    bgs, cgs, cb_ts, yo_ts, s_prevs = [], [], [], [], []
    for g in range(SSD_GROUPS):
        bg = bc_ref[:, g * SSD_STATE:(g + 1) * SSD_STATE]
        cg = bc_ref[:, cbase + g * SSD_STATE:cbase + (g + 1) * SSD_STATE]
        s_prev = sstate[g * gw:(g + 1) * gw, :]
        bgs.append(bg)
        cgs.append(cg)
        s_prevs.append(s_prev)
        cb_ts.append(lax.dot_general(bg, cg, _NT, preferred_element_type=F32))
        yo_ts.append(lax.dot_general(s_prev.astype(BF16), cg, _NT,
                                     preferred_element_type=F32))
    yield

    ps = [(scores[h] * dintra_ref[h]).astype(BF16) for h in range(RET_HEADS)]
    for h in range(RET_HEADS):
        rstate[h] = float(chunk_decay[h]) * rstate[h] + kv[h]
    row = lax.broadcasted_iota(jnp.int32, (CHUNK, CHUNK), 0)
    col = lax.broadcasted_iota(jnp.int32, (CHUNK, CHUNK), 1)
    causal_t = col >= row
    m_ts, xdts, xws = [], [], []
    for hh in range(SSD_HEADS):
        xs_h = xs_t[hh * SSD_HEAD_DIM:(hh + 1) * SSD_HEAD_DIM, :]
        seg = (jnp.broadcast_to(acs_t[hh:hh + 1, :], (CHUNK, CHUNK))
               - jnp.broadcast_to(acs_col[:, hh:hh + 1], (CHUNK, CHUNK)))
        l_t = jnp.exp(jnp.where(causal_t, seg, -jnp.inf))
        m_ts.append((cb_ts[hh // hpg] * l_t).astype(BF16))
        xdts.append((xs_h * rows_of(dt_t, hh)).astype(BF16))
        xws.append((xs_h * rows_of(w_t, hh)).astype(BF16))
    yield

    ys = [jnp.dot(ps[h], v_ref[:, sl], preferred_element_type=F32) + ycross[h]
          for h, sl in enumerate(heads)]
    yds = [jnp.dot(xdts[hh], m_ts[hh], preferred_element_type=F32)
           for hh in range(SSD_HEADS)]
    for g in range(SSD_GROUPS):
        xw = jnp.concatenate(xws[g * hpg:(g + 1) * hpg], axis=0)
        cd = jnp.concatenate([rows_of(cdec, hh) for hh in range(g * hpg, (g + 1) * hpg)], axis=0)
        sstate[g * gw:(g + 1) * gw, :] = (
            cd * s_prevs[g] + jnp.dot(xw, bgs[g], preferred_element_type=F32))
    yield

    for h, sl in enumerate(heads):
        yn = ys[h] * _rms_scale(ys[h]) * rnw_ref[:, sl]
        out_ref[:, sl] = (yn * g_ref[:, sl].astype(F32)).astype(BF16)
    y_t_parts = []
    for hh in range(SSD_HEADS):
        e = hh % hpg
        xs_h = xs_t[hh * SSD_HEAD_DIM:(hh + 1) * SSD_HEAD_DIM, :]
        yo = yo_ts[hh // hpg][e * SSD_HEAD_DIM:(e + 1) * SSD_HEAD_DIM, :] * rows_of(ea_t, hh)
        y_t_parts.append(yds[hh] + yo + rows_of(dskip, hh) * xs_h)
    y = jnp.concatenate(y_t_parts, axis=0).T
    y = y * z_ref[...].astype(F32)
    for g in range(SSD_GROUPS):
        sl = slice(g * gw, (g + 1) * gw)
        yg = y[:, sl]
        out_ref[:, RET_WIDTH + g * gw:RET_WIDTH + (g + 1) * gw] = (
            yg * _rms_scale(yg) * snw_ref[:, sl]).astype(BF16)
    yield


def _mixer_chunk(q_ref, k_ref, v_ref, g_ref, z_ref, xs_ref, bc_ref, qd_ref, kd_ref, dt_ref,
                 dintra_ref, rnw_ref, dtb_ref, alog_ref,
                 dskip_ref, snw_ref, out_ref, rstate, sstate):
    chunk_decay = _ret_gammas() ** CHUNK
    for h in range(RET_HEADS):
        sl = slice(h * RET_DIM, (h + 1) * RET_DIM)
        vh = v_ref[:, sl]
        scores = lax.dot_general(q_ref[:, sl], k_ref[:, sl], _NT, preferred_element_type=F32)
        p = (scores * dintra_ref[h]).astype(BF16)
        y = jnp.dot(p, vh, preferred_element_type=F32)
        y = y + jnp.dot(qd_ref[:, sl], rstate[h].astype(BF16), preferred_element_type=F32)
        kv = lax.dot_general(kd_ref[:, sl], vh, _TN, preferred_element_type=F32)
        rstate[h] = float(chunk_decay[h]) * rstate[h] + kv
        yn = y * _rms_scale(y) * rnw_ref[:, sl]
        out_ref[:, sl] = (yn * g_ref[:, sl].astype(F32)).astype(BF16)

    xs_t = xs_ref[...].astype(F32).T

    dt_t = jax.nn.softplus(dt_ref[...].T[0:SSD_HEADS, :] + dtb_ref[...])
    acs_t = _cumsum_lanes(dt_t * (-jnp.exp(alog_ref[...])))
    a_last = acs_t[:, CHUNK - 1:CHUNK]
    w_t = jnp.exp(a_last - acs_t) * dt_t
    ea_t = jnp.exp(acs_t)
    cdec = jnp.broadcast_to(jnp.exp(a_last), (SSD_HEADS, CHUNK))
    acs_pad = jnp.concatenate([acs_t, jnp.zeros((CHUNK - SSD_HEADS, CHUNK), F32)], axis=0)
    acs_col = acs_pad.T
    dskip = dskip_ref[...]

    row = lax.broadcasted_iota(jnp.int32, (CHUNK, CHUNK), 0)
    col = lax.broadcasted_iota(jnp.int32, (CHUNK, CHUNK), 1)
    causal_t = col >= row

    hpg = SSD_HEADS // SSD_GROUPS
    gw = hpg * SSD_HEAD_DIM
    y_t_parts = []
    for g in range(SSD_GROUPS):
        bg = bc_ref[:, g * SSD_STATE:(g + 1) * SSD_STATE]
        cbase = SSD_GROUPS * SSD_STATE
        cg = bc_ref[:, cbase + g * SSD_STATE:cbase + (g + 1) * SSD_STATE]
        cb_t = lax.dot_general(bg, cg, _NT, preferred_element_type=F32)
        s_prev = sstate[g * gw:(g + 1) * gw, :]
        yo_t = lax.dot_general(s_prev.astype(BF16), cg, _NT, preferred_element_type=F32)

        def rows_of(t, hh):
            return jnp.broadcast_to(t[hh:hh + 1, :], (SSD_HEAD_DIM, CHUNK))

        xw_parts, cd_parts = [], []
        for e in range(hpg):
            hh = g * hpg + e
            xs_h = xs_t[hh * SSD_HEAD_DIM:(hh + 1) * SSD_HEAD_DIM, :]
            seg = (jnp.broadcast_to(acs_t[hh:hh + 1, :], (CHUNK, CHUNK))
                   - jnp.broadcast_to(acs_col[:, hh:hh + 1], (CHUNK, CHUNK)))
            l_t = jnp.exp(jnp.where(causal_t, seg, -jnp.inf))
            m_t = (cb_t * l_t).astype(BF16)
            xdt = (xs_h * rows_of(dt_t, hh)).astype(BF16)
            yd = jnp.dot(xdt, m_t, preferred_element_type=F32)
            yo = yo_t[e * SSD_HEAD_DIM:(e + 1) * SSD_HEAD_DIM, :] * rows_of(ea_t, hh)
            y_t_parts.append(yd + yo + rows_of(dskip, hh) * xs_h)
            xw_parts.append((xs_h * rows_of(w_t, hh)).astype(BF16))
            cd_parts.append(rows_of(cdec, hh))
        xw = jnp.concatenate(xw_parts, axis=0)
        cd = jnp.concatenate(cd_parts, axis=0)
        sstate[g * gw:(g + 1) * gw, :] = cd * s_prev + jnp.dot(xw, bg, preferred_element_type=F32)

    y = jnp.concatenate(y_t_parts, axis=0).T
    y = y * z_ref[...].astype(F32)
    for g in range(SSD_GROUPS):
        sl = slice(g * gw, (g + 1) * gw)
        yg = y[:, sl]
        out_ref[:, RET_WIDTH + g * gw:RET_WIDTH + (g + 1) * gw] = (
            yg * _rms_scale(yg) * snw_ref[:, sl]).astype(BF16)


def _mixer(proj, qkd, dt, params):
    (ret_norm_w, dt_bias, a_log, d_skip, ssd_norm_w) = params
    dintra, _ = _ret_tables()
    proj = proj.reshape(BATCH, SEQ, PROJ_WIDTH)
    qkd = qkd.reshape(BATCH, SEQ, 2 * RET_WIDTH)
    dt = dt.reshape(BATCH, SEQ, DT_PAD)

    def col_block(j, width):
        return pl.BlockSpec((MIX_NB, CHUNK, width), lambda b, c: (b, c, j))

    def full(shape):
        return pl.BlockSpec(shape, lambda b, c: (0,) * len(shape))

    bcast = lambda v: jnp.broadcast_to(v.astype(F32)[:, None], (SSD_HEADS, CHUNK))
    in_specs = [
        col_block(0, RET_WIDTH), col_block(1, RET_WIDTH), col_block(2, RET_WIDTH),
        col_block(3, RET_WIDTH), col_block(4, SSD_INNER), col_block(5, SSD_INNER),
        col_block(12, 2 * SSD_GROUPS * SSD_STATE),
        col_block(0, RET_WIDTH), col_block(1, RET_WIDTH),
        col_block(0, DT_PAD),
        full((RET_HEADS, CHUNK, CHUNK)),
        full((1, RET_WIDTH)),
        full((SSD_HEADS, CHUNK)), full((SSD_HEADS, CHUNK)), full((SSD_HEADS, CHUNK)),
        full((1, SSD_INNER)),
    ]
    assert len(in_specs) - 6 == MIX_N_BATCHED
    mix = pl.pallas_call(
        _mixer_kernel,
        grid=(BATCH // MIX_NB, NUM_CHUNKS),
        in_specs=in_specs,
        out_specs=pl.BlockSpec((MIX_NB, CHUNK, D_MODEL), lambda b, c: (b, c, 0)),
        out_shape=jax.ShapeDtypeStruct((BATCH, SEQ, D_MODEL), BF16),
        scratch_shapes=[
            pltpu.VMEM((MIX_NB, RET_HEADS, RET_DIM, RET_DIM), F32),
            pltpu.VMEM((MIX_NB, SSD_INNER, SSD_STATE), F32),
        ],
        compiler_params=pltpu.CompilerParams(
            dimension_semantics=("arbitrary", "arbitrary"),
            vmem_limit_bytes=VMEM_LIMIT),
        name="mixer",
    )(proj, proj, proj, proj, proj, proj, proj, qkd, qkd, dt, dintra,
      ret_norm_w.astype(F32)[None, :],
      bcast(dt_bias), bcast(a_log), bcast(d_skip), ssd_norm_w.astype(F32)[None, :])
    return mix.reshape(TOKENS, D_MODEL)


OUT_TM = 512


def _outproj_kernel(mix_ref, w_ref, x_ref, nw_ref, h_ref, hn_ref):
    h = x_ref[...] + jnp.dot(mix_ref[...], w_ref[...].astype(BF16),
                             preferred_element_type=F32)
    h_ref[...] = h
    hn_ref[...] = (h * _rms_scale(h) * nw_ref[...]).astype(BF16)


def _outproj(mix, w_out, x2d, norm_w):
    return pl.pallas_call(
        _outproj_kernel,
        grid=(TOKENS // OUT_TM,),
        in_specs=[
            pl.BlockSpec((OUT_TM, D_MODEL), lambda m: (m, 0)),
            pl.BlockSpec((D_MODEL, D_MODEL), lambda m: (0, 0), pipeline_mode=pl.Buffered(1)),
            pl.BlockSpec((OUT_TM, D_MODEL), lambda m: (m, 0)),
            pl.BlockSpec((1, D_MODEL), lambda m: (0, 0)),
        ],
        out_specs=[
            pl.BlockSpec((OUT_TM, D_MODEL), lambda m: (m, 0)),
            pl.BlockSpec((OUT_TM, D_MODEL), lambda m: (m, 0)),
        ],
        out_shape=[
            jax.ShapeDtypeStruct((TOKENS, D_MODEL), F32),
            jax.ShapeDtypeStruct((TOKENS, D_MODEL), BF16),
        ],
        compiler_params=pltpu.CompilerParams(
            dimension_semantics=("arbitrary",),
            vmem_limit_bytes=VMEM_LIMIT),
        name="outproj",
    )(mix, w_out, x2d, norm_w)


MLP_TM = 1024
MLP_TF = 512
MLP_NF = D_FF // MLP_TF
MLP_HROWS = MLP_TM // MLP_NF
MLP_ROWS = 128


def _mlp_kernel(hn_ref, wup_ref, wdn_ref, h_ref, nw_ref, out_ref):
    f = pl.program_id(1)

    @pl.when(f == 0)
    def _():
        out_ref[...] = jnp.zeros_like(out_ref)

    u = jnp.maximum(jnp.dot(hn_ref[...], wup_ref[...].astype(BF16),
                            preferred_element_type=F32), 0.0)
    out_ref[...] += jnp.dot((u * u).astype(BF16), wdn_ref[...].astype(BF16),
                            preferred_element_type=F32)
    r = pl.multiple_of(f * MLP_HROWS, MLP_HROWS)
    out_ref[pl.ds(r, MLP_HROWS), :] += h_ref[...]

    @pl.when(f == MLP_NF - 1)
    def _():
        def body(i, carry):
            r = pl.multiple_of(i * MLP_ROWS, MLP_ROWS)
            h = out_ref[pl.ds(r, MLP_ROWS), :]
            out_ref[pl.ds(r, MLP_ROWS), :] = h * _rms_scale(h) * nw_ref[...]
            return carry
        lax.fori_loop(0, MLP_TM // MLP_ROWS, body, 0)


def _mlp(hn, w_up, w_down, h, norm_w):
    return pl.pallas_call(
        _mlp_kernel,
        grid=(TOKENS // MLP_TM, MLP_NF),
        in_specs=[
            pl.BlockSpec((MLP_TM, D_MODEL), lambda m, f: (m, 0)),
            pl.BlockSpec((D_MODEL, MLP_TF), lambda m, f: (0, f)),
            pl.BlockSpec((MLP_TF, D_MODEL), lambda m, f: (f, 0)),
            pl.BlockSpec((MLP_HROWS, D_MODEL), lambda m, f: (m * MLP_NF + f, 0)),
            pl.BlockSpec((1, D_MODEL), lambda m, f: (0, 0)),
        ],
        out_specs=pl.BlockSpec((MLP_TM, D_MODEL), lambda m, f: (m, 0)),
        out_shape=jax.ShapeDtypeStruct((TOKENS, D_MODEL), F32),
        compiler_params=pltpu.CompilerParams(
            dimension_semantics=("arbitrary", "arbitrary"),
            vmem_limit_bytes=VMEM_LIMIT),
        name="mlp",
    )(hn, w_up, w_down, h, norm_w)


def kernel(x, norm_mix_w, w_in, ret_norm_w, conv_w, conv_b, dt_bias, a_log, d_skip, ssd_norm_w,
           w_out, norm_mlp_w, w_up, w_down, norm_final_w):
    x2d = x.reshape(TOKENS, D_MODEL)
    w_in_t = w_in.T
    w_dt_t = jnp.pad(w_in_t[PROJ_WIDTH:, :], ((0, DT_PAD - SSD_HEADS), (0, 0)))
    proj, qkd, dt = _inproj(x2d, norm_mix_w.astype(F32)[None, :], w_in_t, w_dt_t,
                            conv_w.astype(F32), conv_b.astype(F32)[None, :])
    mix = _mixer(proj, qkd, dt, (ret_norm_w, dt_bias, a_log, d_skip, ssd_norm_w))
    h, hn = _outproj(mix, w_out, x2d, norm_mlp_w.astype(F32)[None, :])
    out = _mlp(hn, w_up, w_down, h, norm_final_w.astype(F32)[None, :])
    return out.reshape(BATCH, SEQ, D_MODEL)
```

```python
import functools

import numpy as np
import jax
import jax.numpy as jnp
from jax import lax
from jax.experimental import pallas as pl
from jax.experimental.pallas import tpu as pltpu

F32 = jnp.float32
BF16 = jnp.bfloat16

D_MODEL = 2048
BATCH = 4
SEQ = 2048
TOKENS = BATCH * SEQ
RET_HEADS = 4
RET_DIM = 256
RET_WIDTH = RET_HEADS * RET_DIM
ROPE_BASE = 10000.0
SSD_INNER = 1024
SSD_HEAD_DIM = 64
SSD_HEADS = 16
SSD_GROUPS = 2
SSD_STATE = 128
SSD_CONV = 4
SSD_CONV_DIM = SSD_INNER + 2 * SSD_GROUPS * SSD_STATE
CHUNK = 128
NUM_CHUNKS = SEQ // CHUNK
PROJ_WIDTH = 4 * RET_WIDTH + SSD_INNER + SSD_CONV_DIM
DT_PAD = 128
D_FF = 4 * D_MODEL
EPS = 1e-6

VMEM_LIMIT = 56 * 1024 * 1024

_NT = (((1,), (1,)), ((), ()))
_TN = (((0,), (0,)), ((), ()))


def _rms_scale(x):
    return lax.rsqrt(jnp.mean(x * x, axis=-1, keepdims=True) + EPS)


def _ret_gammas():
    return 1.0 - 2.0 ** (-5.0 - np.arange(RET_HEADS, dtype=np.float64))


def _ret_tables():
    lg = np.log(_ret_gammas())
    idx = np.arange(CHUNK, dtype=np.float64)
    rel = idx[:, None] - idx[None, :]
    causal = rel >= 0
    dintra = np.where(causal[None], np.exp(np.where(causal, rel, 0.0)[None] * lg[:, None, None]), 0.0)
    qdec = np.exp((idx + 1.0)[:, None] * lg[None, :])
    kdec = np.exp((CHUNK - 1.0 - idx)[:, None] * lg[None, :])
    qkdec = np.concatenate([np.repeat(qdec, RET_DIM, axis=1), np.repeat(kdec, RET_DIM, axis=1)], axis=1)
    return jnp.asarray(dintra, F32), jnp.asarray(qkdec, F32)


def _rope_tables():
    half = RET_DIM // 2
    inv_freq = ROPE_BASE ** (-jnp.arange(half, dtype=F32) / half)
    ang = jnp.arange(SEQ, dtype=F32)[:, None] * inv_freq[None, :]
    return jnp.cos(ang), jnp.sin(ang)


IN_TM = 1024
IN_TN = 512
IN_NT = PROJ_WIDTH // IN_TN
IN_MT = TOKENS // IN_TM
IN_TILES = IN_MT * IN_NT
IN_STEPS = IN_NT + IN_TILES + 1
IN_RB = 256
IN_NCH = 8
IN_CH = IN_TM // IN_NCH
IN_SEQ_TILES = SEQ // IN_TM
IN_N_ROPE = 2 * RET_WIDTH // IN_TN
IN_N_K = RET_WIDTH // IN_TN
IN_N_GATE = (3 * RET_WIDTH // IN_TN, (4 * RET_WIDTH + SSD_INNER) // IN_TN)


def _in_tile(s, lag):
    t = jnp.clip(s - IN_NT - lag, 0, IN_TILES - 1)
    return t // IN_NT, t % IN_NT


def _inproj_kernel(x_ref, nw_ref, w_ref, wdt_ref, cos_ref, sin_ref, dec_ref, cw_ref, cb_ref,
                   proj_ref, qkd_ref, dt_ref, hn_ref, acc_ref, raw_ref, carry_ref):
    s = pl.program_id(0)
    t = s - IN_NT
    m, n = _in_tile(s, 0)
    pm, pn = _in_tile(s, 1)

    def normalise():
        slot = (s // IN_NT) % 2
        r = pl.multiple_of(jnp.minimum(s % IN_NT, IN_NCH - 1) * IN_CH, IN_CH)
        x = x_ref[...]
        hn_ref[slot, pl.ds(r, IN_CH), :] = (x * _rms_scale(x) * nw_ref[...]).astype(BF16)

    def matmul_rows(rb, wbf):
        rows = pl.ds(rb * IN_RB, IN_RB)
        acc_ref[rows, :] = lax.dot_general(
            hn_ref[m % 2, rows, :], wbf, _NT, preferred_element_type=F32)

    def epilogue_rows(kind, rb):
        rows = pl.ds(rb * IN_RB, IN_RB)
        a = raw_ref[pl.ds(8 + rb * IN_RB, IN_RB), :]
        if kind == "rope":
            a = a * jnp.where(pn >= IN_N_K, RET_DIM ** -0.5, 1.0)
            cos, sin = cos_ref[rows, :], sin_ref[rows, :]
            half = RET_DIM // 2
            parts = []
            for hh in range(IN_TN // RET_DIM):
                x1 = a[:, hh * RET_DIM:hh * RET_DIM + half]
                x2 = a[:, hh * RET_DIM + half:(hh + 1) * RET_DIM]
                parts += [x1 * cos - x2 * sin, x1 * sin + x2 * cos]
            r = jnp.concatenate(parts, axis=-1)
            proj_ref[rows, :] = r.astype(BF16)
            rd = r.reshape(IN_RB // CHUNK, CHUNK, IN_TN) * dec_ref[...][None]
            qkd_ref[rows, :] = rd.reshape(IN_RB, IN_TN).astype(BF16)
        elif kind == "plain":
            proj_ref[rows, :] = a.astype(BF16)
        elif kind == "silu":
            proj_ref[rows, :] = jax.nn.silu(a).astype(BF16)
        else:
            conv = cb_ref[...] + cw_ref[SSD_CONV - 1:SSD_CONV, :] * a
            for tap in range(SSD_CONV - 1):
                off = 8 - (SSD_CONV - 1) + tap + rb * IN_RB
                conv = conv + cw_ref[tap:tap + 1, :] * raw_ref[pl.ds(off, IN_RB), :]
            proj_ref[rows, :] = jax.nn.silu(conv).astype(BF16)

    def step(do_matmul, kind):
        if kind is not None:
            raw_ref[8:8 + IN_TM, :] = acc_ref[...]
        if kind == "conv":
            j = pn - IN_N_GATE[1]
            raw_ref[0:8, :] = jnp.where(pm % IN_SEQ_TILES == 0, 0.0, carry_ref[j])
        if do_matmul:
            wbf = w_ref[...].astype(BF16)
        for rb in range(IN_TM // IN_RB):
            if do_matmul:
                matmul_rows(rb, wbf)
            if kind is not None:
                epilogue_rows(kind, rb)
            if do_matmul and rb == 0:
                normalise()
        if kind == "conv":
            carry_ref[j] = raw_ref[IN_TM:IN_TM + 8, :]

    @pl.when((t >= 0) & (t < IN_TILES) & (n == 0))
    def _():
        dt_ref[...] = lax.dot_general(hn_ref[m % 2], wdt_ref[...].astype(BF16), _NT,
                                      preferred_element_type=F32)

    @pl.when(s < IN_NT)
    def _():
        normalise()

    @pl.when(t == 0)
    def _():
        step(True, None)

    live = (t >= 1) & (t < IN_TILES)

    @pl.when(live & (pn < IN_N_ROPE))
    def _():
        step(True, "rope")

    @pl.when(live & (pn >= IN_N_ROPE) & (pn < IN_N_GATE[0]))
    def _():
        step(True, "plain")

    @pl.when(live & (pn >= IN_N_GATE[0]) & (pn < IN_N_GATE[1]))
    def _():
        step(True, "silu")

    @pl.when(live & (pn >= IN_N_GATE[1]))
    def _():
        step(True, "conv")

    @pl.when(t == IN_TILES)
    def _():
        step(False, "conv")


def _inproj(x2d, norm_w, w_main, w_dt, conv_w, conv_b):
    cos, sin = _rope_tables()
    _, qkdec = _ret_tables()
    n_conv = SSD_CONV_DIM // IN_TN
    rope_rows = lambda s: (_in_tile(s, 1)[0] % IN_SEQ_TILES, 0)
    conv_tile = lambda s: (0, jnp.maximum(_in_tile(s, 1)[1] - IN_N_GATE[1], 0))
    return pl.pallas_call(
        _inproj_kernel,
        grid=(IN_STEPS,),
        in_specs=[
            pl.BlockSpec((IN_CH, D_MODEL),
                         lambda s: (jnp.minimum(s // IN_NT, IN_MT - 1) * IN_NCH
                                    + jnp.minimum(s % IN_NT, IN_NCH - 1), 0)),
            pl.BlockSpec((1, D_MODEL), lambda s: (0, 0)),
            pl.BlockSpec((IN_TN, D_MODEL), lambda s: (_in_tile(s, 0)[1], 0)),
            pl.BlockSpec((DT_PAD, D_MODEL), lambda s: (0, 0)),
            pl.BlockSpec((IN_TM, RET_DIM // 2), rope_rows),
            pl.BlockSpec((IN_TM, RET_DIM // 2), rope_rows),
            pl.BlockSpec((CHUNK, IN_TN), lambda s: (0, jnp.minimum(_in_tile(s, 1)[1], IN_N_ROPE - 1))),
            pl.BlockSpec((SSD_CONV, IN_TN), conv_tile),
            pl.BlockSpec((1, IN_TN), conv_tile),
        ],
        out_specs=[
            pl.BlockSpec((IN_TM, IN_TN), lambda s: _in_tile(s, 1)),
            pl.BlockSpec((IN_TM, IN_TN),
                         lambda s: (_in_tile(s, 1)[0], jnp.minimum(_in_tile(s, 1)[1], IN_N_ROPE - 1))),
            pl.BlockSpec((IN_TM, DT_PAD), lambda s: (_in_tile(s, 0)[0], 0)),
        ],
        out_shape=[
            jax.ShapeDtypeStruct((TOKENS, PROJ_WIDTH), BF16),
            jax.ShapeDtypeStruct((TOKENS, 2 * RET_WIDTH), BF16),
            jax.ShapeDtypeStruct((TOKENS, DT_PAD), F32),
        ],
        scratch_shapes=[
            pltpu.VMEM((2, IN_TM, D_MODEL), BF16),
            pltpu.VMEM((IN_TM, IN_TN), F32),
            pltpu.VMEM((IN_TM + 8, IN_TN), F32),
            pltpu.VMEM((n_conv, 8, IN_TN), F32),
        ],
        compiler_params=pltpu.CompilerParams(
            dimension_semantics=("arbitrary",),
            vmem_limit_bytes=VMEM_LIMIT),
        name="inproj",
    )(x2d, norm_w, w_main, w_dt, cos, sin, qkdec, conv_w, conv_b)


def _cumsum_lanes(x):
    lane = lax.broadcasted_iota(jnp.int32, x.shape, 1)
    k = 1
    while k < x.shape[1]:
        x = x + jnp.where(lane >= k, pltpu.roll(x, k, axis=1), 0.0)
        k *= 2
    return x


MIX_NB = 2
MIX_N_BATCHED = 10
MIX_STAGES = 5


def _mixer_kernel(*refs):
    ins, consts = refs[:MIX_N_BATCHED], refs[MIX_N_BATCHED:-3]
    out_ref, rstate, sstate = refs[-3:]

    @pl.when(pl.program_id(1) == 0)
    def _():
        rstate[...] = jnp.zeros_like(rstate)
        sstate[...] = jnp.zeros_like(sstate)

    chains = [_mixer_stages(*[r.at[bi] for r in ins], *consts,
                            out_ref.at[bi], rstate.at[bi], sstate.at[bi])
              for bi in range(MIX_NB)]
    for _ in range(MIX_STAGES):
        for chain in chains:
            next(chain)


def _mixer_stages(q_ref, k_ref, v_ref, g_ref, z_ref, xs_ref, bc_ref, qd_ref, kd_ref, dt_ref,
                  dintra_ref, rnw_ref, dtb_ref, alog_ref,
                  dskip_ref, snw_ref, out_ref, rstate, sstate):
    hpg = SSD_HEADS // SSD_GROUPS
    gw = hpg * SSD_HEAD_DIM
    cbase = SSD_GROUPS * SSD_STATE
    heads = [slice(h * RET_DIM, (h + 1) * RET_DIM) for h in range(RET_HEADS)]
    chunk_decay = _ret_gammas() ** CHUNK

    def rows_of(t, hh):
        return jnp.broadcast_to(t[hh:hh + 1, :], (SSD_HEAD_DIM, CHUNK))

    dt_t = jax.nn.softplus(dt_ref[...].T[0:SSD_HEADS, :] + dtb_ref[...])
    acs_t = _cumsum_lanes(dt_t * (-jnp.exp(alog_ref[...])))
    a_last = acs_t[:, CHUNK - 1:CHUNK]
    w_t = jnp.exp(a_last - acs_t) * dt_t
    ea_t = jnp.exp(acs_t)
    cdec = jnp.broadcast_to(jnp.exp(a_last), (SSD_HEADS, CHUNK))
    acs_pad = jnp.concatenate([acs_t, jnp.zeros((CHUNK - SSD_HEADS, CHUNK), F32)], axis=0)
    acs_col = acs_pad.T
    dskip = dskip_ref[...]
    yield

    scores, ycross, kv = [], [], []
    for h, sl in enumerate(heads):
        scores.append(lax.dot_general(q_ref[:, sl], k_ref[:, sl], _NT, preferred_element_type=F32))
        ycross.append(jnp.dot(qd_ref[:, sl], rstate[h].astype(BF16), preferred_element_type=F32))
        kv.append(lax.dot_general(kd_ref[:, sl], v_ref[:, sl], _TN, preferred_element_type=F32))
    xs_t = xs_ref[...].astype(F32).T
    bgs, cgs, cb_ts, yo_ts, s_prevs = [], [], [], [], []
    for g in range(SSD_GROUPS):
        bg = bc_ref[:, g * SSD_STATE:(g + 1) * SSD_STATE]
        cg = bc_ref[:, cbase + g * SSD_STATE:cbase + (g + 1) * SSD_STATE]
        s_prev = sstate[g * gw:(g + 1) * gw, :]
        bgs.append(bg)
        cgs.append(cg)
        s_prevs.append(s_prev)
        cb_ts.append(lax.dot_general(bg, cg, _NT, preferred_element_type=F32))
        yo_ts.append(lax.dot_general(s_prev.astype(BF16), cg, _NT,
                                     preferred_element_type=F32))
    yield

    ps = [(scores[h] * dintra_ref[h]).astype(BF16) for h in range(RET_HEADS)]
    for h in range(RET_HEADS):
        rstate[h] = float(chunk_decay[h]) * rstate[h] + kv[h]
    row = lax.broadcasted_iota(jnp.int32, (CHUNK, CHUNK), 0)
    col = lax.broadcasted_iota(jnp.int32, (CHUNK, CHUNK), 1)
    causal_t = col >= row
    m_ts, xdts, xws = [], [], []
    for hh in range(SSD_HEADS):
        xs_h = xs_t[hh * SSD_HEAD_DIM:(hh + 1) * SSD_HEAD_DIM, :]
        seg = (jnp.broadcast_to(acs_t[hh:hh + 1, :], (CHUNK, CHUNK))
               - jnp.broadcast_to(acs_col[:, hh:hh + 1], (CHUNK, CHUNK)))
        l_t = jnp.exp(jnp.where(causal_t, seg, -jnp.inf))
        m_ts.append((cb_ts[hh // hpg] * l_t).astype(BF16))
        xdts.append((xs_h * rows_of(dt_t, hh)).astype(BF16))
        xws.append((xs_h * rows_of(w_t, hh)).astype(BF16))
    yield

    ys = [jnp.dot(ps[h], v_ref[:, sl], preferred_element_type=F32) + ycross[h]
          for h, sl in enumerate(heads)]
    yds = [jnp.dot(xdts[hh], m_ts[hh], preferred_element_type=F32)
           for hh in range(SSD_HEADS)]
    for g in range(SSD_GROUPS):
        xw = jnp.concatenate(xws[g * hpg:(g + 1) * hpg], axis=0)
        cd = jnp.concatenate([rows_of(cdec, hh) for hh in range(g * hpg, (g + 1) * hpg)], axis=0)
        sstate[g * gw:(g + 1) * gw, :] = (
            cd * s_prevs[g] + jnp.dot(xw, bgs[g], preferred_element_type=F32))
    yield

    for h, sl in enumerate(heads):
        yn = ys[h] * _rms_scale(ys[h]) * rnw_ref[:, sl]
        out_ref[:, sl] = (yn * g_ref[:, sl].astype(F32)).astype(BF16)
    y_t_parts = []
    for hh in range(SSD_HEADS):
        e = hh % hpg
        xs_h = xs_t[hh * SSD_HEAD_DIM:(hh + 1) * SSD_HEAD_DIM, :]
        yo = yo_ts[hh // hpg][e * SSD_HEAD_DIM:(e + 1) * SSD_HEAD_DIM, :] * rows_of(ea_t, hh)
        y_t_parts.append(yds[hh] + yo + rows_of(dskip, hh) * xs_h)
    y = jnp.concatenate(y_t_parts, axis=0).T
    y = y * z_ref[...].astype(F32)
    for g in range(SSD_GROUPS):
        sl = slice(g * gw, (g + 1) * gw)
        yg = y[:, sl]
        out_ref[:, RET_WIDTH + g * gw:RET_WIDTH + (g + 1) * gw] = (
            yg * _rms_scale(yg) * snw_ref[:, sl]).astype(BF16)
    yield


def _mixer(proj, qkd, dt, params):
    (ret_norm_w, dt_bias, a_log, d_skip, ssd_norm_w) = params
    dintra, _ = _ret_tables()
    proj = proj.reshape(BATCH, SEQ, PROJ_WIDTH)
    qkd = qkd.reshape(BATCH, SEQ, 2 * RET_WIDTH)
    dt = dt.reshape(BATCH, SEQ, DT_PAD)

    def col_block(j, width):
        return pl.BlockSpec((MIX_NB, CHUNK, width), lambda b, c: (b, c, j))

    def full(shape):
        return pl.BlockSpec(shape, lambda b, c: (0,) * len(shape))

    bcast = lambda v: jnp.broadcast_to(v.astype(F32)[:, None], (SSD_HEADS, CHUNK))
    in_specs = [
        col_block(0, RET_WIDTH), col_block(1, RET_WIDTH), col_block(2, RET_WIDTH),
        col_block(3, RET_WIDTH), col_block(4, SSD_INNER), col_block(5, SSD_INNER),
        col_block(12, 2 * SSD_GROUPS * SSD_STATE),
        col_block(0, RET_WIDTH), col_block(1, RET_WIDTH),
        col_block(0, DT_PAD),
        full((RET_HEADS, CHUNK, CHUNK)),
        full((1, RET_WIDTH)),
        full((SSD_HEADS, CHUNK)), full((SSD_HEADS, CHUNK)), full((SSD_HEADS, CHUNK)),
        full((1, SSD_INNER)),
    ]
    assert len(in_specs) - 6 == MIX_N_BATCHED
    mix = pl.pallas_call(
        _mixer_kernel,
        grid=(BATCH // MIX_NB, NUM_CHUNKS),
        in_specs=in_specs,
        out_specs=pl.BlockSpec((MIX_NB, CHUNK, D_MODEL), lambda b, c: (b, c, 0)),
        out_shape=jax.ShapeDtypeStruct((BATCH, SEQ, D_MODEL), BF16),
        scratch_shapes=[
            pltpu.VMEM((MIX_NB, RET_HEADS, RET_DIM, RET_DIM), F32),
            pltpu.VMEM((MIX_NB, SSD_INNER, SSD_STATE), F32),
        ],
        compiler_params=pltpu.CompilerParams(
            dimension_semantics=("arbitrary", "arbitrary"),
            vmem_limit_bytes=VMEM_LIMIT),
        name="mixer",
    )(proj, proj, proj, proj, proj, proj, proj, qkd, qkd, dt, dintra,
      ret_norm_w.astype(F32)[None, :],
      bcast(dt_bias), bcast(a_log), bcast(d_skip), ssd_norm_w.astype(F32)[None, :])
    return mix.reshape(TOKENS, D_MODEL)


OUT_TM = 512


def _outproj_kernel(mix_ref, w_ref, x_ref, nw_ref, h_ref, hn_ref):
    h = x_ref[...] + jnp.dot(mix_ref[...], w_ref[...].astype(BF16),
                             preferred_element_type=F32)
    h_ref[...] = h
    hn_ref[...] = (h * _rms_scale(h) * nw_ref[...]).astype(BF16)


def _outproj(mix, w_out, x2d, norm_w):
    return pl.pallas_call(
        _outproj_kernel,
        grid=(TOKENS // OUT_TM,),
        in_specs=[
            pl.BlockSpec((OUT_TM, D_MODEL), lambda m: (m, 0)),
            pl.BlockSpec((D_MODEL, D_MODEL), lambda m: (0, 0), pipeline_mode=pl.Buffered(1)),
            pl.BlockSpec((OUT_TM, D_MODEL), lambda m: (m, 0)),
            pl.BlockSpec((1, D_MODEL), lambda m: (0, 0)),
        ],
        out_specs=[
            pl.BlockSpec((OUT_TM, D_MODEL), lambda m: (m, 0)),
            pl.BlockSpec((OUT_TM, D_MODEL), lambda m: (m, 0)),
        ],
        out_shape=[
            jax.ShapeDtypeStruct((TOKENS, D_MODEL), F32),
            jax.ShapeDtypeStruct((TOKENS, D_MODEL), BF16),
        ],
        compiler_params=pltpu.CompilerParams(
            dimension_semantics=("arbitrary",),
            vmem_limit_bytes=VMEM_LIMIT),
        name="outproj",
    )(mix, w_out, x2d, norm_w)


MLP_TM = 1024
MLP_TF = 512
MLP_NF = D_FF // MLP_TF
MLP_HROWS = MLP_TM // MLP_NF
MLP_ROWS = 128


def _mlp_kernel(hn_ref, wup_ref, wdn_ref, h_ref, nw_ref, out_ref):
    f = pl.program_id(1)

    @pl.when(f == 0)
    def _():
        out_ref[...] = jnp.zeros_like(out_ref)

    u = jnp.maximum(jnp.dot(hn_ref[...], wup_ref[...].astype(BF16),
                            preferred_element_type=F32), 0.0)
    out_ref[...] += jnp.dot((u * u).astype(BF16), wdn_ref[...].astype(BF16),
                            preferred_element_type=F32)
    r = pl.multiple_of(f * MLP_HROWS, MLP_HROWS)
    out_ref[pl.ds(r, MLP_HROWS), :] += h_ref[...]

    @pl.when(f == MLP_NF - 1)
    def _():
        def body(i, carry):
            r = pl.multiple_of(i * MLP_ROWS, MLP_ROWS)
            h = out_ref[pl.ds(r, MLP_ROWS), :]
            out_ref[pl.ds(r, MLP_ROWS), :] = h * _rms_scale(h) * nw_ref[...]
            return carry
        lax.fori_loop(0, MLP_TM // MLP_ROWS, body, 0)


def _mlp(hn, w_up, w_down, h, norm_w):
    return pl.pallas_call(
        _mlp_kernel,
        grid=(TOKENS // MLP_TM, MLP_NF),
        in_specs=[
            pl.BlockSpec((MLP_TM, D_MODEL), lambda m, f: (m, 0)),
            pl.BlockSpec((D_MODEL, MLP_TF), lambda m, f: (0, f)),
            pl.BlockSpec((MLP_TF, D_MODEL), lambda m, f: (f, 0)),
            pl.BlockSpec((MLP_HROWS, D_MODEL), lambda m, f: (m * MLP_NF + f, 0)),
            pl.BlockSpec((1, D_MODEL), lambda m, f: (0, 0)),
        ],
        out_specs=pl.BlockSpec((MLP_TM, D_MODEL), lambda m, f: (m, 0)),
        out_shape=jax.ShapeDtypeStruct((TOKENS, D_MODEL), F32),
        compiler_params=pltpu.CompilerParams(
            dimension_semantics=("arbitrary", "arbitrary"),
            vmem_limit_bytes=VMEM_LIMIT),
        name="mlp",
    )(hn, w_up, w_down, h, norm_w)


def kernel(x, norm_mix_w, w_in, ret_norm_w, conv_w, conv_b, dt_bias, a_log, d_skip, ssd_norm_w,
           w_out, norm_mlp_w, w_up, w_down, norm_final_w):
    x2d = x.reshape(TOKENS, D_MODEL)
    w_in_t = w_in.T
    w_dt_t = jnp.pad(w_in_t[PROJ_WIDTH:, :], ((0, DT_PAD - SSD_HEADS), (0, 0)))
    proj, qkd, dt = _inproj(x2d, norm_mix_w.astype(F32)[None, :], w_in_t, w_dt_t,
                            conv_w.astype(F32), conv_b.astype(F32)[None, :])
    mix = _mixer(proj, qkd, dt, (ret_norm_w, dt_bias, a_log, d_skip, ssd_norm_w))
    h, hn = _outproj(mix, w_out, x2d, norm_mlp_w.astype(F32)[None, :])
    out = _mlp(hn, w_up, w_down, h, norm_final_w.astype(F32)[None, :])
    return out.reshape(BATCH, SEQ, D_MODEL)
```

```python
import functools

import numpy as np
import jax
import jax.numpy as jnp
from jax import lax
from jax.experimental import pallas as pl
from jax.experimental.pallas import tpu as pltpu

F32 = jnp.float32
BF16 = jnp.bfloat16

D_MODEL = 2048
BATCH = 4
SEQ = 2048
TOKENS = BATCH * SEQ
RET_HEADS = 4
RET_DIM = 256
RET_WIDTH = RET_HEADS * RET_DIM
ROPE_BASE = 10000.0
SSD_INNER = 1024
SSD_HEAD_DIM = 64
SSD_HEADS = 16
SSD_GROUPS = 2
SSD_STATE = 128
SSD_CONV = 4
SSD_CONV_DIM = SSD_INNER + 2 * SSD_GROUPS * SSD_STATE
CHUNK = 128
NUM_CHUNKS = SEQ // CHUNK
PROJ_WIDTH = 4 * RET_WIDTH + SSD_INNER + SSD_CONV_DIM
DT_PAD = 128
D_FF = 4 * D_MODEL
EPS = 1e-6

VMEM_LIMIT = 56 * 1024 * 1024

_NT = (((1,), (1,)), ((), ()))
_TN = (((0,), (0,)), ((), ()))


def _rms_scale(x):
    return lax.rsqrt(jnp.mean(x * x, axis=-1, keepdims=True) + EPS)


def _ret_gammas():
    return 1.0 - 2.0 ** (-5.0 - np.arange(RET_HEADS, dtype=np.float64))


def _ret_tables():
    lg = np.log(_ret_gammas())
    idx = np.arange(CHUNK, dtype=np.float64)
    rel = idx[:, None] - idx[None, :]
    causal = rel >= 0
    dintra = np.where(causal[None], np.exp(np.where(causal, rel, 0.0)[None] * lg[:, None, None]), 0.0)
    qdec = np.exp((idx + 1.0)[:, None] * lg[None, :])
    kdec = np.exp((CHUNK - 1.0 - idx)[:, None] * lg[None, :])
    qkdec = np.concatenate([np.repeat(qdec, RET_DIM, axis=1), np.repeat(kdec, RET_DIM, axis=1)], axis=1)
    return jnp.asarray(dintra, F32), jnp.asarray(qkdec, F32)


def _rope_tables():
    half = RET_DIM // 2
    inv_freq = ROPE_BASE ** (-jnp.arange(half, dtype=F32) / half)
    ang = jnp.arange(SEQ, dtype=F32)[:, None] * inv_freq[None, :]
    return jnp.cos(ang), jnp.sin(ang)


IN_TM = 2048
IN_TN = 512
IN_NT = PROJ_WIDTH // IN_TN
IN_MT = TOKENS // IN_TM
IN_TILES = IN_MT * IN_NT
IN_STEPS = IN_NT + IN_TILES + 1
IN_RB = 256
IN_NCH = 8
IN_CH = IN_TM // IN_NCH
IN_SEQ_TILES = SEQ // IN_TM
IN_N_ROPE = 2 * RET_WIDTH // IN_TN
IN_N_K = RET_WIDTH // IN_TN
IN_N_GATE = (3 * RET_WIDTH // IN_TN, (4 * RET_WIDTH + SSD_INNER) // IN_TN)


def _in_tile(s, lag):
    t = jnp.clip(s - IN_NT - lag, 0, IN_TILES - 1)
    return t // IN_NT, t % IN_NT


def _inproj_kernel(x_ref, nw_ref, w_ref, wdt_ref, cos_ref, sin_ref, dec_ref, cw_ref, cb_ref,
                   proj_ref, qkd_ref, dt_ref, hn_ref, acc_ref, raw_ref, carry_ref):
    s = pl.program_id(0)
    t = s - IN_NT
    m, n = _in_tile(s, 0)
    pm, pn = _in_tile(s, 1)

    def normalise():
        slot = (s // IN_NT) % 2
        r = pl.multiple_of(jnp.minimum(s % IN_NT, IN_NCH - 1) * IN_CH, IN_CH)
        x = x_ref[...]
        hn_ref[slot, pl.ds(r, IN_CH), :] = (x * _rms_scale(x) * nw_ref[...]).astype(BF16)

    def matmul_rows(rb, wbf):
        rows = pl.ds(rb * IN_RB, IN_RB)
        acc_ref[rows, :] = lax.dot_general(
            hn_ref[m % 2, rows, :], wbf, _NT, preferred_element_type=F32)

    def epilogue_rows(kind, rb):
        rows = pl.ds(rb * IN_RB, IN_RB)
        a = raw_ref[pl.ds(8 + rb * IN_RB, IN_RB), :]
        if kind == "rope":
            a = a * jnp.where(pn >= IN_N_K, RET_DIM ** -0.5, 1.0)
            cos, sin = cos_ref[rows, :], sin_ref[rows, :]
            half = RET_DIM // 2
            parts = []
            for hh in range(IN_TN // RET_DIM):
                x1 = a[:, hh * RET_DIM:hh * RET_DIM + half]
                x2 = a[:, hh * RET_DIM + half:(hh + 1) * RET_DIM]
                parts += [x1 * cos - x2 * sin, x1 * sin + x2 * cos]
            r = jnp.concatenate(parts, axis=-1)
            proj_ref[rows, :] = r.astype(BF16)
            rd = r.reshape(IN_RB // CHUNK, CHUNK, IN_TN) * dec_ref[...][None]
            qkd_ref[rows, :] = rd.reshape(IN_RB, IN_TN).astype(BF16)
        elif kind == "plain":
            proj_ref[rows, :] = a.astype(BF16)
        elif kind == "silu":
            proj_ref[rows, :] = jax.nn.silu(a).astype(BF16)
        else:
            conv = cb_ref[...] + cw_ref[SSD_CONV - 1:SSD_CONV, :] * a
            for tap in range(SSD_CONV - 1):
                off = 8 - (SSD_CONV - 1) + tap + rb * IN_RB
                conv = conv + cw_ref[tap:tap + 1, :] * raw_ref[pl.ds(off, IN_RB), :]
            proj_ref[rows, :] = jax.nn.silu(conv).astype(BF16)

    def step(do_matmul, kind):
        if kind is not None:
            raw_ref[8:8 + IN_TM, :] = acc_ref[...]
        if kind == "conv":
            j = pn - IN_N_GATE[1]
            raw_ref[0:8, :] = jnp.where(pm % IN_SEQ_TILES == 0, 0.0, carry_ref[j])
        if do_matmul:
            wbf = w_ref[...].astype(BF16)
        for rb in range(IN_TM // IN_RB):
            if do_matmul:
                matmul_rows(rb, wbf)
            if kind is not None:
                epilogue_rows(kind, rb)
            if do_matmul and rb == 0:
                normalise()
        if kind == "conv":
            carry_ref[j] = raw_ref[IN_TM:IN_TM + 8, :]

    @pl.when((t >= 0) & (t < IN_TILES) & (n == 0))
    def _():
        dt_ref[...] = lax.dot_general(hn_ref[m % 2], wdt_ref[...].astype(BF16), _NT,
                                      preferred_element_type=F32)

    @pl.when(s < IN_NT)
    def _():
        normalise()

    @pl.when(t == 0)
    def _():
        step(True, None)

    live = (t >= 1) & (t < IN_TILES)

    @pl.when(live & (pn < IN_N_ROPE))
    def _():
        step(True, "rope")

    @pl.when(live & (pn >= IN_N_ROPE) & (pn < IN_N_GATE[0]))
    def _():
        step(True, "plain")

    @pl.when(live & (pn >= IN_N_GATE[0]) & (pn < IN_N_GATE[1]))
    def _():
        step(True, "silu")

    @pl.when(live & (pn >= IN_N_GATE[1]))
    def _():
        step(True, "conv")

    @pl.when(t == IN_TILES)
    def _():
        step(False, "conv")


def _inproj(x2d, norm_w, w_main, w_dt, conv_w, conv_b):
    cos, sin = _rope_tables()
    _, qkdec = _ret_tables()
    n_conv = SSD_CONV_DIM // IN_TN
    rope_rows = lambda s: (_in_tile(s, 1)[0] % IN_SEQ_TILES, 0)
    conv_tile = lambda s: (0, jnp.maximum(_in_tile(s, 1)[1] - IN_N_GATE[1], 0))
    return pl.pallas_call(
        _inproj_kernel,
        grid=(IN_STEPS,),
        in_specs=[
            pl.BlockSpec((IN_CH, D_MODEL),
                         lambda s: (jnp.minimum(s // IN_NT, IN_MT - 1) * IN_NCH
                                    + jnp.minimum(s % IN_NT, IN_NCH - 1), 0)),
            pl.BlockSpec((1, D_MODEL), lambda s: (0, 0)),
            pl.BlockSpec((IN_TN, D_MODEL), lambda s: (_in_tile(s, 0)[1], 0)),
            pl.BlockSpec((DT_PAD, D_MODEL), lambda s: (0, 0)),
            pl.BlockSpec((IN_TM, RET_DIM // 2), rope_rows),
            pl.BlockSpec((IN_TM, RET_DIM // 2), rope_rows),
            pl.BlockSpec((CHUNK, IN_TN), lambda s: (0, jnp.minimum(_in_tile(s, 1)[1], IN_N_ROPE - 1))),
            pl.BlockSpec((SSD_CONV, IN_TN), conv_tile),
            pl.BlockSpec((1, IN_TN), conv_tile),
        ],
        out_specs=[
            pl.BlockSpec((IN_TM, IN_TN), lambda s: _in_tile(s, 1)),
            pl.BlockSpec((IN_TM, IN_TN),
                         lambda s: (_in_tile(s, 1)[0], jnp.minimum(_in_tile(s, 1)[1], IN_N_ROPE - 1))),
            pl.BlockSpec((IN_TM, DT_PAD), lambda s: (_in_tile(s, 0)[0], 0)),
        ],
        out_shape=[
            jax.ShapeDtypeStruct((TOKENS, PROJ_WIDTH), BF16),
            jax.ShapeDtypeStruct((TOKENS, 2 * RET_WIDTH), BF16),
            jax.ShapeDtypeStruct((TOKENS, DT_PAD), F32),
        ],
        scratch_shapes=[
            pltpu.VMEM((2, IN_TM, D_MODEL), BF16),
            pltpu.VMEM((IN_TM, IN_TN), F32),
            pltpu.VMEM((IN_TM + 8, IN_TN), F32),
            pltpu.VMEM((n_conv, 8, IN_TN), F32),
        ],
        compiler_params=pltpu.CompilerParams(
            dimension_semantics=("arbitrary",),
            vmem_limit_bytes=VMEM_LIMIT),
        name="inproj",
    )(x2d, norm_w, w_main, w_dt, cos, sin, qkdec, conv_w, conv_b)


def _cumsum_lanes(x):
    lane = lax.broadcasted_iota(jnp.int32, x.shape, 1)
    k = 1
    while k < x.shape[1]:
        x = x + jnp.where(lane >= k, pltpu.roll(x, k, axis=1), 0.0)
        k *= 2
    return x


MIX_NB = 2
MIX_N_BATCHED = 10
MIX_STAGES = 5


def _mixer_kernel(*refs):
    ins, consts = refs[:MIX_N_BATCHED], refs[MIX_N_BATCHED:-3]
    out_ref, rstate, sstate = refs[-3:]

    @pl.when(pl.program_id(1) == 0)
    def _():
        rstate[...] = jnp.zeros_like(rstate)
        sstate[...] = jnp.zeros_like(sstate)

    chains = [_mixer_stages(*[r.at[bi] for r in ins], *consts,
                            out_ref.at[bi], rstate.at[bi], sstate.at[bi])
              for bi in range(MIX_NB)]
    for _ in range(MIX_STAGES):
        for chain in chains:
            next(chain)


def _mixer_stages(q_ref, k_ref, v_ref, g_ref, z_ref, xs_ref, bc_ref, qd_ref, kd_ref, dt_ref,
                  dintra_ref, rnw_ref, dtb_ref, alog_ref,
                  dskip_ref, snw_ref, out_ref, rstate, sstate):
    hpg = SSD_HEADS // SSD_GROUPS
    gw = hpg * SSD_HEAD_DIM
    cbase = SSD_GROUPS * SSD_STATE
    heads = [slice(h * RET_DIM, (h + 1) * RET_DIM) for h in range(RET_HEADS)]
    chunk_decay = _ret_gammas() ** CHUNK

    def rows_of(t, hh):
        return jnp.broadcast_to(t[hh:hh + 1, :], (SSD_HEAD_DIM, CHUNK))

    dt_t = jax.nn.softplus(dt_ref[...].T[0:SSD_HEADS, :] + dtb_ref[...])
    acs_t = _cumsum_lanes(dt_t * (-jnp.exp(alog_ref[...])))
    a_last = acs_t[:, CHUNK - 1:CHUNK]
    w_t = jnp.exp(a_last - acs_t) * dt_t
    ea_t = jnp.exp(acs_t)
    cdec = jnp.broadcast_to(jnp.exp(a_last), (SSD_HEADS, CHUNK))
    acs_pad = jnp.concatenate([acs_t, jnp.zeros((CHUNK - SSD_HEADS, CHUNK), F32)], axis=0)
    acs_col = acs_pad.T
    dskip = dskip_ref[...]
    yield

    scores, ycross, kv = [], [], []
    for h, sl in enumerate(heads):
        scores.append(lax.dot_general(q_ref[:, sl], k_ref[:, sl], _NT, preferred_element_type=F32))
        ycross.append(jnp.dot(qd_ref[:, sl], rstate[h].astype(BF16), preferred_element_type=F32))
        kv.append(lax.dot_general(kd_ref[:, sl], v_ref[:, sl], _TN, preferred_element_type=F32))
    xs_t = xs_ref[...].astype(F32).T
    bgs, cgs, cb_ts, yo_ts, s_prevs = [], [], [], [], []
    for g in range(SSD_GROUPS):
        bg = bc_ref[:, g * SSD_STATE:(g + 1) * SSD_STATE]
        cg = bc_ref[:, cbase + g * SSD_STATE:cbase + (g + 1) * SSD_STATE]
        s_prev = sstate[g * gw:(g + 1) * gw, :]
        bgs.append(bg)
        cgs.append(cg)
        s_prevs.append(s_prev)
        cb_ts.append(lax.dot_general(bg, cg, _NT, preferred_element_type=F32))
        yo_ts.append(lax.dot_general(s_prev.astype(BF16), cg, _NT,
                                     preferred_element_type=F32))
    yield

    ps = [(scores[h] * dintra_ref[h]).astype(BF16) for h in range(RET_HEADS)]
    for h in range(RET_HEADS):
        rstate[h] = float(chunk_decay[h]) * rstate[h] + kv[h]
    row = lax.broadcasted_iota(jnp.int32, (CHUNK, CHUNK), 0)
    col = lax.broadcasted_iota(jnp.int32, (CHUNK, CHUNK), 1)
    causal_t = col >= row
    m_ts, xdts, xws = [], [], []
    for hh in range(SSD_HEADS):
        xs_h = xs_t[hh * SSD_HEAD_DIM:(hh + 1) * SSD_HEAD_DIM, :]
        seg = (jnp.broadcast_to(acs_t[hh:hh + 1, :], (CHUNK, CHUNK))
               - jnp.broadcast_to(acs_col[:, hh:hh + 1], (CHUNK, CHUNK)))
        l_t = jnp.exp(jnp.where(causal_t, seg, -jnp.inf))
        m_ts.append((cb_ts[hh // hpg] * l_t).astype(BF16))
        xdts.append((xs_h * rows_of(dt_t, hh)).astype(BF16))
        xws.append((xs_h * rows_of(w_t, hh)).astype(BF16))
    yield

    ys = [jnp.dot(ps[h], v_ref[:, sl], preferred_element_type=F32) + ycross[h]
          for h, sl in enumerate(heads)]
    yds = [jnp.dot(xdts[hh], m_ts[hh], preferred_element_type=F32)
           for hh in range(SSD_HEADS)]
    for g in range(SSD_GROUPS):
        xw = jnp.concatenate(xws[g * hpg:(g + 1) * hpg], axis=0)
        cd = jnp.concatenate([rows_of(cdec, hh) for hh in range(g * hpg, (g + 1) * hpg)], axis=0)
        sstate[g * gw:(g + 1) * gw, :] = (
            cd * s_prevs[g] + jnp.dot(xw, bgs[g], preferred_element_type=F32))
    yield

    for h, sl in enumerate(heads):
        yn = ys[h] * _rms_scale(ys[h]) * rnw_ref[:, sl]
        out_ref[:, sl] = (yn * g_ref[:, sl].astype(F32)).astype(BF16)
    y_t_parts = []
    for hh in range(SSD_HEADS):
        e = hh % hpg
        xs_h = xs_t[hh * SSD_HEAD_DIM:(hh + 1) * SSD_HEAD_DIM, :]
        yo = yo_ts[hh // hpg][e * SSD_HEAD_DIM:(e + 1) * SSD_HEAD_DIM, :] * rows_of(ea_t, hh)
        y_t_parts.append(yds[hh] + yo + rows_of(dskip, hh) * xs_h)
    y = jnp.concatenate(y_t_parts, axis=0).T
    y = y * z_ref[...].astype(F32)
    for g in range(SSD_GROUPS):
        sl = slice(g * gw, (g + 1) * gw)
        yg = y[:, sl]
        out_ref[:, RET_WIDTH + g * gw:RET_WIDTH + (g + 1) * gw] = (
            yg * _rms_scale(yg) * snw_ref[:, sl]).astype(BF16)
    yield


def _mixer(proj, qkd, dt, params):
    (ret_norm_w, dt_bias, a_log, d_skip, ssd_norm_w) = params
    dintra, _ = _ret_tables()
    proj = proj.reshape(BATCH, SEQ, PROJ_WIDTH)
    qkd = qkd.reshape(BATCH, SEQ, 2 * RET_WIDTH)
    dt = dt.reshape(BATCH, SEQ, DT_PAD)

    def col_block(j, width):
        return pl.BlockSpec((MIX_NB, CHUNK, width), lambda b, c: (b, c, j))

    def full(shape):
        return pl.BlockSpec(shape, lambda b, c: (0,) * len(shape))

    bcast = lambda v: jnp.broadcast_to(v.astype(F32)[:, None], (SSD_HEADS, CHUNK))
    in_specs = [
        col_block(0, RET_WIDTH), col_block(1, RET_WIDTH), col_block(2, RET_WIDTH),
        col_block(3, RET_WIDTH), col_block(4, SSD_INNER), col_block(5, SSD_INNER),
        col_block(12, 2 * SSD_GROUPS * SSD_STATE),
        col_block(0, RET_WIDTH), col_block(1, RET_WIDTH),
        col_block(0, DT_PAD),
        full((RET_HEADS, CHUNK, CHUNK)),
        full((1, RET_WIDTH)),
        full((SSD_HEADS, CHUNK)), full((SSD_HEADS, CHUNK)), full((SSD_HEADS, CHUNK)),
        full((1, SSD_INNER)),
    ]
    assert len(in_specs) - 6 == MIX_N_BATCHED
    mix = pl.pallas_call(
        _mixer_kernel,
        grid=(BATCH // MIX_NB, NUM_CHUNKS),
        in_specs=in_specs,
        out_specs=pl.BlockSpec((MIX_NB, CHUNK, D_MODEL), lambda b, c: (b, c, 0)),
        out_shape=jax.ShapeDtypeStruct((BATCH, SEQ, D_MODEL), BF16),
        scratch_shapes=[
            pltpu.VMEM((MIX_NB, RET_HEADS, RET_DIM, RET_DIM), F32),
            pltpu.VMEM((MIX_NB, SSD_INNER, SSD_STATE), F32),
        ],
        compiler_params=pltpu.CompilerParams(
            dimension_semantics=("arbitrary", "arbitrary"),
            vmem_limit_bytes=VMEM_LIMIT),
        name="mixer",
    )(proj, proj, proj, proj, proj, proj, proj, qkd, qkd, dt, dintra,
      ret_norm_w.astype(F32)[None, :],
      bcast(dt_bias), bcast(a_log), bcast(d_skip), ssd_norm_w.astype(F32)[None, :])
    return mix.reshape(TOKENS, D_MODEL)


OUT_TM = 512


def _outproj_kernel(mix_ref, w_ref, x_ref, nw_ref, h_ref, hn_ref):
    h = x_ref[...] + jnp.dot(mix_ref[...], w_ref[...].astype(BF16),
                             preferred_element_type=F32)
    h_ref[...] = h
    hn_ref[...] = (h * _rms_scale(h) * nw_ref[...]).astype(BF16)


def _outproj(mix, w_out, x2d, norm_w):
    return pl.pallas_call(
        _outproj_kernel,
        grid=(TOKENS // OUT_TM,),
        in_specs=[
            pl.BlockSpec((OUT_TM, D_MODEL), lambda m: (m, 0)),
            pl.BlockSpec((D_MODEL, D_MODEL), lambda m: (0, 0), pipeline_mode=pl.Buffered(1)),
            pl.BlockSpec((OUT_TM, D_MODEL), lambda m: (m, 0)),
            pl.BlockSpec((1, D_MODEL), lambda m: (0, 0)),
        ],
        out_specs=[
            pl.BlockSpec((OUT_TM, D_MODEL), lambda m: (m, 0)),
            pl.BlockSpec((OUT_TM, D_MODEL), lambda m: (m, 0)),
        ],
        out_shape=[
            jax.ShapeDtypeStruct((TOKENS, D_MODEL), F32),
            jax.ShapeDtypeStruct((TOKENS, D_MODEL), BF16),
        ],
        compiler_params=pltpu.CompilerParams(
            dimension_semantics=("arbitrary",),
            vmem_limit_bytes=VMEM_LIMIT),
        name="outproj",
    )(mix, w_out, x2d, norm_w)


MLP_TM = 1024
MLP_TF = 512
MLP_NF = D_FF // MLP_TF
MLP_HROWS = MLP_TM // MLP_NF
MLP_ROWS = 128


def _mlp_kernel(hn_ref, wup_ref, wdn_ref, h_ref, nw_ref, out_ref):
    f = pl.program_id(1)

    @pl.when(f == 0)
    def _():
        out_ref[...] = jnp.zeros_like(out_ref)

    u = jnp.maximum(jnp.dot(hn_ref[...], wup_ref[...].astype(BF16),
                            preferred_element_type=F32), 0.0)
    out_ref[...] += jnp.dot((u * u).astype(BF16), wdn_ref[...].astype(BF16),
                            preferred_element_type=F32)
    r = pl.multiple_of(f * MLP_HROWS, MLP_HROWS)
    out_ref[pl.ds(r, MLP_HROWS), :] += h_ref[...]

    @pl.when(f == MLP_NF - 1)
    def _():
        def body(i, carry):
            r = pl.multiple_of(i * MLP_ROWS, MLP_ROWS)
            h = out_ref[pl.ds(r, MLP_ROWS), :]
            out_ref[pl.ds(r, MLP_ROWS), :] = h * _rms_scale(h) * nw_ref[...]
            return carry
        lax.fori_loop(0, MLP_TM // MLP_ROWS, body, 0)


def _mlp(hn, w_up, w_down, h, norm_w):
    return pl.pallas_call(
        _mlp_kernel,
        grid=(TOKENS // MLP_TM, MLP_NF),
        in_specs=[
            pl.BlockSpec((MLP_TM, D_MODEL), lambda m, f: (m, 0)),
            pl.BlockSpec((D_MODEL, MLP_TF), lambda m, f: (0, f)),
            pl.BlockSpec((MLP_TF, D_MODEL), lambda m, f: (f, 0)),
            pl.BlockSpec((MLP_HROWS, D_MODEL), lambda m, f: (m * MLP_NF + f, 0)),
            pl.BlockSpec((1, D_MODEL), lambda m, f: (0, 0)),
        ],
        out_specs=pl.BlockSpec((MLP_TM, D_MODEL), lambda m, f: (m, 0)),
        out_shape=jax.ShapeDtypeStruct((TOKENS, D_MODEL), F32),
        compiler_params=pltpu.CompilerParams(
            dimension_semantics=("arbitrary", "arbitrary"),
            vmem_limit_bytes=VMEM_LIMIT),
        name="mlp",
    )(hn, w_up, w_down, h, norm_w)


def kernel(x, norm_mix_w, w_in, ret_norm_w, conv_w, conv_b, dt_bias, a_log, d_skip, ssd_norm_w,
           w_out, norm_mlp_w, w_up, w_down, norm_final_w):
    x2d = x.reshape(TOKENS, D_MODEL)
    w_in_t = w_in.T
    w_dt_t = jnp.pad(w_in_t[PROJ_WIDTH:, :], ((0, DT_PAD - SSD_HEADS), (0, 0)))
    proj, qkd, dt = _inproj(x2d, norm_mix_w.astype(F32)[None, :], w_in_t, w_dt_t,
                            conv_w.astype(F32), conv_b.astype(F32)[None, :])
    mix = _mixer(proj, qkd, dt, (ret_norm_w, dt_bias, a_log, d_skip, ssd_norm_w))
    h, hn = _outproj(mix, w_out, x2d, norm_mlp_w.astype(F32)[None, :])
    out = _mlp(hn, w_up, w_down, h, norm_final_w.astype(F32)[None, :])
    return out.reshape(BATCH, SEQ, D_MODEL)
```

```python
import functools

import numpy as np
import jax
import jax.numpy as jnp
from jax import lax
from jax.experimental import pallas as pl
from jax.experimental.pallas import tpu as pltpu

F32 = jnp.float32
BF16 = jnp.bfloat16

D_MODEL = 2048
BATCH = 4
SEQ = 2048
TOKENS = BATCH * SEQ
RET_HEADS = 4
RET_DIM = 256
RET_WIDTH = RET_HEADS * RET_DIM
ROPE_BASE = 10000.0
SSD_INNER = 1024
SSD_HEAD_DIM = 64
SSD_HEADS = 16
SSD_GROUPS = 2
SSD_STATE = 128
SSD_CONV = 4
SSD_CONV_DIM = SSD_INNER + 2 * SSD_GROUPS * SSD_STATE
CHUNK = 128
NUM_CHUNKS = SEQ // CHUNK
PROJ_WIDTH = 4 * RET_WIDTH + SSD_INNER + SSD_CONV_DIM
DT_PAD = 128
D_FF = 4 * D_MODEL
EPS = 1e-6

VMEM_LIMIT = 56 * 1024 * 1024

_NT = (((1,), (1,)), ((), ()))
_TN = (((0,), (0,)), ((), ()))


def _rms_scale(x):
    return lax.rsqrt(jnp.mean(x * x, axis=-1, keepdims=True) + EPS)


def _ret_gammas():
    return 1.0 - 2.0 ** (-5.0 - np.arange(RET_HEADS, dtype=np.float64))


def _ret_tables():
    lg = np.log(_ret_gammas())
    idx = np.arange(CHUNK, dtype=np.float64)
    rel = idx[:, None] - idx[None, :]
    causal = rel >= 0
    dintra = np.where(causal[None], np.exp(np.where(causal, rel, 0.0)[None] * lg[:, None, None]), 0.0)
    qdec = np.exp((idx + 1.0)[:, None] * lg[None, :])
    kdec = np.exp((CHUNK - 1.0 - idx)[:, None] * lg[None, :])
    qkdec = np.concatenate([np.repeat(qdec, RET_DIM, axis=1), np.repeat(kdec, RET_DIM, axis=1)], axis=1)
    return jnp.asarray(dintra, F32), jnp.asarray(qkdec, F32)


def _rope_tables():
    half = RET_DIM // 2
    inv_freq = ROPE_BASE ** (-jnp.arange(half, dtype=F32) / half)
    ang = jnp.arange(SEQ, dtype=F32)[:, None] * inv_freq[None, :]
    return jnp.cos(ang), jnp.sin(ang)


IN_TM = 2048
IN_TN = 512
IN_NT = PROJ_WIDTH // IN_TN
IN_MT = TOKENS // IN_TM
IN_TILES = IN_MT * IN_NT
IN_STEPS = IN_NT + IN_TILES + 1
IN_RB = 256
IN_NCH = 8
IN_CH = IN_TM // IN_NCH
IN_SEQ_TILES = SEQ // IN_TM
IN_N_ROPE = 2 * RET_WIDTH // IN_TN
IN_N_K = RET_WIDTH // IN_TN
IN_N_GATE = (3 * RET_WIDTH // IN_TN, (4 * RET_WIDTH + SSD_INNER) // IN_TN)


def _in_divmod(s):
    assert IN_NT == 13
    q = lax.shift_right_logical(s * 5042, 16)
    return q, s - q * IN_NT


def _in_tile(s, lag):
    return _in_divmod(jnp.clip(s - IN_NT - lag, 0, IN_TILES - 1))


def _inproj_kernel(x_ref, nw_ref, w_ref, wdt_ref, cos_ref, sin_ref, dec_ref, cw_ref, cb_ref,
                   proj_ref, qkd_ref, dt_ref, hn_ref, acc_ref, raw_ref, carry_ref):
    s = pl.program_id(0)
    t = s - IN_NT
    m, n = _in_tile(s, 0)
    pm, pn = _in_tile(s, 1)

    def normalise():
        row, col = _in_divmod(s)
        slot = row % 2
        r = pl.multiple_of(jnp.minimum(col, IN_NCH - 1) * IN_CH, IN_CH)
        x = x_ref[...]
        hn_ref[slot, pl.ds(r, IN_CH), :] = (x * _rms_scale(x) * nw_ref[...]).astype(BF16)

    def matmul_rows(rb, wbf):
        rows = pl.ds(rb * IN_RB, IN_RB)
        acc_ref[rows, :] = lax.dot_general(
            hn_ref[m % 2, rows, :], wbf, _NT, preferred_element_type=F32)

    def epilogue_rows(kind, rb):
        rows = pl.ds(rb * IN_RB, IN_RB)
        a = raw_ref[pl.ds(8 + rb * IN_RB, IN_RB), :]
        if kind == "rope":
            a = a * jnp.where(pn >= IN_N_K, RET_DIM ** -0.5, 1.0)
            cos, sin = cos_ref[rows, :], sin_ref[rows, :]
            half = RET_DIM // 2
            parts = []
            for hh in range(IN_TN // RET_DIM):
                x1 = a[:, hh * RET_DIM:hh * RET_DIM + half]
                x2 = a[:, hh * RET_DIM + half:(hh + 1) * RET_DIM]
                parts += [x1 * cos - x2 * sin, x1 * sin + x2 * cos]
            r = jnp.concatenate(parts, axis=-1)
            proj_ref[rows, :] = r.astype(BF16)
            rd = r.reshape(IN_RB // CHUNK, CHUNK, IN_TN) * dec_ref[...][None]
            qkd_ref[rows, :] = rd.reshape(IN_RB, IN_TN).astype(BF16)
        elif kind == "plain":
            proj_ref[rows, :] = a.astype(BF16)
        elif kind == "silu":
            proj_ref[rows, :] = jax.nn.silu(a).astype(BF16)
        else:
            conv = cb_ref[...] + cw_ref[SSD_CONV - 1:SSD_CONV, :] * a
            for tap in range(SSD_CONV - 1):
                off = 8 - (SSD_CONV - 1) + tap + rb * IN_RB
                conv = conv + cw_ref[tap:tap + 1, :] * raw_ref[pl.ds(off, IN_RB), :]
            proj_ref[rows, :] = jax.nn.silu(conv).astype(BF16)

    def step(do_matmul, kind):
        if kind is not None:
            raw_ref[8:8 + IN_TM, :] = acc_ref[...]
        if kind == "conv":
            j = pn - IN_N_GATE[1]
            raw_ref[0:8, :] = jnp.where(pm % IN_SEQ_TILES == 0, 0.0, carry_ref[j])
        if do_matmul:
            wbf = w_ref[...].astype(BF16)
        for rb in range(IN_TM // IN_RB):
            if do_matmul:
                matmul_rows(rb, wbf)
            if kind is not None:
                epilogue_rows(kind, rb)
            if do_matmul and rb == 0:
                normalise()
        if kind == "conv":
            carry_ref[j] = raw_ref[IN_TM:IN_TM + 8, :]

    @pl.when((t >= 0) & (t < IN_TILES) & (n == 0))
    def _():
        dt_ref[...] = lax.dot_general(hn_ref[m % 2], wdt_ref[...].astype(BF16), _NT,
                                      preferred_element_type=F32)

    @pl.when(s < IN_NT)
    def _():
        normalise()

    @pl.when(t == 0)
    def _():
        step(True, None)

    live = (t >= 1) & (t < IN_TILES)

    @pl.when(live & (pn < IN_N_ROPE))
    def _():
        step(True, "rope")

    @pl.when(live & (pn >= IN_N_ROPE) & (pn < IN_N_GATE[0]))
    def _():
        step(True, "plain")

    @pl.when(live & (pn >= IN_N_GATE[0]) & (pn < IN_N_GATE[1]))
    def _():
        step(True, "silu")

    @pl.when(live & (pn >= IN_N_GATE[1]))
    def _():
        step(True, "conv")

    @pl.when(t == IN_TILES)
    def _():
        step(False, "conv")


def _inproj(x2d, norm_w, w_main, w_dt, conv_w, conv_b):
    cos, sin = _rope_tables()
    _, qkdec = _ret_tables()
    n_conv = SSD_CONV_DIM // IN_TN
    rope_rows = lambda s: (_in_tile(s, 1)[0] % IN_SEQ_TILES, 0)
    conv_tile = lambda s: (0, jnp.maximum(_in_tile(s, 1)[1] - IN_N_GATE[1], 0))
    return pl.pallas_call(
        _inproj_kernel,
        grid=(IN_STEPS,),
        in_specs=[
            pl.BlockSpec((IN_CH, D_MODEL),
                         lambda s: (jnp.minimum(_in_divmod(s)[0], IN_MT - 1) * IN_NCH
                                    + jnp.minimum(_in_divmod(s)[1], IN_NCH - 1), 0)),
            pl.BlockSpec((1, D_MODEL), lambda s: (0, 0)),
            pl.BlockSpec((IN_TN, D_MODEL), lambda s: (_in_tile(s, 0)[1], 0)),
            pl.BlockSpec((DT_PAD, D_MODEL), lambda s: (0, 0)),
            pl.BlockSpec((IN_TM, RET_DIM // 2), rope_rows),
            pl.BlockSpec((IN_TM, RET_DIM // 2), rope_rows),
            pl.BlockSpec((CHUNK, IN_TN), lambda s: (0, jnp.minimum(_in_tile(s, 1)[1], IN_N_ROPE - 1))),
            pl.BlockSpec((SSD_CONV, IN_TN), conv_tile),
            pl.BlockSpec((1, IN_TN), conv_tile),
        ],
        out_specs=[
            pl.BlockSpec((IN_TM, IN_TN), lambda s: _in_tile(s, 1)),
            pl.BlockSpec((IN_TM, IN_TN),
                         lambda s: (_in_tile(s, 1)[0], jnp.minimum(_in_tile(s, 1)[1], IN_N_ROPE - 1))),
            pl.BlockSpec((IN_TM, DT_PAD), lambda s: (_in_tile(s, 0)[0], 0)),
        ],
        out_shape=[
            jax.ShapeDtypeStruct((TOKENS, PROJ_WIDTH), BF16),
            jax.ShapeDtypeStruct((TOKENS, 2 * RET_WIDTH), BF16),
            jax.ShapeDtypeStruct((TOKENS, DT_PAD), F32),
        ],
        scratch_shapes=[
            pltpu.VMEM((2, IN_TM, D_MODEL), BF16),
            pltpu.VMEM((IN_TM, IN_TN), F32),
            pltpu.VMEM((IN_TM + 8, IN_TN), F32),
            pltpu.VMEM((n_conv, 8, IN_TN), F32),
        ],
        compiler_params=pltpu.CompilerParams(
            dimension_semantics=("arbitrary",),
            vmem_limit_bytes=VMEM_LIMIT),
        name="inproj",
    )(x2d, norm_w, w_main, w_dt, cos, sin, qkdec, conv_w, conv_b)


def _cumsum_lanes(x):
    lane = lax.broadcasted_iota(jnp.int32, x.shape, 1)
    k = 1
    while k < x.shape[1]:
        x = x + jnp.where(lane >= k, pltpu.roll(x, k, axis=1), 0.0)
        k *= 2
    return x


MIX_NB = 2
MIX_N_BATCHED = 10
MIX_STAGES = 5


MIX_N_CONSTS = 6
OUT_NB = 4


def _mixer_kernel(*refs):
    ins = refs[:MIX_N_BATCHED]
    consts = refs[MIX_N_BATCHED:MIX_N_BATCHED + MIX_N_CONSTS]
    x_ref, wout_ref, nw_ref, h_ref, hn_ref, rstate, sstate, mix_ref, wbf_ref = (
        refs[MIX_N_BATCHED + MIX_N_CONSTS:])
    b, c = pl.program_id(0), pl.program_id(1)

    @pl.when((b == 0) & (c == 0))
    def _():
        wbf_ref[...] = wout_ref[...].astype(BF16)

    @pl.when(c == 0)
    def _():
        rstate[...] = jnp.zeros_like(rstate)
        sstate[...] = jnp.zeros_like(sstate)

    def step(do_mixer, do_outproj):
        chains = []
        if do_mixer:
            chains += [_mixer_stages(*[r.at[bi] for r in ins], *consts,
                                     mix_ref.at[bi], rstate.at[bi], sstate.at[bi])
                       for bi in range(MIX_NB)]
        if do_outproj:
            chains.append(_outproj_stages(mix_ref, wbf_ref, x_ref, nw_ref, h_ref, hn_ref))
        for _ in range(MIX_STAGES):
            for chain in chains:
                next(chain)

    @pl.when(c == 0)
    def _():
        step(True, False)

    @pl.when((c > 0) & (c < NUM_CHUNKS))
    def _():
        step(True, True)

    @pl.when(c == NUM_CHUNKS)
    def _():
        step(False, True)


def _outproj_stages(mix_ref, wbf_ref, x_ref, nw_ref, h_ref, hn_ref):
    assert OUT_NB + 1 == MIX_STAGES
    lhs = jnp.concatenate([mix_ref[bi] for bi in range(MIX_NB)], axis=0)
    nbw = D_MODEL // OUT_NB
    ssq = [0.0] * MIX_NB
    for nb in range(OUT_NB):
        cols = slice(nb * nbw, (nb + 1) * nbw)
        acc = jnp.dot(lhs, wbf_ref[:, cols], preferred_element_type=F32)
        for bi in range(MIX_NB):
            h = x_ref[bi, :, cols] + acc[bi * CHUNK:(bi + 1) * CHUNK, :]
            h_ref[bi, :, cols] = h
            ssq[bi] = ssq[bi] + jnp.sum(h * h, axis=-1, keepdims=True)
        yield
    for bi in range(MIX_NB):
        scale = lax.rsqrt(ssq[bi] * (1.0 / D_MODEL) + EPS)
        hn_ref[bi] = (h_ref[bi] * scale * nw_ref[...]).astype(BF16)
    yield


def _mixer_stages(q_ref, k_ref, v_ref, g_ref, z_ref, xs_ref, bc_ref, qd_ref, kd_ref, dt_ref,
                  dintra_ref, rnw_ref, dtb_ref, alog_ref,
                  dskip_ref, snw_ref, out_ref, rstate, sstate):
    hpg = SSD_HEADS // SSD_GROUPS
    gw = hpg * SSD_HEAD_DIM
    cbase = SSD_GROUPS * SSD_STATE
    heads = [slice(h * RET_DIM, (h + 1) * RET_DIM) for h in range(RET_HEADS)]
    chunk_decay = _ret_gammas() ** CHUNK

    def rows_of(t, hh):
        return jnp.broadcast_to(t[hh:hh + 1, :], (SSD_HEAD_DIM, CHUNK))

    dt_t = jax.nn.softplus(dt_ref[...].T[0:SSD_HEADS, :] + dtb_ref[...])
    acs_t = _cumsum_lanes(dt_t * (-jnp.exp(alog_ref[...])))
    a_last = acs_t[:, CHUNK - 1:CHUNK]
    w_t = jnp.exp(a_last - acs_t) * dt_t
    ea_t = jnp.exp(acs_t)
    cdec = jnp.broadcast_to(jnp.exp(a_last), (SSD_HEADS, CHUNK))
    acs_pad = jnp.concatenate([acs_t, jnp.zeros((CHUNK - SSD_HEADS, CHUNK), F32)], axis=0)
    acs_col = acs_pad.T
    dskip = dskip_ref[...]
    yield

    scores, ycross, kv = [], [], []
    for h, sl in enumerate(heads):
        scores.append(lax.dot_general(q_ref[:, sl], k_ref[:, sl], _NT, preferred_element_type=F32))
        ycross.append(jnp.dot(qd_ref[:, sl], rstate[h].astype(BF16), preferred_element_type=F32))
        kv.append(lax.dot_general(kd_ref[:, sl], v_ref[:, sl], _TN, preferred_element_type=F32))
    xs_t = xs_ref[...].astype(F32).T
    bgs, cgs, cb_ts, yo_ts, s_prevs = [], [], [], [], []
    for g in range(SSD_GROUPS):
        bg = bc_ref[:, g * SSD_STATE:(g + 1) * SSD_STATE]
        cg = bc_ref[:, cbase + g * SSD_STATE:cbase + (g + 1) * SSD_STATE]
        s_prev = sstate[g * gw:(g + 1) * gw, :]
        bgs.append(bg)
        cgs.append(cg)
        s_prevs.append(s_prev)
        cb_ts.append(lax.dot_general(bg, cg, _NT, preferred_element_type=F32))
        yo_ts.append(lax.dot_general(s_prev.astype(BF16), cg, _NT,
                                     preferred_element_type=F32))
    yield

    ps = [(scores[h] * dintra_ref[h]).astype(BF16) for h in range(RET_HEADS)]
    for h in range(RET_HEADS):
        rstate[h] = float(chunk_decay[h]) * rstate[h] + kv[h]
    row = lax.broadcasted_iota(jnp.int32, (CHUNK, CHUNK), 0)
    col = lax.broadcasted_iota(jnp.int32, (CHUNK, CHUNK), 1)
    causal_t = col >= row
    m_ts, xdts, xws = [], [], []
    for hh in range(SSD_HEADS):
        xs_h = xs_t[hh * SSD_HEAD_DIM:(hh + 1) * SSD_HEAD_DIM, :]
        seg = (jnp.broadcast_to(acs_t[hh:hh + 1, :], (CHUNK, CHUNK))
               - jnp.broadcast_to(acs_col[:, hh:hh + 1], (CHUNK, CHUNK)))
        l_t = jnp.exp(jnp.where(causal_t, seg, -jnp.inf))
        m_ts.append((cb_ts[hh // hpg] * l_t).astype(BF16))
        xdts.append((xs_h * rows_of(dt_t, hh)).astype(BF16))
        xws.append((xs_h * rows_of(w_t, hh)).astype(BF16))
    yield

    ys = [jnp.dot(ps[h], v_ref[:, sl], preferred_element_type=F32) + ycross[h]
          for h, sl in enumerate(heads)]
    yds = [jnp.dot(xdts[hh], m_ts[hh], preferred_element_type=F32)
           for hh in range(SSD_HEADS)]
    for g in range(SSD_GROUPS):
        xw = jnp.concatenate(xws[g * hpg:(g + 1) * hpg], axis=0)
        cd = jnp.concatenate([rows_of(cdec, hh) for hh in range(g * hpg, (g + 1) * hpg)], axis=0)
        sstate[g * gw:(g + 1) * gw, :] = (
            cd * s_prevs[g] + jnp.dot(xw, bgs[g], preferred_element_type=F32))
    yield

    for h, sl in enumerate(heads):
        yn = ys[h] * _rms_scale(ys[h]) * rnw_ref[:, sl]
        out_ref[:, sl] = (yn * g_ref[:, sl].astype(F32)).astype(BF16)
    y_t_parts = []
    for hh in range(SSD_HEADS):
        e = hh % hpg
        xs_h = xs_t[hh * SSD_HEAD_DIM:(hh + 1) * SSD_HEAD_DIM, :]
        yo = yo_ts[hh // hpg][e * SSD_HEAD_DIM:(e + 1) * SSD_HEAD_DIM, :] * rows_of(ea_t, hh)
        y_t_parts.append(yds[hh] + yo + rows_of(dskip, hh) * xs_h)
    y = jnp.concatenate(y_t_parts, axis=0).T
    y = y * z_ref[...].astype(F32)
    for g in range(SSD_GROUPS):
        sl = slice(g * gw, (g + 1) * gw)
        yg = y[:, sl]
        out_ref[:, RET_WIDTH + g * gw:RET_WIDTH + (g + 1) * gw] = (
            yg * _rms_scale(yg) * snw_ref[:, sl]).astype(BF16)
    yield


def _mixer(proj, qkd, dt, x, w_out, norm_w, params):
    (ret_norm_w, dt_bias, a_log, d_skip, ssd_norm_w) = params
    dintra, _ = _ret_tables()
    proj = proj.reshape(BATCH, SEQ, PROJ_WIDTH)
    qkd = qkd.reshape(BATCH, SEQ, 2 * RET_WIDTH)
    dt = dt.reshape(BATCH, SEQ, DT_PAD)
    last = NUM_CHUNKS - 1

    def col_block(j, width):
        return pl.BlockSpec((MIX_NB, CHUNK, width), lambda b, c: (b, jnp.minimum(c, last), j))

    def lagged(width):
        return pl.BlockSpec((MIX_NB, CHUNK, width), lambda b, c: (b, jnp.maximum(c - 1, 0), 0))

    def full(shape, **kw):
        return pl.BlockSpec(shape, lambda b, c: (0,) * len(shape), **kw)

    bcast = lambda v: jnp.broadcast_to(v.astype(F32)[:, None], (SSD_HEADS, CHUNK))
    in_specs = [
        col_block(0, RET_WIDTH), col_block(1, RET_WIDTH), col_block(2, RET_WIDTH),
        col_block(3, RET_WIDTH), col_block(4, SSD_INNER), col_block(5, SSD_INNER),
        col_block(12, 2 * SSD_GROUPS * SSD_STATE),
        col_block(0, RET_WIDTH), col_block(1, RET_WIDTH),
        col_block(0, DT_PAD),
        full((RET_HEADS, CHUNK, CHUNK)),
        full((1, RET_WIDTH)),
        full((SSD_HEADS, CHUNK)), full((SSD_HEADS, CHUNK)), full((SSD_HEADS, CHUNK)),
        full((1, SSD_INNER)),
        lagged(D_MODEL),
        full((D_MODEL, D_MODEL), pipeline_mode=pl.Buffered(1)),
        full((1, D_MODEL)),
    ]
    assert len(in_specs) == MIX_N_BATCHED + MIX_N_CONSTS + 3
    h, hn = pl.pallas_call(
        _mixer_kernel,
        grid=(BATCH // MIX_NB, NUM_CHUNKS + 1),
        in_specs=in_specs,
        out_specs=[lagged(D_MODEL), lagged(D_MODEL)],
        out_shape=[
            jax.ShapeDtypeStruct((BATCH, SEQ, D_MODEL), F32),
            jax.ShapeDtypeStruct((BATCH, SEQ, D_MODEL), BF16),
        ],
        scratch_shapes=[
            pltpu.VMEM((MIX_NB, RET_HEADS, RET_DIM, RET_DIM), F32),
            pltpu.VMEM((MIX_NB, SSD_INNER, SSD_STATE), F32),
            pltpu.VMEM((MIX_NB, CHUNK, D_MODEL), BF16),
            pltpu.VMEM((D_MODEL, D_MODEL), BF16),
        ],
        compiler_params=pltpu.CompilerParams(
            dimension_semantics=("arbitrary", "arbitrary"),
            vmem_limit_bytes=VMEM_LIMIT),
        name="mixer",
    )(proj, proj, proj, proj, proj, proj, proj, qkd, qkd, dt, dintra,
      ret_norm_w.astype(F32)[None, :],
      bcast(dt_bias), bcast(a_log), bcast(d_skip), ssd_norm_w.astype(F32)[None, :],
      x, w_out, norm_w)
    return h.reshape(TOKENS, D_MODEL), hn.reshape(TOKENS, D_MODEL)


MLP_TM = 1024
MLP_TF = 512
MLP_NF = D_FF // MLP_TF
MLP_HROWS = MLP_TM // MLP_NF
MLP_ROWS = 128


def _mlp_kernel(hn_ref, wup_ref, wdn_ref, h_ref, nw_ref, out_ref):
    f = pl.program_id(1)

    @pl.when(f == 0)
    def _():
        out_ref[...] = jnp.zeros_like(out_ref)

    u = jnp.maximum(jnp.dot(hn_ref[...], wup_ref[...].astype(BF16),
                            preferred_element_type=F32), 0.0)
    out_ref[...] += jnp.dot((u * u).astype(BF16), wdn_ref[...].astype(BF16),
                            preferred_element_type=F32)
    r = pl.multiple_of(f * MLP_HROWS, MLP_HROWS)
    out_ref[pl.ds(r, MLP_HROWS), :] += h_ref[...]

    @pl.when(f == MLP_NF - 1)
    def _():
        def body(i, carry):
            r = pl.multiple_of(i * MLP_ROWS, MLP_ROWS)
            h = out_ref[pl.ds(r, MLP_ROWS), :]
            out_ref[pl.ds(r, MLP_ROWS), :] = h * _rms_scale(h) * nw_ref[...]
            return carry
        lax.fori_loop(0, MLP_TM // MLP_ROWS, body, 0)


def _mlp(hn, w_up, w_down, h, norm_w):
    return pl.pallas_call(
        _mlp_kernel,
        grid=(TOKENS // MLP_TM, MLP_NF),
        in_specs=[
            pl.BlockSpec((MLP_TM, D_MODEL), lambda m, f: (m, 0)),
            pl.BlockSpec((D_MODEL, MLP_TF), lambda m, f: (0, f)),
            pl.BlockSpec((MLP_TF, D_MODEL), lambda m, f: (f, 0)),
            pl.BlockSpec((MLP_HROWS, D_MODEL), lambda m, f: (m * MLP_NF + f, 0)),
            pl.BlockSpec((1, D_MODEL), lambda m, f: (0, 0)),
        ],
        out_specs=pl.BlockSpec((MLP_TM, D_MODEL), lambda m, f: (m, 0)),
        out_shape=jax.ShapeDtypeStruct((TOKENS, D_MODEL), F32),
        compiler_params=pltpu.CompilerParams(
            dimension_semantics=("arbitrary", "arbitrary"),
            vmem_limit_bytes=VMEM_LIMIT),
        name="mlp",
    )(hn, w_up, w_down, h, norm_w)


def kernel(x, norm_mix_w, w_in, ret_norm_w, conv_w, conv_b, dt_bias, a_log, d_skip, ssd_norm_w,
           w_out, norm_mlp_w, w_up, w_down, norm_final_w):
    x2d = x.reshape(TOKENS, D_MODEL)
    w_in_t = w_in.T
    w_dt_t = jnp.pad(w_in_t[PROJ_WIDTH:, :], ((0, DT_PAD - SSD_HEADS), (0, 0)))
    proj, qkd, dt = _inproj(x2d, norm_mix_w.astype(F32)[None, :], w_in_t, w_dt_t,
                            conv_w.astype(F32), conv_b.astype(F32)[None, :])
    h, hn = _mixer(proj, qkd, dt, x, w_out, norm_mlp_w.astype(F32)[None, :],
                   (ret_norm_w, dt_bias, a_log, d_skip, ssd_norm_w))
    out = _mlp(hn, w_up, w_down, h, norm_final_w.astype(F32)[None, :])
    return out.reshape(BATCH, SEQ, D_MODEL)
```

```python
import functools

import numpy as np
import jax
import jax.numpy as jnp
from jax import lax
from jax.experimental import pallas as pl
from jax.experimental.pallas import tpu as pltpu

F32 = jnp.float32
BF16 = jnp.bfloat16

D_MODEL = 2048
BATCH = 4
SEQ = 2048
TOKENS = BATCH * SEQ
RET_HEADS = 4
RET_DIM = 256
RET_WIDTH = RET_HEADS * RET_DIM
ROPE_BASE = 10000.0
SSD_INNER = 1024
SSD_HEAD_DIM = 64
SSD_HEADS = 16
SSD_GROUPS = 2
SSD_STATE = 128
SSD_CONV = 4
SSD_CONV_DIM = SSD_INNER + 2 * SSD_GROUPS * SSD_STATE
CHUNK = 128
NUM_CHUNKS = SEQ // CHUNK
PROJ_WIDTH = 4 * RET_WIDTH + SSD_INNER + SSD_CONV_DIM
DT_PAD = 128
D_FF = 4 * D_MODEL
EPS = 1e-6

VMEM_LIMIT = 56 * 1024 * 1024

_NT = (((1,), (1,)), ((), ()))
_TN = (((0,), (0,)), ((), ()))


def _rms_scale(x):
    return lax.rsqrt(jnp.mean(x * x, axis=-1, keepdims=True) + EPS)


def _ret_gammas():
    return 1.0 - 2.0 ** (-5.0 - np.arange(RET_HEADS, dtype=np.float64))


def _ret_tables():
    lg = np.log(_ret_gammas())
    idx = np.arange(CHUNK, dtype=np.float64)
    rel = idx[:, None] - idx[None, :]
    causal = rel >= 0
    dintra = np.where(causal[None], np.exp(np.where(causal, rel, 0.0)[None] * lg[:, None, None]), 0.0)
    qdec = np.exp((idx + 1.0)[:, None] * lg[None, :])
    kdec = np.exp((CHUNK - 1.0 - idx)[:, None] * lg[None, :])
    qkdec = np.concatenate([np.repeat(qdec, RET_DIM, axis=1), np.repeat(kdec, RET_DIM, axis=1)], axis=1)
    return jnp.asarray(dintra, F32), jnp.asarray(qkdec, F32)


def _rope_tables():
    half = RET_DIM // 2
    inv_freq = ROPE_BASE ** (-np.arange(half, dtype=np.float64) / half)
    ang = np.arange(SEQ, dtype=np.float64)[:, None] * inv_freq[None, :]
    return jnp.asarray(np.cos(ang), F32), jnp.asarray(np.sin(ang), F32)


IN_TM = 2048
IN_TN = 512
IN_NT = PROJ_WIDTH // IN_TN
IN_MT = TOKENS // IN_TM
IN_TILES = IN_MT * IN_NT
IN_STEPS = IN_NT + IN_TILES + 1
IN_RB = 256
IN_NCH = 8
IN_CH = IN_TM // IN_NCH
IN_SEQ_TILES = SEQ // IN_TM
IN_N_ROPE = 2 * RET_WIDTH // IN_TN
IN_N_K = RET_WIDTH // IN_TN
IN_N_GATE = (3 * RET_WIDTH // IN_TN, (4 * RET_WIDTH + SSD_INNER) // IN_TN)


def _in_divmod(s):
    assert IN_NT == 13
    q = lax.shift_right_logical(s * 5042, 16)
    return q, s - q * IN_NT


def _in_tile(s, lag):
    return _in_divmod(jnp.clip(s - IN_NT - lag, 0, IN_TILES - 1))


def _inproj_kernel(x_ref, nw_ref, w_ref, wdt_ref, cos_ref, sin_ref, dec_ref, cw_ref, cb_ref,
                   proj_ref, qkd_ref, dt_ref, hn_ref, acc_ref, raw_ref, carry_ref):
    s = pl.program_id(0)
    t = s - IN_NT
    m, n = _in_tile(s, 0)
    pm, pn = _in_tile(s, 1)

    def normalise():
        row, col = _in_divmod(s)
        slot = row % 2
        r = pl.multiple_of(jnp.minimum(col, IN_NCH - 1) * IN_CH, IN_CH)
        x = x_ref[...]
        hn_ref[slot, pl.ds(r, IN_CH), :] = (x * _rms_scale(x) * nw_ref[...]).astype(BF16)

    def matmul_rows(rb, wbf):
        rows = pl.ds(rb * IN_RB, IN_RB)
        acc_ref[rows, :] = lax.dot_general(
            hn_ref[m % 2, rows, :], wbf, _NT, preferred_element_type=F32)

    def epilogue_rows(kind, rb):
        rows = pl.ds(rb * IN_RB, IN_RB)
        a = raw_ref[pl.ds(8 + rb * IN_RB, IN_RB), :]
        if kind == "rope":
            a = a * jnp.where(pn >= IN_N_K, RET_DIM ** -0.5, 1.0)
            cos, sin = cos_ref[rows, :], sin_ref[rows, :]
            half = RET_DIM // 2
            parts = []
            for hh in range(IN_TN // RET_DIM):
                x1 = a[:, hh * RET_DIM:hh * RET_DIM + half]
                x2 = a[:, hh * RET_DIM + half:(hh + 1) * RET_DIM]
                parts += [x1 * cos - x2 * sin, x1 * sin + x2 * cos]
            r = jnp.concatenate(parts, axis=-1)
            proj_ref[rows, :] = r.astype(BF16)
            rd = r.reshape(IN_RB // CHUNK, CHUNK, IN_TN) * dec_ref[...][None]
            qkd_ref[rows, :] = rd.reshape(IN_RB, IN_TN).astype(BF16)
        elif kind == "plain":
            proj_ref[rows, :] = a.astype(BF16)
        elif kind == "silu":
            proj_ref[rows, :] = jax.nn.silu(a).astype(BF16)
        else:
            conv = cb_ref[...] + cw_ref[SSD_CONV - 1:SSD_CONV, :] * a
            for tap in range(SSD_CONV - 1):
                off = 8 - (SSD_CONV - 1) + tap + rb * IN_RB
                conv = conv + cw_ref[tap:tap + 1, :] * raw_ref[pl.ds(off, IN_RB), :]
            proj_ref[rows, :] = jax.nn.silu(conv).astype(BF16)

    def step(do_matmul, kind):
        if kind is not None:
            raw_ref[8:8 + IN_TM, :] = acc_ref[...]
        if kind == "conv":
            j = pn - IN_N_GATE[1]
            raw_ref[0:8, :] = jnp.where(pm % IN_SEQ_TILES == 0, 0.0, carry_ref[j])
        if do_matmul:
            wbf = w_ref[...].astype(BF16)
        for rb in range(IN_TM // IN_RB):
            if do_matmul:
                matmul_rows(rb, wbf)
            if kind is not None:
                epilogue_rows(kind, rb)
            if do_matmul and rb == 0:
                normalise()
        if kind == "conv":
            carry_ref[j] = raw_ref[IN_TM:IN_TM + 8, :]

    @pl.when((t >= 0) & (t < IN_TILES) & (n == 0))
    def _():
        dt_ref[...] = lax.dot_general(hn_ref[m % 2], wdt_ref[...].astype(BF16), _NT,
                                      preferred_element_type=F32)

    @pl.when(s < IN_NT)
    def _():
        normalise()

    @pl.when(t == 0)
    def _():
        step(True, None)

    live = (t >= 1) & (t < IN_TILES)

    @pl.when(live & (pn < IN_N_ROPE))
    def _():
        step(True, "rope")

    @pl.when(live & (pn >= IN_N_ROPE) & (pn < IN_N_GATE[0]))
    def _():
        step(True, "plain")

    @pl.when(live & (pn >= IN_N_GATE[0]) & (pn < IN_N_GATE[1]))
    def _():
        step(True, "silu")

    @pl.when(live & (pn >= IN_N_GATE[1]))
    def _():
        step(True, "conv")

    @pl.when(t == IN_TILES)
    def _():
        step(False, "conv")


def _inproj(x2d, norm_w, w_main, w_dt, conv_w, conv_b):
    cos, sin = _rope_tables()
    _, qkdec = _ret_tables()
    n_conv = SSD_CONV_DIM // IN_TN
    rope_rows = lambda s: (_in_tile(s, 1)[0] % IN_SEQ_TILES, 0)
    conv_tile = lambda s: (0, jnp.maximum(_in_tile(s, 1)[1] - IN_N_GATE[1], 0))
    return pl.pallas_call(
        _inproj_kernel,
        grid=(IN_STEPS,),
        in_specs=[
            pl.BlockSpec((IN_CH, D_MODEL),
                         lambda s: (jnp.minimum(_in_divmod(s)[0], IN_MT - 1) * IN_NCH
                                    + jnp.minimum(_in_divmod(s)[1], IN_NCH - 1), 0)),
            pl.BlockSpec((1, D_MODEL), lambda s: (0, 0)),
            pl.BlockSpec((IN_TN, D_MODEL), lambda s: (_in_tile(s, 0)[1], 0)),
            pl.BlockSpec((DT_PAD, D_MODEL), lambda s: (0, 0)),
            pl.BlockSpec((IN_TM, RET_DIM // 2), rope_rows),
            pl.BlockSpec((IN_TM, RET_DIM // 2), rope_rows),
            pl.BlockSpec((CHUNK, IN_TN), lambda s: (0, jnp.minimum(_in_tile(s, 1)[1], IN_N_ROPE - 1))),
            pl.BlockSpec((SSD_CONV, IN_TN), conv_tile),
            pl.BlockSpec((1, IN_TN), conv_tile),
        ],
        out_specs=[
            pl.BlockSpec((IN_TM, IN_TN), lambda s: _in_tile(s, 1)),
            pl.BlockSpec((IN_TM, IN_TN),
                         lambda s: (_in_tile(s, 1)[0], jnp.minimum(_in_tile(s, 1)[1], IN_N_ROPE - 1))),
            pl.BlockSpec((IN_TM, DT_PAD), lambda s: (_in_tile(s, 0)[0], 0)),
        ],
        out_shape=[
            jax.ShapeDtypeStruct((TOKENS, PROJ_WIDTH), BF16),
            jax.ShapeDtypeStruct((TOKENS, 2 * RET_WIDTH), BF16),
            jax.ShapeDtypeStruct((TOKENS, DT_PAD), F32),
        ],
        scratch_shapes=[
            pltpu.VMEM((2, IN_TM, D_MODEL), BF16),
            pltpu.VMEM((IN_TM, IN_TN), F32),
            pltpu.VMEM((IN_TM + 8, IN_TN), F32),
            pltpu.VMEM((n_conv, 8, IN_TN), F32),
        ],
        compiler_params=pltpu.CompilerParams(
            dimension_semantics=("arbitrary",),
            vmem_limit_bytes=VMEM_LIMIT),
        name="inproj",
    )(x2d, norm_w, w_main, w_dt, cos, sin, qkdec, conv_w, conv_b)


def _cumsum_lanes(x):
    lane = lax.broadcasted_iota(jnp.int32, x.shape, 1)
    k = 1
    while k < x.shape[1]:
        x = x + jnp.where(lane >= k, pltpu.roll(x, k, axis=1), 0.0)
        k *= 2
    return x


MIX_NB = 2
MIX_N_BATCHED = 10
MIX_STAGES = 5


MIX_N_CONSTS = 4
OUT_NB = 4


def _mixer_kernel(*refs):
    ins = refs[:MIX_N_BATCHED]
    consts = refs[MIX_N_BATCHED:MIX_N_BATCHED + MIX_N_CONSTS]
    x_ref, wout_ref, nw_ref, h_ref, hn_ref, rstate, sstate, mix_ref, wbf_ref = (
        refs[MIX_N_BATCHED + MIX_N_CONSTS:])
    b, c = pl.program_id(0), pl.program_id(1)

    @pl.when((b == 0) & (c == 0))
    def _():
        wbf_ref[...] = wout_ref[...].astype(BF16)

    @pl.when(c == 0)
    def _():
        rstate[...] = jnp.zeros_like(rstate)
        sstate[...] = jnp.zeros_like(sstate)

    def step(do_mixer, do_outproj):
        chains = []
        if do_mixer:
            chains += [_mixer_stages(*[r.at[bi] for r in ins], *consts,
                                     mix_ref.at[bi], rstate.at[bi], sstate.at[bi])
                       for bi in range(MIX_NB)]
        if do_outproj:
            chains.append(_outproj_stages(mix_ref, wbf_ref, x_ref, nw_ref, h_ref, hn_ref))
        for _ in range(MIX_STAGES):
            for chain in chains:
                next(chain)

    @pl.when(c == 0)
    def _():
        step(True, False)

    @pl.when((c > 0) & (c < NUM_CHUNKS))
    def _():
        step(True, True)

    @pl.when(c == NUM_CHUNKS)
    def _():
        step(False, True)


def _outproj_stages(mix_ref, wbf_ref, x_ref, nw_ref, h_ref, hn_ref):
    assert OUT_NB + 1 == MIX_STAGES
    lhs = jnp.concatenate([mix_ref[bi] for bi in range(MIX_NB)], axis=0)
    nbw = D_MODEL // OUT_NB
    ssq = [0.0] * MIX_NB
    for nb in range(OUT_NB):
        cols = slice(nb * nbw, (nb + 1) * nbw)
        acc = jnp.dot(lhs, wbf_ref[:, cols], preferred_element_type=F32)
        for bi in range(MIX_NB):
            h = x_ref[bi, :, cols] + acc[bi * CHUNK:(bi + 1) * CHUNK, :]
            h_ref[bi, :, cols] = h
            ssq[bi] = ssq[bi] + jnp.sum(h * h, axis=-1, keepdims=True)
        yield
    for bi in range(MIX_NB):
        scale = lax.rsqrt(ssq[bi] * (1.0 / D_MODEL) + EPS)
        hn_ref[bi] = (h_ref[bi] * scale * nw_ref[...]).astype(BF16)
    yield


def _mixer_stages(q_ref, k_ref, v_ref, g_ref, z_ref, xs_ref, bc_ref, qd_ref, kd_ref, dt_ref,
                  dintra_ref, rnw_ref, hp_ref, snw_ref, out_ref, rstate, sstate):
    hpg = SSD_HEADS // SSD_GROUPS
    gw = hpg * SSD_HEAD_DIM
    cbase = SSD_GROUPS * SSD_STATE
    heads = [slice(h * RET_DIM, (h + 1) * RET_DIM) for h in range(RET_HEADS)]
    chunk_decay = _ret_gammas() ** CHUNK

    def rows_of(t, hh):
        return jnp.broadcast_to(t[hh:hh + 1, :], (SSD_HEAD_DIM, CHUNK))

    dt_t = jax.nn.softplus(dt_ref[...].T[0:SSD_HEADS, :] + hp_ref[0])
    acs_t = _cumsum_lanes(dt_t * (-jnp.exp(hp_ref[1])))
    a_last = acs_t[:, CHUNK - 1:CHUNK]
    w_t = jnp.exp(a_last - acs_t) * dt_t
    ea_t = jnp.exp(acs_t)
    cdec = jnp.broadcast_to(jnp.exp(a_last), (SSD_HEADS, CHUNK))
    acs_pad = jnp.concatenate([acs_t, jnp.zeros((CHUNK - SSD_HEADS, CHUNK), F32)], axis=0)
    acs_col = acs_pad.T
    dskip = hp_ref[2]
    yield

    scores, ycross, kv = [], [], []
    for h, sl in enumerate(heads):
        scores.append(lax.dot_general(q_ref[:, sl], k_ref[:, sl], _NT, preferred_element_type=F32))
        ycross.append(jnp.dot(qd_ref[:, sl], rstate[h].astype(BF16), preferred_element_type=F32))
        kv.append(lax.dot_general(kd_ref[:, sl], v_ref[:, sl], _TN, preferred_element_type=F32))
    xs_t = xs_ref[...].astype(F32).T
    bgs, cgs, cb_ts, yo_ts, s_prevs = [], [], [], [], []
    for g in range(SSD_GROUPS):
        bg = bc_ref[:, g * SSD_STATE:(g + 1) * SSD_STATE]
        cg = bc_ref[:, cbase + g * SSD_STATE:cbase + (g + 1) * SSD_STATE]
        s_prev = sstate[g * gw:(g + 1) * gw, :]
        bgs.append(bg)
        cgs.append(cg)
        s_prevs.append(s_prev)
        cb_ts.append(lax.dot_general(bg, cg, _NT, preferred_element_type=F32))
        yo_ts.append(lax.dot_general(s_prev.astype(BF16), cg, _NT,
                                     preferred_element_type=F32))
    yield

    ps = [(scores[h] * dintra_ref[h]).astype(BF16) for h in range(RET_HEADS)]
    for h in range(RET_HEADS):
        rstate[h] = float(chunk_decay[h]) * rstate[h] + kv[h]
    row = lax.broadcasted_iota(jnp.int32, (CHUNK, CHUNK), 0)
    col = lax.broadcasted_iota(jnp.int32, (CHUNK, CHUNK), 1)
    causal_t = col >= row
    m_ts, xdts, xws = [], [], []
    for hh in range(SSD_HEADS):
        xs_h = xs_t[hh * SSD_HEAD_DIM:(hh + 1) * SSD_HEAD_DIM, :]
        seg = (jnp.broadcast_to(acs_t[hh:hh + 1, :], (CHUNK, CHUNK))
               - jnp.broadcast_to(acs_col[:, hh:hh + 1], (CHUNK, CHUNK)))
        l_t = jnp.exp(jnp.where(causal_t, seg, -jnp.inf))
        m_ts.append((cb_ts[hh // hpg] * l_t).astype(BF16))
        xdts.append((xs_h * rows_of(dt_t, hh)).astype(BF16))
        xws.append((xs_h * rows_of(w_t, hh)).astype(BF16))
    yield

    ys = [jnp.dot(ps[h], v_ref[:, sl], preferred_element_type=F32) + ycross[h]
          for h, sl in enumerate(heads)]
    yds = [jnp.dot(xdts[hh], m_ts[hh], preferred_element_type=F32)
           for hh in range(SSD_HEADS)]
    for g in range(SSD_GROUPS):
        xw = jnp.concatenate(xws[g * hpg:(g + 1) * hpg], axis=0)
        cd = jnp.concatenate([rows_of(cdec, hh) for hh in range(g * hpg, (g + 1) * hpg)], axis=0)
        sstate[g * gw:(g + 1) * gw, :] = (
            cd * s_prevs[g] + jnp.dot(xw, bgs[g], preferred_element_type=F32))
    yield

    for h, sl in enumerate(heads):
        yn = ys[h] * _rms_scale(ys[h]) * rnw_ref[:, sl]
        out_ref[:, sl] = (yn * g_ref[:, sl].astype(F32)).astype(BF16)
    y_t_parts = []
    for hh in range(SSD_HEADS):
        e = hh % hpg
        xs_h = xs_t[hh * SSD_HEAD_DIM:(hh + 1) * SSD_HEAD_DIM, :]
        yo = yo_ts[hh // hpg][e * SSD_HEAD_DIM:(e + 1) * SSD_HEAD_DIM, :] * rows_of(ea_t, hh)
        y_t_parts.append(yds[hh] + yo + rows_of(dskip, hh) * xs_h)
    y = jnp.concatenate(y_t_parts, axis=0).T
    y = y * z_ref[...].astype(F32)
    for g in range(SSD_GROUPS):
        sl = slice(g * gw, (g + 1) * gw)
        yg = y[:, sl]
        out_ref[:, RET_WIDTH + g * gw:RET_WIDTH + (g + 1) * gw] = (
            yg * _rms_scale(yg) * snw_ref[:, sl]).astype(BF16)
    yield


def _mixer(proj, qkd, dt, x, w_out, norm_w, params):
    (ret_norm_w, dt_bias, a_log, d_skip, ssd_norm_w) = params
    dintra, _ = _ret_tables()
    proj = proj.reshape(BATCH, SEQ, PROJ_WIDTH)
    qkd = qkd.reshape(BATCH, SEQ, 2 * RET_WIDTH)
    dt = dt.reshape(BATCH, SEQ, DT_PAD)
    last = NUM_CHUNKS - 1

    def col_block(j, width):
        return pl.BlockSpec((MIX_NB, CHUNK, width), lambda b, c: (b, jnp.minimum(c, last), j))

    def lagged(width):
        return pl.BlockSpec((MIX_NB, CHUNK, width), lambda b, c: (b, jnp.maximum(c - 1, 0), 0))

    def full(shape, **kw):
        return pl.BlockSpec(shape, lambda b, c: (0,) * len(shape), **kw)

    head_params = jnp.broadcast_to(
        jnp.stack([dt_bias, a_log, d_skip]).astype(F32)[:, :, None], (3, SSD_HEADS, CHUNK))
    in_specs = [
        col_block(0, RET_WIDTH), col_block(1, RET_WIDTH), col_block(2, RET_WIDTH),
        col_block(3, RET_WIDTH), col_block(4, SSD_INNER), col_block(5, SSD_INNER),
        col_block(12, 2 * SSD_GROUPS * SSD_STATE),
        col_block(0, RET_WIDTH), col_block(1, RET_WIDTH),
        col_block(0, DT_PAD),
        full((RET_HEADS, CHUNK, CHUNK)),
        full((1, RET_WIDTH)),
        full((3, SSD_HEADS, CHUNK)),
        full((1, SSD_INNER)),
        lagged(D_MODEL),
        full((D_MODEL, D_MODEL), pipeline_mode=pl.Buffered(1)),
        full((1, D_MODEL)),
    ]
    assert len(in_specs) == MIX_N_BATCHED + MIX_N_CONSTS + 3
    h, hn = pl.pallas_call(
        _mixer_kernel,
        grid=(BATCH // MIX_NB, NUM_CHUNKS + 1),
        in_specs=in_specs,
        out_specs=[lagged(D_MODEL), lagged(D_MODEL)],
        out_shape=[
            jax.ShapeDtypeStruct((BATCH, SEQ, D_MODEL), F32),
            jax.ShapeDtypeStruct((BATCH, SEQ, D_MODEL), BF16),
        ],
        scratch_shapes=[
            pltpu.VMEM((MIX_NB, RET_HEADS, RET_DIM, RET_DIM), F32),
            pltpu.VMEM((MIX_NB, SSD_INNER, SSD_STATE), F32),
            pltpu.VMEM((MIX_NB, CHUNK, D_MODEL), BF16),
            pltpu.VMEM((D_MODEL, D_MODEL), BF16),
        ],
        compiler_params=pltpu.CompilerParams(
            dimension_semantics=("arbitrary", "arbitrary"),
            vmem_limit_bytes=VMEM_LIMIT),
        name="mixer",
    )(proj, proj, proj, proj, proj, proj, proj, qkd, qkd, dt, dintra,
      ret_norm_w.astype(F32)[None, :],
      head_params, ssd_norm_w.astype(F32)[None, :],
      x, w_out, norm_w)
    return h.reshape(TOKENS, D_MODEL), hn.reshape(TOKENS, D_MODEL)


MLP_TM = 1024
MLP_TF = 512
MLP_NF = D_FF // MLP_TF
MLP_HROWS = MLP_TM // MLP_NF
MLP_ROWS = 128


def _mlp_kernel(hn_ref, wup_ref, wdn_ref, h_ref, nw_ref, out_ref, unew_ref, uprev_ref):
    f = pl.program_id(1)

    def up():
        u = jnp.maximum(jnp.dot(hn_ref[...], wup_ref[...].astype(BF16),
                                preferred_element_type=F32), 0.0)
        unew_ref[...] = (u * u).astype(BF16)

    def down(first):
        d = jnp.dot(uprev_ref[...], wdn_ref[...].astype(BF16), preferred_element_type=F32)
        if first:
            out_ref[...] = d
        else:
            out_ref[...] += d
        r = pl.multiple_of((f - 1) * MLP_HROWS, MLP_HROWS)
        out_ref[pl.ds(r, MLP_HROWS), :] += h_ref[...]

    @pl.when(f == 0)
    def _():
        up()

    @pl.when(f == 1)
    def _():
        uprev_ref[...] = unew_ref[...]
        down(True)
        up()

    @pl.when((f > 1) & (f < MLP_NF))
    def _():
        uprev_ref[...] = unew_ref[...]
        down(False)
        up()

    @pl.when(f == MLP_NF)
    def _():
        uprev_ref[...] = unew_ref[...]
        down(False)

        def body(i, carry):
            r = pl.multiple_of(i * MLP_ROWS, MLP_ROWS)
            h = out_ref[pl.ds(r, MLP_ROWS), :]
            out_ref[pl.ds(r, MLP_ROWS), :] = h * _rms_scale(h) * nw_ref[...]
            return carry
        lax.fori_loop(0, MLP_TM // MLP_ROWS, body, 0)


def _mlp(hn, w_up, w_down, h, norm_w):
    return pl.pallas_call(
        _mlp_kernel,
        grid=(TOKENS // MLP_TM, MLP_NF + 1),
        in_specs=[
            pl.BlockSpec((MLP_TM, D_MODEL), lambda m, f: (m, 0)),
            pl.BlockSpec((D_MODEL, MLP_TF), lambda m, f: (0, jnp.minimum(f, MLP_NF - 1))),
            pl.BlockSpec((MLP_TF, D_MODEL), lambda m, f: (jnp.maximum(f - 1, 0), 0)),
            pl.BlockSpec((MLP_HROWS, D_MODEL),
                         lambda m, f: (m * MLP_NF + jnp.maximum(f - 1, 0), 0)),
            pl.BlockSpec((1, D_MODEL), lambda m, f: (0, 0)),
        ],
        out_specs=pl.BlockSpec((MLP_TM, D_MODEL), lambda m, f: (m, 0)),
        out_shape=jax.ShapeDtypeStruct((TOKENS, D_MODEL), F32),
        scratch_shapes=[pltpu.VMEM((MLP_TM, MLP_TF), BF16), pltpu.VMEM((MLP_TM, MLP_TF), BF16)],
        compiler_params=pltpu.CompilerParams(
            dimension_semantics=("arbitrary", "arbitrary"),
            vmem_limit_bytes=VMEM_LIMIT),
        name="mlp",
    )(hn, w_up, w_down, h, norm_w)


def kernel(x, norm_mix_w, w_in, ret_norm_w, conv_w, conv_b, dt_bias, a_log, d_skip, ssd_norm_w,
           w_out, norm_mlp_w, w_up, w_down, norm_final_w):
    x2d = x.reshape(TOKENS, D_MODEL)
    w_in_t = w_in.T
    w_dt_t = jnp.pad(w_in_t[PROJ_WIDTH:, :], ((0, DT_PAD - SSD_HEADS), (0, 0)))
    proj, qkd, dt = _inproj(x2d, norm_mix_w.astype(F32)[None, :], w_in_t, w_dt_t,
                            conv_w.astype(F32), conv_b.astype(F32)[None, :])
    h, hn = _mixer(proj, qkd, dt, x, w_out, norm_mlp_w.astype(F32)[None, :],
                   (ret_norm_w, dt_bias, a_log, d_skip, ssd_norm_w))
    out = _mlp(hn, w_up, w_down, h, norm_final_w.astype(F32)[None, :])
    return out.reshape(BATCH, SEQ, D_MODEL)
```

```python
import functools

import numpy as np
import jax
import jax.numpy as jnp
from jax import lax
from jax.experimental import pallas as pl
from jax.experimental.pallas import tpu as pltpu

F32 = jnp.float32
BF16 = jnp.bfloat16

D_MODEL = 2048
BATCH = 4
SEQ = 2048
TOKENS = BATCH * SEQ
RET_HEADS = 4
RET_DIM = 256
RET_WIDTH = RET_HEADS * RET_DIM
ROPE_BASE = 10000.0
SSD_INNER = 1024
SSD_HEAD_DIM = 64
SSD_HEADS = 16
SSD_GROUPS = 2
SSD_STATE = 128
SSD_CONV = 4
SSD_CONV_DIM = SSD_INNER + 2 * SSD_GROUPS * SSD_STATE
CHUNK = 128
NUM_CHUNKS = SEQ // CHUNK
PROJ_WIDTH = 4 * RET_WIDTH + SSD_INNER + SSD_CONV_DIM
DT_PAD = 128
D_FF = 4 * D_MODEL
EPS = 1e-6

VMEM_LIMIT = 56 * 1024 * 1024

_NT = (((1,), (1,)), ((), ()))
_TN = (((0,), (0,)), ((), ()))


def _rms_scale(x):
    return lax.rsqrt(jnp.mean(x * x, axis=-1, keepdims=True) + EPS)


def _ret_gammas():
    return 1.0 - 2.0 ** (-5.0 - np.arange(RET_HEADS, dtype=np.float64))


def _ret_tables():
    lg = np.log(_ret_gammas())
    idx = np.arange(CHUNK, dtype=np.float64)
    rel = idx[:, None] - idx[None, :]
    causal = rel >= 0
    dintra = np.where(causal[None], np.exp(np.where(causal, rel, 0.0)[None] * lg[:, None, None]), 0.0)
    qdec = np.exp((idx + 1.0)[:, None] * lg[None, :])
    kdec = np.exp((CHUNK - 1.0 - idx)[:, None] * lg[None, :])
    qkdec = np.concatenate([np.repeat(qdec, RET_DIM, axis=1), np.repeat(kdec, RET_DIM, axis=1)], axis=1)
    return jnp.asarray(dintra, F32), jnp.asarray(qkdec, F32)


def _rope_tables():
    half = RET_DIM // 2
    inv_freq = ROPE_BASE ** (-np.arange(half, dtype=np.float64) / half)
    ang = np.arange(SEQ, dtype=np.float64)[:, None] * inv_freq[None, :]
    return jnp.asarray(np.cos(ang), F32), jnp.asarray(np.sin(ang), F32)


IN_TM = 2048
IN_TN = 512
IN_NT = PROJ_WIDTH // IN_TN
IN_MT = TOKENS // IN_TM
IN_TILES = IN_MT * IN_NT
IN_STEPS = IN_NT + IN_TILES + 1
IN_RB = 256
IN_NCH = 8
IN_CH = IN_TM // IN_NCH
IN_SEQ_TILES = SEQ // IN_TM
IN_N_ROPE = 2 * RET_WIDTH // IN_TN
IN_N_K = RET_WIDTH // IN_TN
IN_N_GATE = (3 * RET_WIDTH // IN_TN, (4 * RET_WIDTH + SSD_INNER) // IN_TN)


def _in_divmod(s):
    assert IN_NT == 13
    q = lax.shift_right_logical(s * 5042, 16)
    return q, s - q * IN_NT


def _in_tile(s, lag):
    return _in_divmod(jnp.clip(s - IN_NT - lag, 0, IN_TILES - 1))


def _inproj_kernel(x_ref, nw_ref, w_ref, wdt_ref, cos_ref, sin_ref, dec_ref, cw_ref, cb_ref,
                   proj_ref, qkd_ref, dt_ref, hn_ref, acc_ref, raw_ref, carry_ref):
    s = pl.program_id(0)
    t = s - IN_NT
    m, n = _in_tile(s, 0)
    pm, pn = _in_tile(s, 1)

    def normalise():
        row, col = _in_divmod(s)
        slot = row % 2
        r = pl.multiple_of(jnp.minimum(col, IN_NCH - 1) * IN_CH, IN_CH)
        x = x_ref[...]
        hn_ref[slot, pl.ds(r, IN_CH), :] = (x * _rms_scale(x) * nw_ref[...]).astype(BF16)

    def matmul_rows(rb, wbf):
        rows = pl.ds(rb * IN_RB, IN_RB)
        acc_ref[rows, :] = lax.dot_general(
            hn_ref[m % 2, rows, :], wbf, _NT, preferred_element_type=F32)

    def epilogue_rows(kind, rb):
        rows = pl.ds(rb * IN_RB, IN_RB)
        a = raw_ref[pl.ds(8 + rb * IN_RB, IN_RB), :]
        if kind == "rope":
            a = a * jnp.where(pn >= IN_N_K, RET_DIM ** -0.5, 1.0)
            cos, sin = cos_ref[rows, :], sin_ref[rows, :]
            half = RET_DIM // 2
            parts = []
            for hh in range(IN_TN // RET_DIM):
                x1 = a[:, hh * RET_DIM:hh * RET_DIM + half]
                x2 = a[:, hh * RET_DIM + half:(hh + 1) * RET_DIM]
                parts += [x1 * cos - x2 * sin, x1 * sin + x2 * cos]
            r = jnp.concatenate(parts, axis=-1)
            proj_ref[rows, :] = r.astype(BF16)
            rd = r.reshape(IN_RB // CHUNK, CHUNK, IN_TN) * dec_ref[...][None]
            qkd_ref[rows, :] = rd.reshape(IN_RB, IN_TN).astype(BF16)
        elif kind == "plain":
            proj_ref[rows, :] = a.astype(BF16)
        elif kind == "silu":
            proj_ref[rows, :] = jax.nn.silu(a).astype(BF16)
        else:
            conv = cb_ref[...] + cw_ref[SSD_CONV - 1:SSD_CONV, :] * a
            for tap in range(SSD_CONV - 1):
                off = 8 - (SSD_CONV - 1) + tap + rb * IN_RB
                conv = conv + cw_ref[tap:tap + 1, :] * raw_ref[pl.ds(off, IN_RB), :]
            proj_ref[rows, :] = jax.nn.silu(conv).astype(BF16)

    def step(do_matmul, kind):
        if kind is not None:
            raw_ref[8:8 + IN_TM, :] = acc_ref[...]
        if kind == "conv":
            j = pn - IN_N_GATE[1]
            raw_ref[0:8, :] = jnp.where(pm % IN_SEQ_TILES == 0, 0.0, carry_ref[j])
        if do_matmul:
            wbf = w_ref[...].astype(BF16)
        for rb in range(IN_TM // IN_RB):
            if do_matmul:
                matmul_rows(rb, wbf)
            if kind is not None:
                epilogue_rows(kind, rb)
            if do_matmul and rb == 0:
                normalise()
        if kind == "conv":
            carry_ref[j] = raw_ref[IN_TM:IN_TM + 8, :]

    @pl.when((t >= 0) & (t < IN_TILES) & (n == 0))
    def _():
        dt_ref[...] = lax.dot_general(hn_ref[m % 2], wdt_ref[...].astype(BF16), _NT,
                                      preferred_element_type=F32)

    @pl.when(s < IN_NT)
    def _():
        normalise()

    @pl.when(t == 0)
    def _():
        step(True, None)

    live = (t >= 1) & (t < IN_TILES)

    @pl.when(live & (pn < IN_N_ROPE))
    def _():
        step(True, "rope")

    @pl.when(live & (pn >= IN_N_ROPE) & (pn < IN_N_GATE[0]))
    def _():
        step(True, "plain")

    @pl.when(live & (pn >= IN_N_GATE[0]) & (pn < IN_N_GATE[1]))
    def _():
        step(True, "silu")

    @pl.when(live & (pn >= IN_N_GATE[1]))
    def _():
        step(True, "conv")

    @pl.when(t == IN_TILES)
    def _():
        step(False, "conv")


def _inproj(x2d, norm_w, w_main, w_dt, conv_w, conv_b):
    cos, sin = _rope_tables()
    _, qkdec = _ret_tables()
    n_conv = SSD_CONV_DIM // IN_TN
    rope_rows = lambda s: (_in_tile(s, 1)[0] % IN_SEQ_TILES, 0)
    conv_tile = lambda s: (0, jnp.maximum(_in_tile(s, 1)[1] - IN_N_GATE[1], 0))
    return pl.pallas_call(
        _inproj_kernel,
        grid=(IN_STEPS,),
        in_specs=[
            pl.BlockSpec((IN_CH, D_MODEL),
                         lambda s: (jnp.minimum(_in_divmod(s)[0], IN_MT - 1) * IN_NCH
                                    + jnp.minimum(_in_divmod(s)[1], IN_NCH - 1), 0)),
            pl.BlockSpec((1, D_MODEL), lambda s: (0, 0)),
            pl.BlockSpec((IN_TN, D_MODEL), lambda s: (_in_tile(s, 0)[1], 0)),
            pl.BlockSpec((DT_PAD, D_MODEL), lambda s: (0, 0)),
            pl.BlockSpec((IN_TM, RET_DIM // 2), rope_rows),
            pl.BlockSpec((IN_TM, RET_DIM // 2), rope_rows),
            pl.BlockSpec((CHUNK, IN_TN), lambda s: (0, jnp.minimum(_in_tile(s, 1)[1], IN_N_ROPE - 1))),
            pl.BlockSpec((SSD_CONV, IN_TN), conv_tile),
            pl.BlockSpec((1, IN_TN), conv_tile),
        ],
        out_specs=[
            pl.BlockSpec((IN_TM, IN_TN), lambda s: _in_tile(s, 1)),
            pl.BlockSpec((IN_TM, IN_TN),
                         lambda s: (_in_tile(s, 1)[0], jnp.minimum(_in_tile(s, 1)[1], IN_N_ROPE - 1))),
            pl.BlockSpec((IN_TM, DT_PAD), lambda s: (_in_tile(s, 0)[0], 0)),
        ],
        out_shape=[
            jax.ShapeDtypeStruct((TOKENS, PROJ_WIDTH), BF16),
            jax.ShapeDtypeStruct((TOKENS, 2 * RET_WIDTH), BF16),
            jax.ShapeDtypeStruct((TOKENS, DT_PAD), F32),
        ],
        scratch_shapes=[
            pltpu.VMEM((2, IN_TM, D_MODEL), BF16),
            pltpu.VMEM((IN_TM, IN_TN), F32),
            pltpu.VMEM((IN_TM + 8, IN_TN), F32),
            pltpu.VMEM((n_conv, 8, IN_TN), F32),
        ],
        compiler_params=pltpu.CompilerParams(
            dimension_semantics=("arbitrary",),
            vmem_limit_bytes=VMEM_LIMIT),
        name="inproj",
    )(x2d, norm_w, w_main, w_dt, cos, sin, qkdec, conv_w, conv_b)


def _cumsum_lanes(x):
    lane = lax.broadcasted_iota(jnp.int32, x.shape, 1)
    k = 1
    while k < x.shape[1]:
        x = x + jnp.where(lane >= k, pltpu.roll(x, k, axis=1), 0.0)
        k *= 2
    return x


MIX_NB = 2
MIX_N_BATCHED = 10
MIX_STAGES = 5


MIX_N_CONSTS = 4
OUT_NB = 4


def _mixer_kernel(*refs):
    ins = refs[:MIX_N_BATCHED]
    consts = refs[MIX_N_BATCHED:MIX_N_BATCHED + MIX_N_CONSTS]
    x_ref, wout_ref, nw_ref, h_ref, hn_ref, rstate, sstate, mix_ref, wbf_ref = (
        refs[MIX_N_BATCHED + MIX_N_CONSTS:])
    b, c = pl.program_id(0), pl.program_id(1)

    @pl.when((b == 0) & (c == 0))
    def _():
        wbf_ref[...] = wout_ref[...].astype(BF16)

    @pl.when(c == 0)
    def _():
        rstate[...] = jnp.zeros_like(rstate)
        sstate[...] = jnp.zeros_like(sstate)

    def step(do_mixer, do_outproj):
        chains = []
        if do_mixer:
            chains += [_mixer_stages(*[r.at[bi] for r in ins], *consts,
                                     mix_ref.at[bi], rstate.at[bi], sstate.at[bi])
                       for bi in range(MIX_NB)]
        if do_outproj:
            chains.append(_outproj_stages(mix_ref, wbf_ref, x_ref, nw_ref, h_ref, hn_ref))
        for _ in range(MIX_STAGES):
            for chain in chains:
                next(chain)

    @pl.when(c == 0)
    def _():
        step(True, False)

    @pl.when((c > 0) & (c < NUM_CHUNKS))
    def _():
        step(True, True)

    @pl.when(c == NUM_CHUNKS)
    def _():
        step(False, True)


def _outproj_stages(mix_ref, wbf_ref, x_ref, nw_ref, h_ref, hn_ref):
    assert OUT_NB + 1 == MIX_STAGES
    lhs = jnp.concatenate([mix_ref[bi] for bi in range(MIX_NB)], axis=0)
    nbw = D_MODEL // OUT_NB
    ssq = [0.0] * MIX_NB
    for nb in range(OUT_NB):
        cols = slice(nb * nbw, (nb + 1) * nbw)
        acc = jnp.dot(lhs, wbf_ref[:, cols], preferred_element_type=F32)
        for bi in range(MIX_NB):
            h = x_ref[bi, :, cols] + acc[bi * CHUNK:(bi + 1) * CHUNK, :]
            h_ref[bi, :, cols] = h
            ssq[bi] = ssq[bi] + jnp.sum(h * h, axis=-1, keepdims=True)
        yield
    for bi in range(MIX_NB):
        scale = lax.rsqrt(ssq[bi] * (1.0 / D_MODEL) + EPS)
        hn_ref[bi] = (h_ref[bi] * scale * nw_ref[...]).astype(BF16)
    yield


def _mixer_stages(q_ref, k_ref, v_ref, g_ref, z_ref, xs_ref, bc_ref, qd_ref, kd_ref, dt_ref,
                  dintra_ref, rnw_ref, hp_ref, snw_ref, out_ref, rstate, sstate):
    hpg = SSD_HEADS // SSD_GROUPS
    gw = hpg * SSD_HEAD_DIM
    cbase = SSD_GROUPS * SSD_STATE
    heads = [slice(h * RET_DIM, (h + 1) * RET_DIM) for h in range(RET_HEADS)]
    chunk_decay = _ret_gammas() ** CHUNK

    def rows_of(t, hh):
        return jnp.broadcast_to(t[hh:hh + 1, :], (SSD_HEAD_DIM, CHUNK))

    dt_t = jax.nn.softplus(dt_ref[...].T[0:SSD_HEADS, :] + hp_ref[0])
    acs_t = _cumsum_lanes(dt_t * (-jnp.exp(hp_ref[1])))
    a_last = acs_t[:, CHUNK - 1:CHUNK]
    w_t = jnp.exp(a_last - acs_t) * dt_t
    ea_t = jnp.exp(acs_t)
    cdec = jnp.broadcast_to(jnp.exp(a_last), (SSD_HEADS, CHUNK))
    acs_pad = jnp.concatenate([acs_t, jnp.zeros((CHUNK - SSD_HEADS, CHUNK), F32)], axis=0)
    acs_col = acs_pad.T
    dskip = hp_ref[2]
    yield

    scores, ycross, kv = [], [], []
    for h, sl in enumerate(heads):
        scores.append(lax.dot_general(q_ref[:, sl], k_ref[:, sl], _NT, preferred_element_type=F32))
        ycross.append(jnp.dot(qd_ref[:, sl], rstate[h].astype(BF16), preferred_element_type=F32))
        kv.append(lax.dot_general(kd_ref[:, sl], v_ref[:, sl], _TN, preferred_element_type=F32))
    xs_t = xs_ref[...].astype(F32).T
    bgs, cgs, cb_ts, yo_ts, s_prevs = [], [], [], [], []
    for g in range(SSD_GROUPS):
        bg = bc_ref[:, g * SSD_STATE:(g + 1) * SSD_STATE]
        cg = bc_ref[:, cbase + g * SSD_STATE:cbase + (g + 1) * SSD_STATE]
        s_prev = sstate[g * gw:(g + 1) * gw, :]
        bgs.append(bg)
        cgs.append(cg)
        s_prevs.append(s_prev)
        cb_ts.append(lax.dot_general(bg, cg, _NT, preferred_element_type=F32))
        yo_ts.append(lax.dot_general(s_prev.astype(BF16), cg, _NT,
                                     preferred_element_type=F32))
    yield

    ps = [(scores[h] * dintra_ref[h]).astype(BF16) for h in range(RET_HEADS)]
    for h in range(RET_HEADS):
        rstate[h] = float(chunk_decay[h]) * rstate[h] + kv[h]
    row = lax.broadcasted_iota(jnp.int32, (CHUNK, CHUNK), 0)
    col = lax.broadcasted_iota(jnp.int32, (CHUNK, CHUNK), 1)
    causal_t = col >= row
    m_ts, xdts, xws = [], [], []
    for hh in range(SSD_HEADS):
        xs_h = xs_t[hh * SSD_HEAD_DIM:(hh + 1) * SSD_HEAD_DIM, :]
        seg = (jnp.broadcast_to(acs_t[hh:hh + 1, :], (CHUNK, CHUNK))
               - jnp.broadcast_to(acs_col[:, hh:hh + 1], (CHUNK, CHUNK)))
        l_t = jnp.exp(jnp.where(causal_t, seg, -jnp.inf))
        m_ts.append((cb_ts[hh // hpg] * l_t).astype(BF16))
        xdts.append((xs_h * rows_of(dt_t, hh)).astype(BF16))
        xws.append((xs_h * rows_of(w_t, hh)).astype(BF16))
    yield

    ys = [jnp.dot(ps[h], v_ref[:, sl], preferred_element_type=F32) + ycross[h]
          for h, sl in enumerate(heads)]
    yds = [jnp.dot(xdts[hh], m_ts[hh], preferred_element_type=F32)
           for hh in range(SSD_HEADS)]
    for g in range(SSD_GROUPS):
        xw = jnp.concatenate(xws[g * hpg:(g + 1) * hpg], axis=0)
        cd = jnp.concatenate([rows_of(cdec, hh) for hh in range(g * hpg, (g + 1) * hpg)], axis=0)
        sstate[g * gw:(g + 1) * gw, :] = (
            cd * s_prevs[g] + jnp.dot(xw, bgs[g], preferred_element_type=F32))
    yield

    for h, sl in enumerate(heads):
        yn = ys[h] * _rms_scale(ys[h]) * rnw_ref[:, sl]
        out_ref[:, sl] = (yn * g_ref[:, sl].astype(F32)).astype(BF16)
    y_t_parts = []
    for hh in range(SSD_HEADS):
        e = hh % hpg
        xs_h = xs_t[hh * SSD_HEAD_DIM:(hh + 1) * SSD_HEAD_DIM, :]
        yo = yo_ts[hh // hpg][e * SSD_HEAD_DIM:(e + 1) * SSD_HEAD_DIM, :] * rows_of(ea_t, hh)
        y_t_parts.append(yds[hh] + yo + rows_of(dskip, hh) * xs_h)
    y = jnp.concatenate(y_t_parts, axis=0).T
    y = y * z_ref[...].astype(F32)
    for g in range(SSD_GROUPS):
        sl = slice(g * gw, (g + 1) * gw)
        yg = y[:, sl]
        out_ref[:, RET_WIDTH + g * gw:RET_WIDTH + (g + 1) * gw] = (
            yg * _rms_scale(yg) * snw_ref[:, sl]).astype(BF16)
    yield


def _mixer(proj, qkd, dt, x, w_out, norm_w, params):
    (ret_norm_w, dt_bias, a_log, d_skip, ssd_norm_w) = params
    dintra, _ = _ret_tables()
    proj = proj.reshape(BATCH, SEQ, PROJ_WIDTH)
    qkd = qkd.reshape(BATCH, SEQ, 2 * RET_WIDTH)
    dt = dt.reshape(BATCH, SEQ, DT_PAD)
    last = NUM_CHUNKS - 1

    def col_block(j, width):
        return pl.BlockSpec((MIX_NB, CHUNK, width), lambda b, c: (b, jnp.minimum(c, last), j))

    def lagged(width):
        return pl.BlockSpec((MIX_NB, CHUNK, width), lambda b, c: (b, jnp.maximum(c - 1, 0), 0))

    def full(shape, **kw):
        return pl.BlockSpec(shape, lambda b, c: (0,) * len(shape), **kw)

    head_params = jnp.broadcast_to(
        jnp.stack([dt_bias, a_log, d_skip]).astype(F32)[:, :, None], (3, SSD_HEADS, CHUNK))
    in_specs = [
        col_block(0, RET_WIDTH), col_block(1, RET_WIDTH), col_block(2, RET_WIDTH),
        col_block(3, RET_WIDTH), col_block(4, SSD_INNER), col_block(5, SSD_INNER),
        col_block(12, 2 * SSD_GROUPS * SSD_STATE),
        col_block(0, RET_WIDTH), col_block(1, RET_WIDTH),
        col_block(0, DT_PAD),
        full((RET_HEADS, CHUNK, CHUNK)),
        full((1, RET_WIDTH)),
        full((3, SSD_HEADS, CHUNK)),
        full((1, SSD_INNER)),
        lagged(D_MODEL),
        full((D_MODEL, D_MODEL), pipeline_mode=pl.Buffered(1)),
        full((1, D_MODEL)),
    ]
    assert len(in_specs) == MIX_N_BATCHED + MIX_N_CONSTS + 3
    h, hn = pl.pallas_call(
        _mixer_kernel,
        grid=(BATCH // MIX_NB, NUM_CHUNKS + 1),
        in_specs=in_specs,
        out_specs=[lagged(D_MODEL), lagged(D_MODEL)],
        out_shape=[
            jax.ShapeDtypeStruct((BATCH, SEQ, D_MODEL), F32),
            jax.ShapeDtypeStruct((BATCH, SEQ, D_MODEL), BF16),
        ],
        scratch_shapes=[
            pltpu.VMEM((MIX_NB, RET_HEADS, RET_DIM, RET_DIM), F32),
            pltpu.VMEM((MIX_NB, SSD_INNER, SSD_STATE), F32),
            pltpu.VMEM((MIX_NB, CHUNK, D_MODEL), BF16),
            pltpu.VMEM((D_MODEL, D_MODEL), BF16),
        ],
        compiler_params=pltpu.CompilerParams(
            dimension_semantics=("arbitrary", "arbitrary"),
            vmem_limit_bytes=VMEM_LIMIT),
        name="mixer",
    )(proj, proj, proj, proj, proj, proj, proj, qkd, qkd, dt, dintra,
      ret_norm_w.astype(F32)[None, :],
      head_params, ssd_norm_w.astype(F32)[None, :],
      x, w_out, norm_w)
    return h.reshape(TOKENS, D_MODEL), hn.reshape(TOKENS, D_MODEL)


MLP_TM = 1024
MLP_TF = 512
MLP_NF = D_FF // MLP_TF
MLP_HROWS = MLP_TM // MLP_NF
MLP_ROWS = 128


def _mlp_kernel(hn_ref, wup_ref, wdn_ref, h_ref, nw_ref, out_ref):
    f = pl.program_id(1)

    @pl.when(f == 0)
    def _():
        out_ref[...] = jnp.zeros_like(out_ref)

    u = jnp.maximum(jnp.dot(hn_ref[...], wup_ref[...].astype(BF16),
                            preferred_element_type=F32), 0.0)
    out_ref[...] += jnp.dot((u * u).astype(BF16), wdn_ref[...].astype(BF16),
                            preferred_element_type=F32)
    r = pl.multiple_of(f * MLP_HROWS, MLP_HROWS)
    out_ref[pl.ds(r, MLP_HROWS), :] += h_ref[...]

    @pl.when(f == MLP_NF - 1)
    def _():
        def body(i, carry):
            r = pl.multiple_of(i * MLP_ROWS, MLP_ROWS)
            h = out_ref[pl.ds(r, MLP_ROWS), :]
            out_ref[pl.ds(r, MLP_ROWS), :] = h * _rms_scale(h) * nw_ref[...]
            return carry
        lax.fori_loop(0, MLP_TM // MLP_ROWS, body, 0)


def _mlp(hn, w_up, w_down, h, norm_w):
    return pl.pallas_call(
        _mlp_kernel,
        grid=(TOKENS // MLP_TM, MLP_NF),
        in_specs=[
            pl.BlockSpec((MLP_TM, D_MODEL), lambda m, f: (m, 0)),
            pl.BlockSpec((D_MODEL, MLP_TF), lambda m, f: (0, f)),
            pl.BlockSpec((MLP_TF, D_MODEL), lambda m, f: (f, 0)),
            pl.BlockSpec((MLP_HROWS, D_MODEL), lambda m, f: (m * MLP_NF + f, 0)),
            pl.BlockSpec((1, D_MODEL), lambda m, f: (0, 0)),
        ],
        out_specs=pl.BlockSpec((MLP_TM, D_MODEL), lambda m, f: (m, 0)),
        out_shape=jax.ShapeDtypeStruct((TOKENS, D_MODEL), F32),
        compiler_params=pltpu.CompilerParams(
            dimension_semantics=("arbitrary", "arbitrary"),
            vmem_limit_bytes=VMEM_LIMIT),
        name="mlp",
    )(hn, w_up, w_down, h, norm_w)


def kernel(x, norm_mix_w, w_in, ret_norm_w, conv_w, conv_b, dt_bias, a_log, d_skip, ssd_norm_w,
           w_out, norm_mlp_w, w_up, w_down, norm_final_w):
    x2d = x.reshape(TOKENS, D_MODEL)
    w_in_t = w_in.T
    w_dt_t = jnp.pad(w_in_t[PROJ_WIDTH:, :], ((0, DT_PAD - SSD_HEADS), (0, 0)))
    proj, qkd, dt = _inproj(x2d, norm_mix_w.astype(F32)[None, :], w_in_t, w_dt_t,
                            conv_w.astype(F32), conv_b.astype(F32)[None, :])
    h, hn = _mixer(proj, qkd, dt, x, w_out, norm_mlp_w.astype(F32)[None, :],
                   (ret_norm_w, dt_bias, a_log, d_skip, ssd_norm_w))
    out = _mlp(hn, w_up, w_down, h, norm_final_w.astype(F32)[None, :])
    return out.reshape(BATCH, SEQ, D_MODEL)
```

```python
import functools

import numpy as np
import jax
import jax.numpy as jnp
from jax import lax
from jax.experimental import pallas as pl
from jax.experimental.pallas import tpu as pltpu

F32 = jnp.float32
BF16 = jnp.bfloat16

D_MODEL = 2048
BATCH = 4
SEQ = 2048
TOKENS = BATCH * SEQ
RET_HEADS = 4
RET_DIM = 256
RET_WIDTH = RET_HEADS * RET_DIM
ROPE_BASE = 10000.0
SSD_INNER = 1024
SSD_HEAD_DIM = 64
SSD_HEADS = 16
SSD_GROUPS = 2
SSD_STATE = 128
SSD_CONV = 4
SSD_CONV_DIM = SSD_INNER + 2 * SSD_GROUPS * SSD_STATE
CHUNK = 128
NUM_CHUNKS = SEQ // CHUNK
PROJ_WIDTH = 4 * RET_WIDTH + SSD_INNER + SSD_CONV_DIM
DT_PAD = 128
D_FF = 4 * D_MODEL
EPS = 1e-6

VMEM_LIMIT = 56 * 1024 * 1024

_NT = (((1,), (1,)), ((), ()))
_TN = (((0,), (0,)), ((), ()))


def _rms_scale(x):
    return lax.rsqrt(jnp.mean(x * x, axis=-1, keepdims=True) + EPS)


def _ret_gammas():
    return 1.0 - 2.0 ** (-5.0 - np.arange(RET_HEADS, dtype=np.float64))


def _ret_tables():
    lg = np.log(_ret_gammas())
    idx = np.arange(CHUNK, dtype=np.float64)
    rel = idx[:, None] - idx[None, :]
    causal = rel >= 0
    dintra = np.where(causal[None], np.exp(np.where(causal, rel, 0.0)[None] * lg[:, None, None]), 0.0)
    qdec = np.exp((idx + 1.0)[:, None] * lg[None, :])
    kdec = np.exp((CHUNK - 1.0 - idx)[:, None] * lg[None, :])
    qkdec = np.concatenate([np.repeat(qdec, RET_DIM, axis=1), np.repeat(kdec, RET_DIM, axis=1)], axis=1)
    return jnp.asarray(dintra, F32), jnp.asarray(qkdec, F32)


def _rope_tables():
    half = RET_DIM // 2
    inv_freq = ROPE_BASE ** (-np.arange(half, dtype=np.float64) / half)
    ang = np.arange(SEQ, dtype=np.float64)[:, None] * inv_freq[None, :]
    return jnp.asarray(np.cos(ang), F32), jnp.asarray(np.sin(ang), F32)


IN_TM = 2048
IN_TN = 512
IN_NT = PROJ_WIDTH // IN_TN
IN_MT = TOKENS // IN_TM
IN_TILES = IN_MT * IN_NT
IN_STEPS = IN_NT + IN_TILES + 1
IN_RB = 256
IN_NCH = 8
IN_CH = IN_TM // IN_NCH
IN_SEQ_TILES = SEQ // IN_TM
IN_N_ROPE = 2 * RET_WIDTH // IN_TN
IN_N_K = RET_WIDTH // IN_TN
IN_N_GATE = (3 * RET_WIDTH // IN_TN, (4 * RET_WIDTH + SSD_INNER) // IN_TN)


def _in_divmod(s):
    assert IN_NT == 13
    q = lax.shift_right_logical(s * 5042, 16)
    return q, s - q * IN_NT


def _in_tile(s, lag):
    return _in_divmod(jnp.clip(s - IN_NT - lag, 0, IN_TILES - 1))


def _inproj_kernel(x_ref, nw_ref, w_ref, wdt_ref, cos_ref, sin_ref, dec_ref, cw_ref, cb_ref,
                   proj_ref, qkd_ref, dt_ref, hn_ref, acc_ref, raw_ref, carry_ref):
    s = pl.program_id(0)
    t = s - IN_NT
    m, n = _in_tile(s, 0)
    pm, pn = _in_tile(s, 1)

    def normalise():
        row, col = _in_divmod(s)
        slot = row % 2
        r = pl.multiple_of(jnp.minimum(col, IN_NCH - 1) * IN_CH, IN_CH)
        x = x_ref[...]
        hn_ref[slot, pl.ds(r, IN_CH), :] = (x * _rms_scale(x) * nw_ref[...]).astype(BF16)

    def matmul_rows(rb, wbf):
        rows = pl.ds(rb * IN_RB, IN_RB)
        acc_ref[rows, :] = lax.dot_general(
            hn_ref[m % 2, rows, :], wbf, _NT, preferred_element_type=F32)

    def epilogue_rows(kind, rb):
        rows = pl.ds(rb * IN_RB, IN_RB)
        a = raw_ref[pl.ds(8 + rb * IN_RB, IN_RB), :]
        if kind == "rope":
            a = a * jnp.where(pn >= IN_N_K, RET_DIM ** -0.5, 1.0)
            cos, sin = cos_ref[rows, :], sin_ref[rows, :]
            half = RET_DIM // 2
            parts = []
            for hh in range(IN_TN // RET_DIM):
                x1 = a[:, hh * RET_DIM:hh * RET_DIM + half]
                x2 = a[:, hh * RET_DIM + half:(hh + 1) * RET_DIM]
                parts += [x1 * cos - x2 * sin, x1 * sin + x2 * cos]
            r = jnp.concatenate(parts, axis=-1)
            proj_ref[rows, :] = r.astype(BF16)
            rd = r.reshape(IN_RB // CHUNK, CHUNK, IN_TN) * dec_ref[...][None]
            qkd_ref[rows, :] = rd.reshape(IN_RB, IN_TN).astype(BF16)
        elif kind == "plain":
            proj_ref[rows, :] = a.astype(BF16)
        elif kind == "silu":
            proj_ref[rows, :] = jax.nn.silu(a).astype(BF16)
        else:
            conv = cb_ref[...] + cw_ref[SSD_CONV - 1:SSD_CONV, :] * a
            for tap in range(SSD_CONV - 1):
                off = 8 - (SSD_CONV - 1) + tap + rb * IN_RB
                conv = conv + cw_ref[tap:tap + 1, :] * raw_ref[pl.ds(off, IN_RB), :]
            proj_ref[rows, :] = jax.nn.silu(conv).astype(BF16)

    def step(do_matmul, kind):
        if kind is not None:
            raw_ref[8:8 + IN_TM, :] = acc_ref[...]
        if kind == "conv":
            j = pn - IN_N_GATE[1]
            raw_ref[0:8, :] = jnp.where(pm % IN_SEQ_TILES == 0, 0.0, carry_ref[j])
        if do_matmul:
            wbf = w_ref[...].astype(BF16)
        for rb in range(IN_TM // IN_RB):
            if do_matmul:
                matmul_rows(rb, wbf)
            if kind is not None:
                epilogue_rows(kind, rb)
            if do_matmul and rb == 0:
                normalise()
        if kind == "conv":
            carry_ref[j] = raw_ref[IN_TM:IN_TM + 8, :]

    @pl.when((t >= 0) & (t < IN_TILES) & (n == 0))
    def _():
        dt_ref[...] = lax.dot_general(hn_ref[m % 2], wdt_ref[...].astype(BF16), _NT,
                                      preferred_element_type=F32)

    @pl.when(s < IN_NT)
    def _():
        normalise()

    @pl.when(t == 0)
    def _():
        step(True, None)

    live = (t >= 1) & (t < IN_TILES)

    @pl.when(live & (pn < IN_N_ROPE))
    def _():
        step(True, "rope")

    @pl.when(live & (pn >= IN_N_ROPE) & (pn < IN_N_GATE[0]))
    def _():
        step(True, "plain")

    @pl.when(live & (pn >= IN_N_GATE[0]) & (pn < IN_N_GATE[1]))
    def _():
        step(True, "silu")

    @pl.when(live & (pn >= IN_N_GATE[1]))
    def _():
        step(True, "conv")

    @pl.when(t == IN_TILES)
    def _():
        step(False, "conv")


def _inproj(x2d, norm_w, w_main, w_dt, conv_w, conv_b):
    cos, sin = _rope_tables()
    _, qkdec = _ret_tables()
    n_conv = SSD_CONV_DIM // IN_TN
    rope_rows = lambda s: (_in_tile(s, 1)[0] % IN_SEQ_TILES, 0)
    conv_tile = lambda s: (0, jnp.maximum(_in_tile(s, 1)[1] - IN_N_GATE[1], 0))
    return pl.pallas_call(
        _inproj_kernel,
        grid=(IN_STEPS,),
        in_specs=[
            pl.BlockSpec((IN_CH, D_MODEL),
                         lambda s: (jnp.minimum(_in_divmod(s)[0], IN_MT - 1) * IN_NCH
                                    + jnp.minimum(_in_divmod(s)[1], IN_NCH - 1), 0)),
            pl.BlockSpec((1, D_MODEL), lambda s: (0, 0)),
            pl.BlockSpec((IN_TN, D_MODEL), lambda s: (_in_tile(s, 0)[1], 0)),
            pl.BlockSpec((DT_PAD, D_MODEL), lambda s: (0, 0)),
            pl.BlockSpec((IN_TM, RET_DIM // 2), rope_rows),
            pl.BlockSpec((IN_TM, RET_DIM // 2), rope_rows),
            pl.BlockSpec((CHUNK, IN_TN), lambda s: (0, jnp.minimum(_in_tile(s, 1)[1], IN_N_ROPE - 1))),
            pl.BlockSpec((SSD_CONV, IN_TN), conv_tile),
            pl.BlockSpec((1, IN_TN), conv_tile),
        ],
        out_specs=[
            pl.BlockSpec((IN_TM, IN_TN), lambda s: _in_tile(s, 1)),
            pl.BlockSpec((IN_TM, IN_TN),
                         lambda s: (_in_tile(s, 1)[0], jnp.minimum(_in_tile(s, 1)[1], IN_N_ROPE - 1))),
            pl.BlockSpec((IN_TM, DT_PAD), lambda s: (_in_tile(s, 0)[0], 0)),
        ],
        out_shape=[
            jax.ShapeDtypeStruct((TOKENS, PROJ_WIDTH), BF16),
            jax.ShapeDtypeStruct((TOKENS, 2 * RET_WIDTH), BF16),
            jax.ShapeDtypeStruct((TOKENS, DT_PAD), F32),
        ],
        scratch_shapes=[
            pltpu.VMEM((2, IN_TM, D_MODEL), BF16),
            pltpu.VMEM((IN_TM, IN_TN), F32),
            pltpu.VMEM((IN_TM + 8, IN_TN), F32),
            pltpu.VMEM((n_conv, 8, IN_TN), F32),
        ],
        compiler_params=pltpu.CompilerParams(
            dimension_semantics=("arbitrary",),
            vmem_limit_bytes=VMEM_LIMIT),
        name="inproj",
    )(x2d, norm_w, w_main, w_dt, cos, sin, qkdec, conv_w, conv_b)


def _cumsum_lanes(x):
    lane = lax.broadcasted_iota(jnp.int32, x.shape, 1)
    k = 1
    while k < x.shape[1]:
        x = x + jnp.where(lane >= k, pltpu.roll(x, k, axis=1), 0.0)
        k *= 2
    return x


MIX_NB = 2
MIX_N_BATCHED = 10
MIX_STAGES = 5


MIX_N_CONSTS = 4
OUT_NB = 8
OUT_BLOCKS_PER_STAGE = (2, 1, 2, 1, 2)


def _mixer_kernel(*refs):
    ins = refs[:MIX_N_BATCHED]
    consts = refs[MIX_N_BATCHED:MIX_N_BATCHED + MIX_N_CONSTS]
    x_ref, wout_ref, nw_ref, h_ref, hn_ref, rstate, sstate, mix_ref, wbf_ref = (
        refs[MIX_N_BATCHED + MIX_N_CONSTS:])
    b, c = pl.program_id(0), pl.program_id(1)

    @pl.when((b == 0) & (c == 0))
    def _():
        wbf_ref[...] = wout_ref[...].astype(BF16)

    @pl.when(c == 0)
    def _():
        rstate[...] = jnp.zeros_like(rstate)
        sstate[...] = jnp.zeros_like(sstate)

    def step(do_mixer, do_outproj):
        chains = []
        if do_mixer:
            chains += [_mixer_stages(*[r.at[bi] for r in ins], *consts,
                                     mix_ref.at[bi], rstate.at[bi], sstate.at[bi])
                       for bi in range(MIX_NB)]
        if do_outproj:
            chains.append(_outproj_stages(mix_ref, wbf_ref, x_ref, nw_ref, h_ref, hn_ref))
        for _ in range(MIX_STAGES):
            for chain in chains:
                next(chain)

    @pl.when(c == 0)
    def _():
        step(True, False)

    @pl.when((c > 0) & (c < NUM_CHUNKS))
    def _():
        step(True, True)

    @pl.when(c == NUM_CHUNKS)
    def _():
        step(False, True)


def _outproj_stages(mix_ref, wbf_ref, x_ref, nw_ref, h_ref, hn_ref):
    assert sum(OUT_BLOCKS_PER_STAGE) == OUT_NB and len(OUT_BLOCKS_PER_STAGE) == MIX_STAGES
    lhs = jnp.concatenate([mix_ref[bi] for bi in range(MIX_NB)], axis=0)
    nbw = D_MODEL // OUT_NB
    ssq = [0.0] * MIX_NB
    nb = 0
    for stage, blocks in enumerate(OUT_BLOCKS_PER_STAGE):
        for _ in range(blocks):
            cols = slice(nb * nbw, (nb + 1) * nbw)
            acc = jnp.dot(lhs, wbf_ref[:, cols], preferred_element_type=F32)
            for bi in range(MIX_NB):
                h = x_ref[bi, :, cols] + acc[bi * CHUNK:(bi + 1) * CHUNK, :]
                h_ref[bi, :, cols] = h
                ssq[bi] = ssq[bi] + jnp.sum(h * h, axis=-1, keepdims=True)
            nb += 1
        if stage == MIX_STAGES - 1:
            for bi in range(MIX_NB):
                scale = lax.rsqrt(ssq[bi] * (1.0 / D_MODEL) + EPS)
                hn_ref[bi] = (h_ref[bi] * scale * nw_ref[...]).astype(BF16)
        yield


def _mixer_stages(q_ref, k_ref, v_ref, g_ref, z_ref, xs_ref, bc_ref, qd_ref, kd_ref, dt_ref,
                  dintra_ref, rnw_ref, hp_ref, snw_ref, out_ref, rstate, sstate):
    hpg = SSD_HEADS // SSD_GROUPS
    gw = hpg * SSD_HEAD_DIM
    cbase = SSD_GROUPS * SSD_STATE
    heads = [slice(h * RET_DIM, (h + 1) * RET_DIM) for h in range(RET_HEADS)]
    chunk_decay = _ret_gammas() ** CHUNK

    def rows_of(t, hh):
        return jnp.broadcast_to(t[hh:hh + 1, :], (SSD_HEAD_DIM, CHUNK))

    dt_t = jax.nn.softplus(dt_ref[...].T[0:SSD_HEADS, :] + hp_ref[0])
    acs_t = _cumsum_lanes(dt_t * (-jnp.exp(hp_ref[1])))
    a_last = acs_t[:, CHUNK - 1:CHUNK]
    w_t = jnp.exp(a_last - acs_t) * dt_t
    ea_t = jnp.exp(acs_t)
    cdec = jnp.broadcast_to(jnp.exp(a_last), (SSD_HEADS, CHUNK))
    acs_pad = jnp.concatenate([acs_t, jnp.zeros((CHUNK - SSD_HEADS, CHUNK), F32)], axis=0)
    acs_col = acs_pad.T
    dskip = hp_ref[2]
    yield

    scores, ycross, kv = [], [], []
    for h, sl in enumerate(heads):
        scores.append(lax.dot_general(q_ref[:, sl], k_ref[:, sl], _NT, preferred_element_type=F32))
        ycross.append(jnp.dot(qd_ref[:, sl], rstate[h].astype(BF16), preferred_element_type=F32))
        kv.append(lax.dot_general(kd_ref[:, sl], v_ref[:, sl], _TN, preferred_element_type=F32))
    xs_t = xs_ref[...].astype(F32).T
    bgs = [bc_ref[:, g * SSD_STATE:(g + 1) * SSD_STATE] for g in range(SSD_GROUPS)]
    cgs = [bc_ref[:, cbase + g * SSD_STATE:cbase + (g + 1) * SSD_STATE] for g in range(SSD_GROUPS)]
    zero_blk = jnp.zeros((CHUNK, SSD_STATE), BF16)

    def block_diag(a, b):
        return jnp.concatenate([jnp.concatenate([a, zero_blk], axis=1),
                                jnp.concatenate([zero_blk, b], axis=1)], axis=0)

    bd_b, bd_c = block_diag(*bgs), block_diag(*cgs)
    cbs = [lax.dot_general(cgs[g], bgs[g], _NT, preferred_element_type=F32)
           for g in range(SSD_GROUPS)]
    s_prev = sstate[...]
    yo_all = lax.dot_general(s_prev.astype(BF16), bd_c, _NT,
                             preferred_element_type=F32)
    yield

    ps = [(scores[h] * dintra_ref[h]).astype(BF16) for h in range(RET_HEADS)]
    for h in range(RET_HEADS):
        rstate[h] = float(chunk_decay[h]) * rstate[h] + kv[h]
    row = lax.broadcasted_iota(jnp.int32, (CHUNK, CHUNK), 0)
    col = lax.broadcasted_iota(jnp.int32, (CHUNK, CHUNK), 1)
    causal = row >= col
    ms, xws = [], []
    for hh in range(SSD_HEADS):
        xs_h = xs_t[hh * SSD_HEAD_DIM:(hh + 1) * SSD_HEAD_DIM, :]
        seg = (jnp.broadcast_to(acs_col[:, hh:hh + 1], (CHUNK, CHUNK))
               - jnp.broadcast_to(acs_t[hh:hh + 1, :], (CHUNK, CHUNK)))
        decay = jnp.exp(jnp.where(causal, seg, -jnp.inf))
        dt_s = jnp.broadcast_to(dt_t[hh:hh + 1, :], (CHUNK, CHUNK))
        ms.append((cbs[hh // hpg] * decay * dt_s).astype(BF16))
        xws.append((xs_h * rows_of(w_t, hh)).astype(BF16))
    yield

    ys = [jnp.dot(ps[h], v_ref[:, sl], preferred_element_type=F32) + ycross[h]
          for h, sl in enumerate(heads)]
    lane = lax.broadcasted_iota(jnp.int32, (CHUNK, 2 * SSD_HEAD_DIM), 1)
    yd_pairs = []
    for j in range(SSD_HEADS // 2):
        m_pair = jnp.concatenate([ms[2 * j], ms[2 * j + 1]], axis=1)
        x_pair = xs_ref[:, 2 * j * SSD_HEAD_DIM:(2 * j + 2) * SSD_HEAD_DIM]
        zero = jnp.zeros_like(x_pair)
        x_bd = jnp.concatenate([jnp.where(lane < SSD_HEAD_DIM, x_pair, zero),
                                jnp.where(lane >= SSD_HEAD_DIM, x_pair, zero)], axis=0)
        yd_pairs.append(jnp.dot(m_pair, x_bd, preferred_element_type=F32))
    xw = jnp.concatenate([jnp.concatenate(xws[g * hpg:(g + 1) * hpg], axis=0)
                          for g in range(SSD_GROUPS)], axis=1)
    cd = jnp.concatenate(
        [jnp.concatenate([rows_of(cdec, hh) for hh in range(g * hpg, (g + 1) * hpg)], axis=0)
         for g in range(SSD_GROUPS)], axis=1)
    sstate[...] = cd * s_prev + jnp.dot(xw, bd_b, preferred_element_type=F32)
    yield

    for h, sl in enumerate(heads):
        yn = ys[h] * _rms_scale(ys[h]) * rnw_ref[:, sl]
        out_ref[:, sl] = (yn * g_ref[:, sl].astype(F32)).astype(BF16)
    rest_t = []
    for hh in range(SSD_HEADS):
        g, e = divmod(hh, hpg)
        xs_h = xs_t[hh * SSD_HEAD_DIM:(hh + 1) * SSD_HEAD_DIM, :]
        yo = yo_all[e * SSD_HEAD_DIM:(e + 1) * SSD_HEAD_DIM, g * SSD_STATE:(g + 1) * SSD_STATE]
        rest_t.append(yo * rows_of(ea_t, hh) + rows_of(dskip, hh) * xs_h)
    y = jnp.concatenate(yd_pairs, axis=1) + jnp.concatenate(rest_t, axis=0).T
    y = y * z_ref[...].astype(F32)
    for g in range(SSD_GROUPS):
        sl = slice(g * gw, (g + 1) * gw)
        yg = y[:, sl]
        out_ref[:, RET_WIDTH + g * gw:RET_WIDTH + (g + 1) * gw] = (
            yg * _rms_scale(yg) * snw_ref[:, sl]).astype(BF16)
    yield


def _mixer(proj, qkd, dt, x, w_out, norm_w, params):
    (ret_norm_w, dt_bias, a_log, d_skip, ssd_norm_w) = params
    dintra, _ = _ret_tables()
    proj = proj.reshape(BATCH, SEQ, PROJ_WIDTH)
    qkd = qkd.reshape(BATCH, SEQ, 2 * RET_WIDTH)
    dt = dt.reshape(BATCH, SEQ, DT_PAD)
    last = NUM_CHUNKS - 1

    def col_block(j, width):
        return pl.BlockSpec((MIX_NB, CHUNK, width), lambda b, c: (b, jnp.minimum(c, last), j))

    def lagged(width):
        return pl.BlockSpec((MIX_NB, CHUNK, width), lambda b, c: (b, jnp.maximum(c - 1, 0), 0))

    def full(shape, **kw):
        return pl.BlockSpec(shape, lambda b, c: (0,) * len(shape), **kw)

    head_params = jnp.broadcast_to(
        jnp.stack([dt_bias, a_log, d_skip]).astype(F32)[:, :, None], (3, SSD_HEADS, CHUNK))
    in_specs = [
        col_block(0, RET_WIDTH), col_block(1, RET_WIDTH), col_block(2, RET_WIDTH),
        col_block(3, RET_WIDTH), col_block(4, SSD_INNER), col_block(5, SSD_INNER),
        col_block(12, 2 * SSD_GROUPS * SSD_STATE),
        col_block(0, RET_WIDTH), col_block(1, RET_WIDTH),
        col_block(0, DT_PAD),
        full((RET_HEADS, CHUNK, CHUNK)),
        full((1, RET_WIDTH)),
        full((3, SSD_HEADS, CHUNK)),
        full((1, SSD_INNER)),
        lagged(D_MODEL),
        full((D_MODEL, D_MODEL), pipeline_mode=pl.Buffered(1)),
        full((1, D_MODEL)),
    ]
    assert len(in_specs) == MIX_N_BATCHED + MIX_N_CONSTS + 3
    h, hn = pl.pallas_call(
        _mixer_kernel,
        grid=(BATCH // MIX_NB, NUM_CHUNKS + 1),
        in_specs=in_specs,
        out_specs=[lagged(D_MODEL), lagged(D_MODEL)],
        out_shape=[
            jax.ShapeDtypeStruct((BATCH, SEQ, D_MODEL), F32),
            jax.ShapeDtypeStruct((BATCH, SEQ, D_MODEL), BF16),
        ],
        scratch_shapes=[
            pltpu.VMEM((MIX_NB, RET_HEADS, RET_DIM, RET_DIM), F32),
            pltpu.VMEM((MIX_NB, SSD_INNER // SSD_GROUPS, SSD_GROUPS * SSD_STATE), F32),
            pltpu.VMEM((MIX_NB, CHUNK, D_MODEL), BF16),
            pltpu.VMEM((D_MODEL, D_MODEL), BF16),
        ],
        compiler_params=pltpu.CompilerParams(
            dimension_semantics=("arbitrary", "arbitrary"),
            vmem_limit_bytes=VMEM_LIMIT),
        name="mixer",
    )(proj, proj, proj, proj, proj, proj, proj, qkd, qkd, dt, dintra,
      ret_norm_w.astype(F32)[None, :],
      head_params, ssd_norm_w.astype(F32)[None, :],
      x, w_out, norm_w)
    return h.reshape(TOKENS, D_MODEL), hn.reshape(TOKENS, D_MODEL)


MLP_TM = 1024
MLP_TF = 512
MLP_NF = D_FF // MLP_TF
MLP_HROWS = MLP_TM // MLP_NF
MLP_ROWS = 128


def _mlp_kernel(hn_ref, wup_ref, wdn_ref, h_ref, nw_ref, out_ref):
    f = pl.program_id(1)

    @pl.when(f == 0)
    def _():
        out_ref[...] = jnp.zeros_like(out_ref)

    u = jnp.maximum(jnp.dot(hn_ref[...], wup_ref[...].astype(BF16),
                            preferred_element_type=F32), 0.0)
    out_ref[...] += jnp.dot((u * u).astype(BF16), wdn_ref[...].astype(BF16),
                            preferred_element_type=F32)
    r = pl.multiple_of(f * MLP_HROWS, MLP_HROWS)
    out_ref[pl.ds(r, MLP_HROWS), :] += h_ref[...]

    @pl.when(f == MLP_NF - 1)
    def _():
        def body(i, carry):
            r = pl.multiple_of(i * MLP_ROWS, MLP_ROWS)
            h = out_ref[pl.ds(r, MLP_ROWS), :]
            out_ref[pl.ds(r, MLP_ROWS), :] = h * _rms_scale(h) * nw_ref[...]
            return carry
        lax.fori_loop(0, MLP_TM // MLP_ROWS, body, 0)


def _mlp(hn, w_up, w_down, h, norm_w):
    return pl.pallas_call(
        _mlp_kernel,
        grid=(TOKENS // MLP_TM, MLP_NF),
        in_specs=[
            pl.BlockSpec((MLP_TM, D_MODEL), lambda m, f: (m, 0)),
            pl.BlockSpec((D_MODEL, MLP_TF), lambda m, f: (0, f)),
            pl.BlockSpec((MLP_TF, D_MODEL), lambda m, f: (f, 0)),
            pl.BlockSpec((MLP_HROWS, D_MODEL), lambda m, f: (m * MLP_NF + f, 0)),
            pl.BlockSpec((1, D_MODEL), lambda m, f: (0, 0)),
        ],
        out_specs=pl.BlockSpec((MLP_TM, D_MODEL), lambda m, f: (m, 0)),
        out_shape=jax.ShapeDtypeStruct((TOKENS, D_MODEL), F32),
        compiler_params=pltpu.CompilerParams(
            dimension_semantics=("arbitrary", "arbitrary"),
            vmem_limit_bytes=VMEM_LIMIT),
        name="mlp",
    )(hn, w_up, w_down, h, norm_w)


def kernel(x, norm_mix_w, w_in, ret_norm_w, conv_w, conv_b, dt_bias, a_log, d_skip, ssd_norm_w,
           w_out, norm_mlp_w, w_up, w_down, norm_final_w):
    x2d = x.reshape(TOKENS, D_MODEL)
    w_in_t = w_in.T
    w_dt_t = jnp.pad(w_in_t[PROJ_WIDTH:, :], ((0, DT_PAD - SSD_HEADS), (0, 0)))
    proj, qkd, dt = _inproj(x2d, norm_mix_w.astype(F32)[None, :], w_in_t, w_dt_t,
                            conv_w.astype(F32), conv_b.astype(F32)[None, :])
    h, hn = _mixer(proj, qkd, dt, x, w_out, norm_mlp_w.astype(F32)[None, :],
                   (ret_norm_w, dt_bias, a_log, d_skip, ssd_norm_w))
    out = _mlp(hn, w_up, w_down, h, norm_final_w.astype(F32)[None, :])
    return out.reshape(BATCH, SEQ, D_MODEL)
```

```python
import numpy as np
import jax
import jax.numpy as jnp
from jax import lax
from jax.experimental import pallas as pl
from jax.experimental.pallas import tpu as pltpu

F32 = jnp.float32
BF16 = jnp.bfloat16

D_MODEL = 2048
BATCH = 4
SEQ = 2048
TOKENS = BATCH * SEQ
RET_HEADS = 4
RET_DIM = 256
RET_WIDTH = RET_HEADS * RET_DIM
ROPE_BASE = 10000.0
SSD_INNER = 1024
SSD_HEAD_DIM = 64
SSD_HEADS = 16
SSD_GROUPS = 2
SSD_STATE = 128
SSD_CONV = 4
SSD_CONV_DIM = SSD_INNER + 2 * SSD_GROUPS * SSD_STATE
CHUNK = 128
NUM_CHUNKS = SEQ // CHUNK
PROJ_WIDTH = 4 * RET_WIDTH + SSD_INNER + SSD_CONV_DIM
DT_PAD = 128
D_FF = 4 * D_MODEL
EPS = 1e-6

VMEM_LIMIT = 56 * 1024 * 1024

_NT = (((1,), (1,)), ((), ()))
_TN = (((0,), (0,)), ((), ()))


def _rms_scale(x):
    return lax.rsqrt(jnp.mean(x * x, axis=-1, keepdims=True) + EPS)


def _ret_gammas():
    return 1.0 - 2.0 ** (-5.0 - np.arange(RET_HEADS, dtype=np.float64))


def _ret_tables():
    lg = np.log(_ret_gammas())
    idx = np.arange(CHUNK, dtype=np.float64)
    rel = idx[:, None] - idx[None, :]
    causal = rel >= 0
    dintra = np.where(causal[None], np.exp(np.where(causal, rel, 0.0)[None] * lg[:, None, None]), 0.0)
    qdec = np.exp((idx + 1.0)[:, None] * lg[None, :])
    kdec = np.exp((CHUNK - 1.0 - idx)[:, None] * lg[None, :])
    qkdec = np.concatenate([np.repeat(qdec, RET_DIM, axis=1), np.repeat(kdec, RET_DIM, axis=1)], axis=1)
    return jnp.asarray(dintra, F32), jnp.asarray(qkdec, F32)


def _rope_tables():
    half = RET_DIM // 2
    inv_freq = ROPE_BASE ** (-np.arange(half, dtype=np.float64) / half)
    ang = np.arange(SEQ, dtype=np.float64)[:, None] * inv_freq[None, :]
    return jnp.asarray(np.cos(ang), F32), jnp.asarray(np.sin(ang), F32)


IN_TM = 2048
IN_TN = 512
IN_NT = PROJ_WIDTH // IN_TN
IN_MT = TOKENS // IN_TM
IN_TILES = IN_MT * IN_NT
IN_STEPS = IN_NT + IN_TILES + 1
IN_RB = 256
IN_NCH = 8
IN_CH = IN_TM // IN_NCH
IN_SEQ_TILES = SEQ // IN_TM
IN_N_ROPE = 2 * RET_WIDTH // IN_TN
IN_N_K = RET_WIDTH // IN_TN
IN_N_GATE = (3 * RET_WIDTH // IN_TN, (4 * RET_WIDTH + SSD_INNER) // IN_TN)


def _in_divmod(s):
    assert IN_NT == 13
    q = lax.shift_right_logical(s * 5042, 16)
    return q, s - q * IN_NT


def _in_tile(s, lag):
    return _in_divmod(jnp.clip(s - IN_NT - lag, 0, IN_TILES - 1))


def _inproj_kernel(x_ref, nw_ref, w_ref, wdt_ref, cos_ref, sin_ref, dec_ref, cw_ref, cb_ref,
                   proj_ref, qkd_ref, dt_ref, hn_ref, raw_ref, carry_ref):
    s = pl.program_id(0)
    t = s - IN_NT
    m, n = _in_tile(s, 0)
    pm, pn = _in_tile(s, 1)

    def normalise():
        row, col = _in_divmod(s)
        slot = row % 2
        r = pl.multiple_of(jnp.minimum(col, IN_NCH - 1) * IN_CH, IN_CH)
        x = x_ref[...]
        hn_ref[slot, pl.ds(r, IN_CH), :] = (x * _rms_scale(x) * nw_ref[...]).astype(BF16)

    def matmul_rows(rb, wbf):
        rows = pl.ds(rb * IN_RB, IN_RB)
        raw_ref[pl.ds(8 + rb * IN_RB, IN_RB), :] = lax.dot_general(
            hn_ref[m % 2, rows, :], wbf, _NT, preferred_element_type=F32)

    def epilogue_rows(kind, rb):
        rows = pl.ds(rb * IN_RB, IN_RB)
        a = raw_ref[pl.ds(8 + rb * IN_RB, IN_RB), :]
        if kind == "rope":
            a = a * jnp.where(pn >= IN_N_K, RET_DIM ** -0.5, 1.0)
            cos, sin = cos_ref[rows, :], sin_ref[rows, :]
            half = RET_DIM // 2
            parts = []
            for hh in range(IN_TN // RET_DIM):
                x1 = a[:, hh * RET_DIM:hh * RET_DIM + half]
                x2 = a[:, hh * RET_DIM + half:(hh + 1) * RET_DIM]
                parts += [x1 * cos - x2 * sin, x1 * sin + x2 * cos]
            r = jnp.concatenate(parts, axis=-1)
            proj_ref[rows, :] = r.astype(BF16)
            rd = r.reshape(IN_RB // CHUNK, CHUNK, IN_TN) * dec_ref[...][None]
            qkd_ref[rows, :] = rd.reshape(IN_RB, IN_TN).astype(BF16)
        elif kind == "plain":
            proj_ref[rows, :] = a.astype(BF16)
        elif kind == "silu":
            proj_ref[rows, :] = jax.nn.silu(a).astype(BF16)
        else:
            conv = cb_ref[...] + cw_ref[SSD_CONV - 1:SSD_CONV, :] * a
            for tap in range(SSD_CONV - 1):
                off = 8 - (SSD_CONV - 1) + tap + rb * IN_RB
                conv = conv + cw_ref[tap:tap + 1, :] * raw_ref[pl.ds(off, IN_RB), :]
            proj_ref[rows, :] = jax.nn.silu(conv).astype(BF16)

    def step(do_matmul, kind):
        if kind == "conv":
            j = pn - IN_N_GATE[1]
            raw_ref[0:8, :] = jnp.where(pm % IN_SEQ_TILES == 0, 0.0, carry_ref[j])
            carry_ref[j] = raw_ref[IN_TM:IN_TM + 8, :]
        if do_matmul:
            wbf = w_ref[...].astype(BF16)
        last = IN_TM // IN_RB - 1
        for rb in range(last, -1, -1):
            if kind is not None:
                epilogue_rows(kind, rb)
            if do_matmul:
                matmul_rows(rb, wbf)
            if do_matmul and rb == last:
                normalise()

    @pl.when((t >= 0) & (t < IN_TILES) & (n == 0))
    def _():
        dt_ref[...] = lax.dot_general(hn_ref[m % 2], wdt_ref[...].astype(BF16), _NT,
                                      preferred_element_type=F32)

    @pl.when(s < IN_NT)
    def _():
        normalise()

    @pl.when(t == 0)
    def _():
        step(True, None)

    live = (t >= 1) & (t < IN_TILES)

    @pl.when(live & (pn < IN_N_ROPE))
    def _():
        step(True, "rope")

    @pl.when(live & (pn >= IN_N_ROPE) & (pn < IN_N_GATE[0]))
    def _():
        step(True, "plain")

    @pl.when(live & (pn >= IN_N_GATE[0]) & (pn < IN_N_GATE[1]))
    def _():
        step(True, "silu")

    @pl.when(live & (pn >= IN_N_GATE[1]))
    def _():
        step(True, "conv")

    @pl.when(t == IN_TILES)
    def _():
        step(False, "conv")


def _inproj(x2d, norm_w, w_main, w_dt, conv_w, conv_b):
    cos, sin = _rope_tables()
    _, qkdec = _ret_tables()
    n_conv = SSD_CONV_DIM // IN_TN
    rope_rows = lambda s: (_in_tile(s, 1)[0] % IN_SEQ_TILES, 0)
    conv_tile = lambda s: (0, jnp.maximum(_in_tile(s, 1)[1] - IN_N_GATE[1], 0))
    return pl.pallas_call(
        _inproj_kernel,
        grid=(IN_STEPS,),
        in_specs=[
            pl.BlockSpec((IN_CH, D_MODEL),
                         lambda s: (jnp.minimum(_in_divmod(s)[0], IN_MT - 1) * IN_NCH
                                    + jnp.minimum(_in_divmod(s)[1], IN_NCH - 1), 0)),
            pl.BlockSpec((1, D_MODEL), lambda s: (0, 0)),
            pl.BlockSpec((IN_TN, D_MODEL), lambda s: (_in_tile(s, 0)[1], 0)),
            pl.BlockSpec((DT_PAD, D_MODEL), lambda s: (0, 0)),
            pl.BlockSpec((IN_TM, RET_DIM // 2), rope_rows),
            pl.BlockSpec((IN_TM, RET_DIM // 2), rope_rows),
            pl.BlockSpec((CHUNK, IN_TN), lambda s: (0, jnp.minimum(_in_tile(s, 1)[1], IN_N_ROPE - 1))),
            pl.BlockSpec((SSD_CONV, IN_TN), conv_tile),
            pl.BlockSpec((1, IN_TN), conv_tile),
        ],
        out_specs=[
            pl.BlockSpec((IN_TM, IN_TN), lambda s: _in_tile(s, 1)),
            pl.BlockSpec((IN_TM, IN_TN),
                         lambda s: (_in_tile(s, 1)[0], jnp.minimum(_in_tile(s, 1)[1], IN_N_ROPE - 1))),
            pl.BlockSpec((IN_TM, DT_PAD), lambda s: (_in_tile(s, 0)[0], 0)),
        ],
        out_shape=[
            jax.ShapeDtypeStruct((TOKENS, PROJ_WIDTH), BF16),
            jax.ShapeDtypeStruct((TOKENS, 2 * RET_WIDTH), BF16),
            jax.ShapeDtypeStruct((TOKENS, DT_PAD), F32),
        ],
        scratch_shapes=[
            pltpu.VMEM((2, IN_TM, D_MODEL), BF16),
            pltpu.VMEM((IN_TM + 8, IN_TN), F32),
            pltpu.VMEM((n_conv, 8, IN_TN), F32),
        ],
        compiler_params=pltpu.CompilerParams(
            dimension_semantics=("arbitrary",),
            vmem_limit_bytes=VMEM_LIMIT),
        name="inproj",
    )(x2d, norm_w, w_main, w_dt, cos, sin, qkdec, conv_w, conv_b)


def _cumsum_lanes(x):
    lane = lax.broadcasted_iota(jnp.int32, x.shape, 1)
    k = 1
    while k < x.shape[1]:
        x = x + jnp.where(lane >= k, pltpu.roll(x, k, axis=1), 0.0)
        k *= 2
    return x


MIX_NB = 2
MIX_N_BATCHED = 10
MIX_STAGES = 5


MIX_N_CONSTS = 4
OUT_NB = 8
OUT_BLOCKS_PER_STAGE = (2, 1, 2, 1, 2)


def _mixer_kernel(*refs):
    ins = refs[:MIX_N_BATCHED]
    consts = refs[MIX_N_BATCHED:MIX_N_BATCHED + MIX_N_CONSTS]
    x_ref, wout_ref, nw_ref, h_ref, hn_ref, rstate, sstate, mix_ref, wbf_ref = (
        refs[MIX_N_BATCHED + MIX_N_CONSTS:])
    b, c = pl.program_id(0), pl.program_id(1)

    @pl.when((b == 0) & (c == 0))
    def _():
        wbf_ref[...] = wout_ref[...].astype(BF16)

    @pl.when(c == 0)
    def _():
        rstate[...] = jnp.zeros_like(rstate)
        sstate[...] = jnp.zeros_like(sstate)

    def step(do_mixer, do_outproj):
        chains = []
        if do_mixer:
            chains += [_mixer_stages(*[r.at[bi] for r in ins], *consts,
                                     mix_ref.at[bi], rstate.at[bi], sstate.at[bi])
                       for bi in range(MIX_NB)]
        if do_outproj:
            chains.append(_outproj_stages(mix_ref, wbf_ref, x_ref, nw_ref, h_ref, hn_ref))
        for _ in range(MIX_STAGES):
            for chain in chains:
                next(chain)

    @pl.when(c == 0)
    def _():
        step(True, False)

    @pl.when((c > 0) & (c < NUM_CHUNKS))
    def _():
        step(True, True)

    @pl.when(c == NUM_CHUNKS)
    def _():
        step(False, True)


def _outproj_stages(mix_ref, wbf_ref, x_ref, nw_ref, h_ref, hn_ref):
    assert sum(OUT_BLOCKS_PER_STAGE) == OUT_NB and len(OUT_BLOCKS_PER_STAGE) == MIX_STAGES
    lhs = jnp.concatenate([mix_ref[bi] for bi in range(MIX_NB)], axis=0)
    nbw = D_MODEL // OUT_NB
    ssq = [0.0] * MIX_NB
    nb = 0
    for stage, blocks in enumerate(OUT_BLOCKS_PER_STAGE):
        for _ in range(blocks):
            cols = slice(nb * nbw, (nb + 1) * nbw)
            acc = jnp.dot(lhs, wbf_ref[:, cols], preferred_element_type=F32)
            for bi in range(MIX_NB):
                h = x_ref[bi, :, cols] + acc[bi * CHUNK:(bi + 1) * CHUNK, :]
                h_ref[bi, :, cols] = h
                ssq[bi] = ssq[bi] + jnp.sum(h * h, axis=-1, keepdims=True)
            nb += 1
        if stage == MIX_STAGES - 1:
            for bi in range(MIX_NB):
                scale = lax.rsqrt(ssq[bi] * (1.0 / D_MODEL) + EPS)
                hn_ref[bi] = (h_ref[bi] * scale * nw_ref[...]).astype(BF16)
        yield


def _mixer_stages(q_ref, k_ref, v_ref, g_ref, z_ref, xs_ref, bc_ref, qd_ref, kd_ref, dt_ref,
                  dintra_ref, rnw_ref, hp_ref, snw_ref, out_ref, rstate, sstate):
    hpg = SSD_HEADS // SSD_GROUPS
    gw = hpg * SSD_HEAD_DIM
    cbase = SSD_GROUPS * SSD_STATE
    heads = [slice(h * RET_DIM, (h + 1) * RET_DIM) for h in range(RET_HEADS)]
    chunk_decay = _ret_gammas() ** CHUNK

    def rows_of(t, hh):
        return jnp.broadcast_to(t[hh:hh + 1, :], (SSD_HEAD_DIM, CHUNK))

    dt_t = jax.nn.softplus(dt_ref[...].T[0:SSD_HEADS, :] + hp_ref[0])
    acs_t = _cumsum_lanes(dt_t * (-jnp.exp(hp_ref[1])))
    a_last = acs_t[:, CHUNK - 1:CHUNK]
    w_t = jnp.exp(a_last - acs_t) * dt_t
    ea_t = jnp.exp(acs_t)
    cdec = jnp.broadcast_to(jnp.exp(a_last), (SSD_HEADS, CHUNK))
    acs_pad = jnp.concatenate([acs_t, jnp.zeros((CHUNK - SSD_HEADS, CHUNK), F32)], axis=0)
    acs_col = acs_pad.T
    dskip = hp_ref[2]
    yield

    scores, ycross, kv = [], [], []
    for h, sl in enumerate(heads):
        scores.append(lax.dot_general(q_ref[:, sl], k_ref[:, sl], _NT, preferred_element_type=F32))
        ycross.append(jnp.dot(qd_ref[:, sl], rstate[h].astype(BF16), preferred_element_type=F32))
        kv.append(lax.dot_general(kd_ref[:, sl], v_ref[:, sl], _TN, preferred_element_type=F32))
    xs_t = xs_ref[...].astype(F32).T
    bgs = [bc_ref[:, g * SSD_STATE:(g + 1) * SSD_STATE] for g in range(SSD_GROUPS)]
    cgs = [bc_ref[:, cbase + g * SSD_STATE:cbase + (g + 1) * SSD_STATE] for g in range(SSD_GROUPS)]
    zero_blk = jnp.zeros((CHUNK, SSD_STATE), BF16)

    def block_diag(a, b):
        return jnp.concatenate([jnp.concatenate([a, zero_blk], axis=1),
                                jnp.concatenate([zero_blk, b], axis=1)], axis=0)

    bd_b, bd_c = block_diag(*bgs), block_diag(*cgs)
    cbs = [lax.dot_general(cgs[g], bgs[g], _NT, preferred_element_type=F32)
           for g in range(SSD_GROUPS)]
    s_prev = sstate[...]
    yo_all = lax.dot_general(s_prev.astype(BF16), bd_c, _NT,
                             preferred_element_type=F32)
    yield

    ps = [(scores[h] * dintra_ref[h]).astype(BF16) for h in range(RET_HEADS)]
    for h in range(RET_HEADS):
        rstate[h] = float(chunk_decay[h]) * rstate[h] + kv[h]
    row = lax.broadcasted_iota(jnp.int32, (CHUNK, CHUNK), 0)
    col = lax.broadcasted_iota(jnp.int32, (CHUNK, CHUNK), 1)
    causal = row >= col
    ms, xws = [], []
    for hh in range(SSD_HEADS):
        xs_h = xs_t[hh * SSD_HEAD_DIM:(hh + 1) * SSD_HEAD_DIM, :]
        seg = (jnp.broadcast_to(acs_col[:, hh:hh + 1], (CHUNK, CHUNK))
               - jnp.broadcast_to(acs_t[hh:hh + 1, :], (CHUNK, CHUNK)))
        decay = jnp.exp(jnp.where(causal, seg, -jnp.inf))
        dt_s = jnp.broadcast_to(dt_t[hh:hh + 1, :], (CHUNK, CHUNK))
        ms.append((cbs[hh // hpg] * decay * dt_s).astype(BF16))
        xws.append((xs_h * rows_of(w_t, hh)).astype(BF16))
    yield

    ys = [jnp.dot(ps[h], v_ref[:, sl], preferred_element_type=F32) + ycross[h]
          for h, sl in enumerate(heads)]
    lane = lax.broadcasted_iota(jnp.int32, (CHUNK, 2 * SSD_HEAD_DIM), 1)
    yd_pairs = []
    for j in range(SSD_HEADS // 2):
        m_pair = jnp.concatenate([ms[2 * j], ms[2 * j + 1]], axis=1)
        x_pair = xs_ref[:, 2 * j * SSD_HEAD_DIM:(2 * j + 2) * SSD_HEAD_DIM]
        zero = jnp.zeros_like(x_pair)
        x_bd = jnp.concatenate([jnp.where(lane < SSD_HEAD_DIM, x_pair, zero),
                                jnp.where(lane >= SSD_HEAD_DIM, x_pair, zero)], axis=0)
        yd_pairs.append(jnp.dot(m_pair, x_bd, preferred_element_type=F32))
    xw = jnp.concatenate([jnp.concatenate(xws[g * hpg:(g + 1) * hpg], axis=0)
                          for g in range(SSD_GROUPS)], axis=1)
    cd = jnp.concatenate(
        [jnp.concatenate([rows_of(cdec, hh) for hh in range(g * hpg, (g + 1) * hpg)], axis=0)
         for g in range(SSD_GROUPS)], axis=1)
    sstate[...] = cd * s_prev + jnp.dot(xw, bd_b, preferred_element_type=F32)
    yield

    for h, sl in enumerate(heads):
        yn = ys[h] * _rms_scale(ys[h]) * rnw_ref[:, sl]
        out_ref[:, sl] = (yn * g_ref[:, sl].astype(F32)).astype(BF16)
    rest_t = []
    for hh in range(SSD_HEADS):
        g, e = divmod(hh, hpg)
        xs_h = xs_t[hh * SSD_HEAD_DIM:(hh + 1) * SSD_HEAD_DIM, :]
        yo = yo_all[e * SSD_HEAD_DIM:(e + 1) * SSD_HEAD_DIM, g * SSD_STATE:(g + 1) * SSD_STATE]
        rest_t.append(yo * rows_of(ea_t, hh) + rows_of(dskip, hh) * xs_h)
    y = jnp.concatenate(yd_pairs, axis=1) + jnp.concatenate(rest_t, axis=0).T
    y = y * z_ref[...].astype(F32)
    for g in range(SSD_GROUPS):
        sl = slice(g * gw, (g + 1) * gw)
        yg = y[:, sl]
        out_ref[:, RET_WIDTH + g * gw:RET_WIDTH + (g + 1) * gw] = (
            yg * _rms_scale(yg) * snw_ref[:, sl]).astype(BF16)
    yield


def _mixer(proj, qkd, dt, x, w_out, norm_w, params):
    (ret_norm_w, dt_bias, a_log, d_skip, ssd_norm_w) = params
    dintra, _ = _ret_tables()
    proj = proj.reshape(BATCH, SEQ, PROJ_WIDTH)
    qkd = qkd.reshape(BATCH, SEQ, 2 * RET_WIDTH)
    dt = dt.reshape(BATCH, SEQ, DT_PAD)
    last = NUM_CHUNKS - 1

    def col_block(j, width):
        return pl.BlockSpec((MIX_NB, CHUNK, width), lambda b, c: (b, jnp.minimum(c, last), j))

    def lagged(width):
        return pl.BlockSpec((MIX_NB, CHUNK, width), lambda b, c: (b, jnp.maximum(c - 1, 0), 0))

    def full(shape, **kw):
        return pl.BlockSpec(shape, lambda b, c: (0,) * len(shape), **kw)

    head_params = jnp.broadcast_to(
        jnp.stack([dt_bias, a_log, d_skip]).astype(F32)[:, :, None], (3, SSD_HEADS, CHUNK))
    in_specs = [
        col_block(0, RET_WIDTH), col_block(1, RET_WIDTH), col_block(2, RET_WIDTH),
        col_block(3, RET_WIDTH), col_block(4, SSD_INNER), col_block(5, SSD_INNER),
        col_block(12, 2 * SSD_GROUPS * SSD_STATE),
        col_block(0, RET_WIDTH), col_block(1, RET_WIDTH),
        col_block(0, DT_PAD),
        full((RET_HEADS, CHUNK, CHUNK)),
        full((1, RET_WIDTH)),
        full((3, SSD_HEADS, CHUNK)),
        full((1, SSD_INNER)),
        lagged(D_MODEL),
        full((D_MODEL, D_MODEL), pipeline_mode=pl.Buffered(1)),
        full((1, D_MODEL)),
    ]
    assert len(in_specs) == MIX_N_BATCHED + MIX_N_CONSTS + 3
    h, hn = pl.pallas_call(
        _mixer_kernel,
        grid=(BATCH // MIX_NB, NUM_CHUNKS + 1),
        in_specs=in_specs,
        out_specs=[lagged(D_MODEL), lagged(D_MODEL)],
        out_shape=[
            jax.ShapeDtypeStruct((BATCH, SEQ, D_MODEL), F32),
            jax.ShapeDtypeStruct((BATCH, SEQ, D_MODEL), BF16),
        ],
        scratch_shapes=[
            pltpu.VMEM((MIX_NB, RET_HEADS, RET_DIM, RET_DIM), F32),
            pltpu.VMEM((MIX_NB, SSD_INNER // SSD_GROUPS, SSD_GROUPS * SSD_STATE), F32),
            pltpu.VMEM((MIX_NB, CHUNK, D_MODEL), BF16),
            pltpu.VMEM((D_MODEL, D_MODEL), BF16),
        ],
        compiler_params=pltpu.CompilerParams(
            dimension_semantics=("arbitrary", "arbitrary"),
            vmem_limit_bytes=VMEM_LIMIT),
        name="mixer",
    )(proj, proj, proj, proj, proj, proj, proj, qkd, qkd, dt, dintra,
      ret_norm_w.astype(F32)[None, :],
      head_params, ssd_norm_w.astype(F32)[None, :],
      x, w_out, norm_w)
    return h.reshape(TOKENS, D_MODEL), hn.reshape(TOKENS, D_MODEL)


MLP_TM = 1024
MLP_TF = 512
MLP_NF = D_FF // MLP_TF
MLP_HROWS = MLP_TM // MLP_NF
MLP_ROWS = 128


def _mlp_kernel(hn_ref, wup_ref, wdn_ref, h_ref, nw_ref, out_ref):
    f = pl.program_id(1)

    @pl.when(f == 0)
    def _():
        out_ref[...] = jnp.zeros_like(out_ref)

    u = jnp.maximum(jnp.dot(hn_ref[...], wup_ref[...].astype(BF16),
                            preferred_element_type=F32), 0.0)
    out_ref[...] += jnp.dot((u * u).astype(BF16), wdn_ref[...].astype(BF16),
                            preferred_element_type=F32)
    r = pl.multiple_of(f * MLP_HROWS, MLP_HROWS)
    out_ref[pl.ds(r, MLP_HROWS), :] += h_ref[...]

    @pl.when(f == MLP_NF - 1)
    def _():
        def body(i, carry):
            r = pl.multiple_of(i * MLP_ROWS, MLP_ROWS)
            h = out_ref[pl.ds(r, MLP_ROWS), :]
            out_ref[pl.ds(r, MLP_ROWS), :] = h * _rms_scale(h) * nw_ref[...]
            return carry
        lax.fori_loop(0, MLP_TM // MLP_ROWS, body, 0)


def _mlp(hn, w_up, w_down, h, norm_w):
    return pl.pallas_call(
        _mlp_kernel,
        grid=(TOKENS // MLP_TM, MLP_NF),
        in_specs=[
            pl.BlockSpec((MLP_TM, D_MODEL), lambda m, f: (m, 0)),
            pl.BlockSpec((D_MODEL, MLP_TF), lambda m, f: (0, f)),
            pl.BlockSpec((MLP_TF, D_MODEL), lambda m, f: (f, 0)),
            pl.BlockSpec((MLP_HROWS, D_MODEL), lambda m, f: (m * MLP_NF + f, 0)),
            pl.BlockSpec((1, D_MODEL), lambda m, f: (0, 0)),
        ],
        out_specs=pl.BlockSpec((MLP_TM, D_MODEL), lambda m, f: (m, 0)),
        out_shape=jax.ShapeDtypeStruct((TOKENS, D_MODEL), F32),
        compiler_params=pltpu.CompilerParams(
            dimension_semantics=("arbitrary", "arbitrary"),
            vmem_limit_bytes=VMEM_LIMIT),
        name="mlp",
    )(hn, w_up, w_down, h, norm_w)


def kernel(x, norm_mix_w, w_in, ret_norm_w, conv_w, conv_b, dt_bias, a_log, d_skip, ssd_norm_w,
           w_out, norm_mlp_w, w_up, w_down, norm_final_w):
    x2d = x.reshape(TOKENS, D_MODEL)
    w_in_t = w_in.T
    w_dt_t = jnp.pad(w_in_t[PROJ_WIDTH:, :], ((0, DT_PAD - SSD_HEADS), (0, 0)))
    proj, qkd, dt = _inproj(x2d, norm_mix_w.astype(F32)[None, :], w_in_t, w_dt_t,
                            conv_w.astype(F32), conv_b.astype(F32)[None, :])
    h, hn = _mixer(proj, qkd, dt, x, w_out, norm_mlp_w.astype(F32)[None, :],
                   (ret_norm_w, dt_bias, a_log, d_skip, ssd_norm_w))
    out = _mlp(hn, w_up, w_down, h, norm_final_w.astype(F32)[None, :])
    return out.reshape(BATCH, SEQ, D_MODEL)
```

```python
import numpy as np
import jax
import jax.numpy as jnp
from jax import lax
from jax.experimental import pallas as pl
from jax.experimental.pallas import tpu as pltpu

F32 = jnp.float32
BF16 = jnp.bfloat16

D_MODEL = 2048
BATCH = 4
SEQ = 2048
TOKENS = BATCH * SEQ
RET_HEADS = 4
RET_DIM = 256
RET_WIDTH = RET_HEADS * RET_DIM
ROPE_BASE = 10000.0
SSD_INNER = 1024
SSD_HEAD_DIM = 64
SSD_HEADS = 16
SSD_GROUPS = 2
SSD_STATE = 128
SSD_CONV = 4
SSD_CONV_DIM = SSD_INNER + 2 * SSD_GROUPS * SSD_STATE
CHUNK = 128
NUM_CHUNKS = SEQ // CHUNK
PROJ_WIDTH = 4 * RET_WIDTH + SSD_INNER + SSD_CONV_DIM
DT_PAD = 128
D_FF = 4 * D_MODEL
EPS = 1e-6

VMEM_LIMIT = 56 * 1024 * 1024

_NT = (((1,), (1,)), ((), ()))
_TN = (((0,), (0,)), ((), ()))


def _rms_scale(x):
    return lax.rsqrt(jnp.mean(x * x, axis=-1, keepdims=True) + EPS)


def _ret_gammas():
    return 1.0 - 2.0 ** (-5.0 - np.arange(RET_HEADS, dtype=np.float64))


def _ret_tables():
    lg = np.log(_ret_gammas())
    idx = np.arange(CHUNK, dtype=np.float64)
    rel = idx[:, None] - idx[None, :]
    causal = rel >= 0
    dintra = np.where(causal[None], np.exp(np.where(causal, rel, 0.0)[None] * lg[:, None, None]), 0.0)
    qdec = np.exp((idx + 1.0)[:, None] * lg[None, :])
    kdec = np.exp((CHUNK - 1.0 - idx)[:, None] * lg[None, :])
    qkdec = np.concatenate([np.repeat(qdec, RET_DIM, axis=1), np.repeat(kdec, RET_DIM, axis=1)], axis=1)
    return jnp.asarray(dintra, F32), jnp.asarray(qkdec, F32)


def _rope_tables():
    half = RET_DIM // 2
    inv_freq = ROPE_BASE ** (-np.arange(half, dtype=np.float64) / half)
    ang = np.arange(SEQ, dtype=np.float64)[:, None] * inv_freq[None, :]
    return jnp.asarray(np.cos(ang), F32), jnp.asarray(np.sin(ang), F32)


IN_TM = 2048
IN_TN = 512
IN_NT = PROJ_WIDTH // IN_TN
IN_MT = TOKENS // IN_TM
IN_TILES = IN_MT * IN_NT
IN_STEPS = IN_NT + IN_TILES + 1
IN_RB = 256
IN_NCH = 8
IN_CH = IN_TM // IN_NCH
IN_SEQ_TILES = SEQ // IN_TM
IN_N_ROPE = 2 * RET_WIDTH // IN_TN
IN_N_K = RET_WIDTH // IN_TN
IN_N_GATE = (3 * RET_WIDTH // IN_TN, (4 * RET_WIDTH + SSD_INNER) // IN_TN)


def _in_divmod(s):
    assert IN_NT == 13
    q = lax.shift_right_logical(s * 5042, 16)
    return q, s - q * IN_NT


def _in_tile(s, lag):
    return _in_divmod(jnp.clip(s - IN_NT - lag, 0, IN_TILES - 1))


def _inproj_kernel(x_ref, nw_ref, w_ref, wdt_ref, cos_ref, sin_ref, dec_ref, cw_ref, cb_ref,
                   proj_ref, qkd_ref, dt_ref, hn_ref, raw_ref, carry_ref):
    s = pl.program_id(0)
    t = s - IN_NT
    m, n = _in_tile(s, 0)
    pm, pn = _in_tile(s, 1)

    def normalise():
        row, col = _in_divmod(s)
        slot = row % 2
        r = pl.multiple_of(jnp.minimum(col, IN_NCH - 1) * IN_CH, IN_CH)
        x = x_ref[...]
        hn_ref[slot, pl.ds(r, IN_CH), :] = (x * _rms_scale(x) * nw_ref[...]).astype(BF16)

    def matmul_rows(rb, wbf):
        rows = pl.ds(rb * IN_RB, IN_RB)
        raw_ref[pl.ds(8 + rb * IN_RB, IN_RB), :] = lax.dot_general(
            hn_ref[m % 2, rows, :], wbf, _NT, preferred_element_type=F32)

    def epilogue_rows(kind, rb):
        rows = pl.ds(rb * IN_RB, IN_RB)
        a = raw_ref[pl.ds(8 + rb * IN_RB, IN_RB), :]
        if kind == "rope":
            a = a * jnp.where(pn >= IN_N_K, RET_DIM ** -0.5, 1.0)
            cos, sin = cos_ref[rows, :], sin_ref[rows, :]
            half = RET_DIM // 2
            parts = []
            for hh in range(IN_TN // RET_DIM):
                x1 = a[:, hh * RET_DIM:hh * RET_DIM + half]
                x2 = a[:, hh * RET_DIM + half:(hh + 1) * RET_DIM]
                parts += [x1 * cos - x2 * sin, x1 * sin + x2 * cos]
            r = jnp.concatenate(parts, axis=-1)
            proj_ref[rows, :] = r.astype(BF16)
            rd = r.reshape(IN_RB // CHUNK, CHUNK, IN_TN) * dec_ref[...][None]
            qkd_ref[rows, :] = rd.reshape(IN_RB, IN_TN).astype(BF16)
        elif kind == "plain":
            proj_ref[rows, :] = a.astype(BF16)
        elif kind == "silu":
            proj_ref[rows, :] = jax.nn.silu(a).astype(BF16)
        else:
            conv = cb_ref[...] + cw_ref[SSD_CONV - 1:SSD_CONV, :] * a
            for tap in range(SSD_CONV - 1):
                off = 8 - (SSD_CONV - 1) + tap + rb * IN_RB
                conv = conv + cw_ref[tap:tap + 1, :] * raw_ref[pl.ds(off, IN_RB), :]
            proj_ref[rows, :] = jax.nn.silu(conv).astype(BF16)

    def step(do_matmul, kind):
        if kind == "conv":
            j = pn - IN_N_GATE[1]
            raw_ref[0:8, :] = jnp.where(pm % IN_SEQ_TILES == 0, 0.0, carry_ref[j])
            carry_ref[j] = raw_ref[IN_TM:IN_TM + 8, :]
        if do_matmul:
            wbf = w_ref[...].astype(BF16)
        last = IN_TM // IN_RB - 1
        for rb in range(last, -1, -1):
            if kind is not None:
                epilogue_rows(kind, rb)
            if do_matmul:
                matmul_rows(rb, wbf)
            if do_matmul and rb == last:
                normalise()

    @pl.when((t >= 0) & (t < IN_TILES) & (n == 0))
    def _():
        dt_ref[...] = lax.dot_general(hn_ref[m % 2], wdt_ref[...].astype(BF16), _NT,
                                      preferred_element_type=F32)

    @pl.when(s < IN_NT)
    def _():
        normalise()

    @pl.when(t == 0)
    def _():
        step(True, None)

    live = (t >= 1) & (t < IN_TILES)

    @pl.when(live & (pn < IN_N_ROPE))
    def _():
        step(True, "rope")

    @pl.when(live & (pn >= IN_N_ROPE) & (pn < IN_N_GATE[0]))
    def _():
        step(True, "plain")

    @pl.when(live & (pn >= IN_N_GATE[0]) & (pn < IN_N_GATE[1]))
    def _():
        step(True, "silu")

    @pl.when(live & (pn >= IN_N_GATE[1]))
    def _():
        step(True, "conv")

    @pl.when(t == IN_TILES)
    def _():
        step(False, "conv")


def _inproj(x2d, norm_w, w_main, w_dt, conv_w, conv_b):
    cos, sin = _rope_tables()
    _, qkdec = _ret_tables()
    n_conv = SSD_CONV_DIM // IN_TN
    rope_rows = lambda s: (_in_tile(s, 1)[0] % IN_SEQ_TILES, 0)
    conv_tile = lambda s: (0, jnp.maximum(_in_tile(s, 1)[1] - IN_N_GATE[1], 0))
    return pl.pallas_call(
        _inproj_kernel,
        grid=(IN_STEPS,),
        in_specs=[
            pl.BlockSpec((IN_CH, D_MODEL),
                         lambda s: (jnp.minimum(_in_divmod(s)[0], IN_MT - 1) * IN_NCH
                                    + jnp.minimum(_in_divmod(s)[1], IN_NCH - 1), 0)),
            pl.BlockSpec((1, D_MODEL), lambda s: (0, 0)),
            pl.BlockSpec((IN_TN, D_MODEL), lambda s: (_in_tile(s, 0)[1], 0)),
            pl.BlockSpec((DT_PAD, D_MODEL), lambda s: (0, 0)),
            pl.BlockSpec((IN_TM, RET_DIM // 2), rope_rows),
            pl.BlockSpec((IN_TM, RET_DIM // 2), rope_rows),
            pl.BlockSpec((CHUNK, IN_TN), lambda s: (0, jnp.minimum(_in_tile(s, 1)[1], IN_N_ROPE - 1))),
            pl.BlockSpec((SSD_CONV, IN_TN), conv_tile),
            pl.BlockSpec((1, IN_TN), conv_tile),
        ],
        out_specs=[
            pl.BlockSpec((IN_TM, IN_TN), lambda s: _in_tile(s, 1)),
            pl.BlockSpec((IN_TM, IN_TN),
                         lambda s: (_in_tile(s, 1)[0], jnp.minimum(_in_tile(s, 1)[1], IN_N_ROPE - 1))),
            pl.BlockSpec((IN_TM, DT_PAD), lambda s: (_in_tile(s, 0)[0], 0)),
        ],
        out_shape=[
            jax.ShapeDtypeStruct((TOKENS, PROJ_WIDTH), BF16),
            jax.ShapeDtypeStruct((TOKENS, 2 * RET_WIDTH), BF16),
            jax.ShapeDtypeStruct((TOKENS, DT_PAD), F32),
        ],
        scratch_shapes=[
            pltpu.VMEM((2, IN_TM, D_MODEL), BF16),
            pltpu.VMEM((IN_TM + 8, IN_TN), F32),
            pltpu.VMEM((n_conv, 8, IN_TN), F32),
        ],
        compiler_params=pltpu.CompilerParams(
            dimension_semantics=("arbitrary",),
            vmem_limit_bytes=VMEM_LIMIT),
        name="inproj",
    )(x2d, norm_w, w_main, w_dt, cos, sin, qkdec, conv_w, conv_b)


def _cumsum_lanes(x):
    lane = lax.broadcasted_iota(jnp.int32, x.shape, 1)
    k = 1
    while k < x.shape[1]:
        x = x + jnp.where(lane >= k, pltpu.roll(x, k, axis=1), 0.0)
        k *= 2
    return x


MIX_NB = 2
MIX_N_BATCHED = 3
MIX_STAGES = 5


MIX_N_CONSTS = 4
OUT_NB = 8
OUT_BLOCKS_PER_STAGE = (2, 1, 2, 1, 2)


def _mixer_kernel(*refs):
    ins = refs[:MIX_N_BATCHED]
    consts = refs[MIX_N_BATCHED:MIX_N_BATCHED + MIX_N_CONSTS]
    x_ref, wout_ref, nw_ref, h_ref, hn_ref, rstate, sstate, mix_ref, wbf_ref = (
        refs[MIX_N_BATCHED + MIX_N_CONSTS:])
    b, c = pl.program_id(0), pl.program_id(1)

    @pl.when((b == 0) & (c == 0))
    def _():
        wbf_ref[...] = wout_ref[...].astype(BF16)

    @pl.when(c == 0)
    def _():
        rstate[...] = jnp.zeros_like(rstate)
        sstate[...] = jnp.zeros_like(sstate)

    def step(do_mixer, do_outproj):
        chains = []
        if do_mixer:
            proj_ref, qkd_ref, dt_ref = ins
            for bi in range(MIX_NB):
                seg = lambda ref, j, width=RET_WIDTH: ref.at[bi, :, pl.ds(j * RET_WIDTH, width)]
                views = ([seg(proj_ref, j) for j in range(6)]
                         + [seg(proj_ref, 6, 2 * SSD_GROUPS * SSD_STATE)]
                         + [seg(qkd_ref, 0), seg(qkd_ref, 1), dt_ref.at[bi]])
                chains.append(_mixer_stages(*views, *consts, mix_ref.at[bi],
                                            rstate.at[bi], sstate.at[bi]))
        if do_outproj:
            chains.append(_outproj_stages(mix_ref, wbf_ref, x_ref, nw_ref, h_ref, hn_ref))
        for _ in range(MIX_STAGES):
            for chain in chains:
                next(chain)

    @pl.when(c == 0)
    def _():
        step(True, False)

    @pl.when((c > 0) & (c < NUM_CHUNKS))
    def _():
        step(True, True)

    @pl.when(c == NUM_CHUNKS)
    def _():
        step(False, True)


def _outproj_stages(mix_ref, wbf_ref, x_ref, nw_ref, h_ref, hn_ref):
    assert sum(OUT_BLOCKS_PER_STAGE) == OUT_NB and len(OUT_BLOCKS_PER_STAGE) == MIX_STAGES
    lhs = jnp.concatenate([mix_ref[bi] for bi in range(MIX_NB)], axis=0)
    nbw = D_MODEL // OUT_NB
    ssq = [0.0] * MIX_NB
    nb = 0
    for stage, blocks in enumerate(OUT_BLOCKS_PER_STAGE):
        for _ in range(blocks):
            cols = slice(nb * nbw, (nb + 1) * nbw)
            acc = jnp.dot(lhs, wbf_ref[:, cols], preferred_element_type=F32)
            for bi in range(MIX_NB):
                h = x_ref[bi, :, cols] + acc[bi * CHUNK:(bi + 1) * CHUNK, :]
                h_ref[bi, :, cols] = h
                ssq[bi] = ssq[bi] + jnp.sum(h * h, axis=-1, keepdims=True)
            nb += 1
        if stage == MIX_STAGES - 1:
            for bi in range(MIX_NB):
                scale = lax.rsqrt(ssq[bi] * (1.0 / D_MODEL) + EPS)
                hn_ref[bi] = (h_ref[bi] * scale * nw_ref[...]).astype(BF16)
        yield


def _mixer_stages(q_ref, k_ref, v_ref, g_ref, z_ref, xs_ref, bc_ref, qd_ref, kd_ref, dt_ref,
                  dintra_ref, rnw_ref, hp_ref, snw_ref, out_ref, rstate, sstate):
    hpg = SSD_HEADS // SSD_GROUPS
    gw = hpg * SSD_HEAD_DIM
    cbase = SSD_GROUPS * SSD_STATE
    heads = [slice(h * RET_DIM, (h + 1) * RET_DIM) for h in range(RET_HEADS)]
    chunk_decay = _ret_gammas() ** CHUNK

    def rows_of(t, hh):
        return jnp.broadcast_to(t[hh:hh + 1, :], (SSD_HEAD_DIM, CHUNK))

    dt_t = jax.nn.softplus(dt_ref[...].T[0:SSD_HEADS, :] + hp_ref[0])
    acs_t = _cumsum_lanes(dt_t * (-jnp.exp(hp_ref[1])))
    a_last = acs_t[:, CHUNK - 1:CHUNK]
    w_t = jnp.exp(a_last - acs_t) * dt_t
    ea_t = jnp.exp(acs_t)
    cdec = jnp.broadcast_to(jnp.exp(a_last), (SSD_HEADS, CHUNK))
    acs_pad = jnp.concatenate([acs_t, jnp.zeros((CHUNK - SSD_HEADS, CHUNK), F32)], axis=0)
    acs_col = acs_pad.T
    dskip = hp_ref[2]
    yield

    scores, ycross, kv = [], [], []
    for h, sl in enumerate(heads):
        scores.append(lax.dot_general(q_ref[:, sl], k_ref[:, sl], _NT, preferred_element_type=F32))
        ycross.append(jnp.dot(qd_ref[:, sl], rstate[h].astype(BF16), preferred_element_type=F32))
        kv.append(lax.dot_general(kd_ref[:, sl], v_ref[:, sl], _TN, preferred_element_type=F32))
    xs_t = xs_ref[...].astype(F32).T
    bgs = [bc_ref[:, g * SSD_STATE:(g + 1) * SSD_STATE] for g in range(SSD_GROUPS)]
    cgs = [bc_ref[:, cbase + g * SSD_STATE:cbase + (g + 1) * SSD_STATE] for g in range(SSD_GROUPS)]
    zero_blk = jnp.zeros((CHUNK, SSD_STATE), BF16)

    def block_diag(a, b):
        return jnp.concatenate([jnp.concatenate([a, zero_blk], axis=1),
                                jnp.concatenate([zero_blk, b], axis=1)], axis=0)

    bd_b, bd_c = block_diag(*bgs), block_diag(*cgs)
    cbs = [lax.dot_general(cgs[g], bgs[g], _NT, preferred_element_type=F32)
           for g in range(SSD_GROUPS)]
    s_prev = sstate[...]
    yo_all = lax.dot_general(s_prev.astype(BF16), bd_c, _NT,
                             preferred_element_type=F32)
    yield

    ps = [(scores[h] * dintra_ref[h]).astype(BF16) for h in range(RET_HEADS)]
    for h in range(RET_HEADS):
        rstate[h] = float(chunk_decay[h]) * rstate[h] + kv[h]
    row = lax.broadcasted_iota(jnp.int32, (CHUNK, CHUNK), 0)
    col = lax.broadcasted_iota(jnp.int32, (CHUNK, CHUNK), 1)
    causal = row >= col
    ms, xws = [], []
    for hh in range(SSD_HEADS):
        xs_h = xs_t[hh * SSD_HEAD_DIM:(hh + 1) * SSD_HEAD_DIM, :]
        seg = (jnp.broadcast_to(acs_col[:, hh:hh + 1], (CHUNK, CHUNK))
               - jnp.broadcast_to(acs_t[hh:hh + 1, :], (CHUNK, CHUNK)))
        decay = jnp.exp(jnp.where(causal, seg, -jnp.inf))
        dt_s = jnp.broadcast_to(dt_t[hh:hh + 1, :], (CHUNK, CHUNK))
        ms.append((cbs[hh // hpg] * decay * dt_s).astype(BF16))
        xws.append((xs_h * rows_of(w_t, hh)).astype(BF16))
    yield

    ys = [jnp.dot(ps[h], v_ref[:, sl], preferred_element_type=F32) + ycross[h]
          for h, sl in enumerate(heads)]
    lane = lax.broadcasted_iota(jnp.int32, (CHUNK, 2 * SSD_HEAD_DIM), 1)
    yd_pairs = []
    for j in range(SSD_HEADS // 2):
        m_pair = jnp.concatenate([ms[2 * j], ms[2 * j + 1]], axis=1)
        x_pair = xs_ref[:, 2 * j * SSD_HEAD_DIM:(2 * j + 2) * SSD_HEAD_DIM]
        zero = jnp.zeros_like(x_pair)
        x_bd = jnp.concatenate([jnp.where(lane < SSD_HEAD_DIM, x_pair, zero),
                                jnp.where(lane >= SSD_HEAD_DIM, x_pair, zero)], axis=0)
        yd_pairs.append(jnp.dot(m_pair, x_bd, preferred_element_type=F32))
    xw = jnp.concatenate([jnp.concatenate(xws[g * hpg:(g + 1) * hpg], axis=0)
                          for g in range(SSD_GROUPS)], axis=1)
    cd = jnp.concatenate(
        [jnp.concatenate([rows_of(cdec, hh) for hh in range(g * hpg, (g + 1) * hpg)], axis=0)
         for g in range(SSD_GROUPS)], axis=1)
    sstate[...] = cd * s_prev + jnp.dot(xw, bd_b, preferred_element_type=F32)
    yield

    for h, sl in enumerate(heads):
        yn = ys[h] * _rms_scale(ys[h]) * rnw_ref[:, sl]
        out_ref[:, sl] = (yn * g_ref[:, sl].astype(F32)).astype(BF16)
    rest_t = []
    for hh in range(SSD_HEADS):
        g, e = divmod(hh, hpg)
        xs_h = xs_t[hh * SSD_HEAD_DIM:(hh + 1) * SSD_HEAD_DIM, :]
        yo = yo_all[e * SSD_HEAD_DIM:(e + 1) * SSD_HEAD_DIM, g * SSD_STATE:(g + 1) * SSD_STATE]
        rest_t.append(yo * rows_of(ea_t, hh) + rows_of(dskip, hh) * xs_h)
    y = jnp.concatenate(yd_pairs, axis=1) + jnp.concatenate(rest_t, axis=0).T
    y = y * z_ref[...].astype(F32)
    for g in range(SSD_GROUPS):
        sl = slice(g * gw, (g + 1) * gw)
        yg = y[:, sl]
        out_ref[:, RET_WIDTH + g * gw:RET_WIDTH + (g + 1) * gw] = (
            yg * _rms_scale(yg) * snw_ref[:, sl]).astype(BF16)
    yield


def _mixer(proj, qkd, dt, x, w_out, norm_w, params):
    (ret_norm_w, dt_bias, a_log, d_skip, ssd_norm_w) = params
    dintra, _ = _ret_tables()
    proj = proj.reshape(BATCH, SEQ, PROJ_WIDTH)
    qkd = qkd.reshape(BATCH, SEQ, 2 * RET_WIDTH)
    dt = dt.reshape(BATCH, SEQ, DT_PAD)
    last = NUM_CHUNKS - 1

    def col_block(j, width):
        return pl.BlockSpec((MIX_NB, CHUNK, width), lambda b, c: (b, jnp.minimum(c, last), j))

    def lagged(width):
        return pl.BlockSpec((MIX_NB, CHUNK, width), lambda b, c: (b, jnp.maximum(c - 1, 0), 0))

    def full(shape, **kw):
        return pl.BlockSpec(shape, lambda b, c: (0,) * len(shape), **kw)

    head_params = jnp.broadcast_to(
        jnp.stack([dt_bias, a_log, d_skip]).astype(F32)[:, :, None], (3, SSD_HEADS, CHUNK))
    in_specs = [
        col_block(0, PROJ_WIDTH), col_block(0, 2 * RET_WIDTH), col_block(0, DT_PAD),
        full((RET_HEADS, CHUNK, CHUNK)),
        full((1, RET_WIDTH)),
        full((3, SSD_HEADS, CHUNK)),
        full((1, SSD_INNER)),
        lagged(D_MODEL),
        full((D_MODEL, D_MODEL), pipeline_mode=pl.Buffered(1)),
        full((1, D_MODEL)),
    ]
    assert len(in_specs) == MIX_N_BATCHED + MIX_N_CONSTS + 3
    h, hn = pl.pallas_call(
        _mixer_kernel,
        grid=(BATCH // MIX_NB, NUM_CHUNKS + 1),
        in_specs=in_specs,
        out_specs=[lagged(D_MODEL), lagged(D_MODEL)],
        out_shape=[
            jax.ShapeDtypeStruct((BATCH, SEQ, D_MODEL), F32),
            jax.ShapeDtypeStruct((BATCH, SEQ, D_MODEL), BF16),
        ],
        scratch_shapes=[
            pltpu.VMEM((MIX_NB, RET_HEADS, RET_DIM, RET_DIM), F32),
            pltpu.VMEM((MIX_NB, SSD_INNER // SSD_GROUPS, SSD_GROUPS * SSD_STATE), F32),
            pltpu.VMEM((MIX_NB, CHUNK, D_MODEL), BF16),
            pltpu.VMEM((D_MODEL, D_MODEL), BF16),
        ],
        compiler_params=pltpu.CompilerParams(
            dimension_semantics=("arbitrary", "arbitrary"),
            vmem_limit_bytes=VMEM_LIMIT),
        name="mixer",
    )(proj, qkd, dt, dintra,
      ret_norm_w.astype(F32)[None, :],
      head_params, ssd_norm_w.astype(F32)[None, :],
      x, w_out, norm_w)
    return h.reshape(TOKENS, D_MODEL), hn.reshape(TOKENS, D_MODEL)


MLP_TM = 1024
MLP_TF = 512
MLP_NF = D_FF // MLP_TF
MLP_HROWS = MLP_TM // MLP_NF
MLP_ROWS = 256


def _mlp_kernel(hn_ref, wup_ref, wdn_ref, h_ref, nw_ref, out_ref):
    f = pl.program_id(1)

    @pl.when(f == 0)
    def _():
        out_ref[...] = jnp.zeros_like(out_ref)

    u = jnp.maximum(jnp.dot(hn_ref[...], wup_ref[...].astype(BF16),
                            preferred_element_type=F32), 0.0)
    out_ref[...] += jnp.dot((u * u).astype(BF16), wdn_ref[...].astype(BF16),
                            preferred_element_type=F32)
    r = pl.multiple_of(f * MLP_HROWS, MLP_HROWS)
    out_ref[pl.ds(r, MLP_HROWS), :] += h_ref[...]

    @pl.when(f == MLP_NF - 1)
    def _():
        def body(i, carry):
            r = pl.multiple_of(i * MLP_ROWS, MLP_ROWS)
            h = out_ref[pl.ds(r, MLP_ROWS), :]
            out_ref[pl.ds(r, MLP_ROWS), :] = h * _rms_scale(h) * nw_ref[...]
            return carry
        lax.fori_loop(0, MLP_TM // MLP_ROWS, body, 0)


def _mlp(hn, w_up, w_down, h, norm_w):
    return pl.pallas_call(
        _mlp_kernel,
        grid=(TOKENS // MLP_TM, MLP_NF),
        in_specs=[
            pl.BlockSpec((MLP_TM, D_MODEL), lambda m, f: (m, 0)),
            pl.BlockSpec((D_MODEL, MLP_TF), lambda m, f: (0, f)),
            pl.BlockSpec((MLP_TF, D_MODEL), lambda m, f: (f, 0)),
            pl.BlockSpec((MLP_HROWS, D_MODEL), lambda m, f: (m * MLP_NF + f, 0)),
            pl.BlockSpec((1, D_MODEL), lambda m, f: (0, 0)),
        ],
        out_specs=pl.BlockSpec((MLP_TM, D_MODEL), lambda m, f: (m, 0)),
        out_shape=jax.ShapeDtypeStruct((TOKENS, D_MODEL), F32),
        compiler_params=pltpu.CompilerParams(
            dimension_semantics=("arbitrary", "arbitrary"),
            vmem_limit_bytes=VMEM_LIMIT),
        name="mlp",
    )(hn, w_up, w_down, h, norm_w)


def kernel(x, norm_mix_w, w_in, ret_norm_w, conv_w, conv_b, dt_bias, a_log, d_skip, ssd_norm_w,
           w_out, norm_mlp_w, w_up, w_down, norm_final_w):
    x2d = x.reshape(TOKENS, D_MODEL)
    w_in_t = w_in.T
    w_dt_t = jnp.pad(w_in_t[PROJ_WIDTH:, :], ((0, DT_PAD - SSD_HEADS), (0, 0)))
    proj, qkd, dt = _inproj(x2d, norm_mix_w.astype(F32)[None, :], w_in_t, w_dt_t,
                            conv_w.astype(F32), conv_b.astype(F32)[None, :])
    h, hn = _mixer(proj, qkd, dt, x, w_out, norm_mlp_w.astype(F32)[None, :],
                   (ret_norm_w, dt_bias, a_log, d_skip, ssd_norm_w))
    out = _mlp(hn, w_up, w_down, h, norm_final_w.astype(F32)[None, :])
    return out.reshape(BATCH, SEQ, D_MODEL)
```

```python
import numpy as np
import jax
import jax.numpy as jnp
from jax import lax
from jax.experimental import pallas as pl
from jax.experimental.pallas import tpu as pltpu

F32 = jnp.float32
BF16 = jnp.bfloat16

D_MODEL = 2048
BATCH = 4
SEQ = 2048
TOKENS = BATCH * SEQ
RET_HEADS = 4
RET_DIM = 256
RET_WIDTH = RET_HEADS * RET_DIM
ROPE_BASE = 10000.0
SSD_INNER = 1024
SSD_HEAD_DIM = 64
SSD_HEADS = 16
SSD_GROUPS = 2
SSD_STATE = 128
SSD_CONV = 4
SSD_CONV_DIM = SSD_INNER + 2 * SSD_GROUPS * SSD_STATE
CHUNK = 128
NUM_CHUNKS = SEQ // CHUNK
PROJ_WIDTH = 4 * RET_WIDTH + SSD_INNER + SSD_CONV_DIM
DT_PAD = 128
D_FF = 4 * D_MODEL
EPS = 1e-6

VMEM_LIMIT = 56 * 1024 * 1024

_NT = (((1,), (1,)), ((), ()))
_TN = (((0,), (0,)), ((), ()))


def _rms_scale(x):
    return lax.rsqrt(jnp.mean(x * x, axis=-1, keepdims=True) + EPS)


def _ret_gammas():
    return 1.0 - 2.0 ** (-5.0 - np.arange(RET_HEADS, dtype=np.float64))


def _ret_tables():
    lg = np.log(_ret_gammas())
    idx = np.arange(CHUNK, dtype=np.float64)
    rel = idx[:, None] - idx[None, :]
    causal = rel >= 0
    dintra = np.where(causal[None], np.exp(np.where(causal, rel, 0.0)[None] * lg[:, None, None]), 0.0)
    qdec = np.exp((idx + 1.0)[:, None] * lg[None, :])
    kdec = np.exp((CHUNK - 1.0 - idx)[:, None] * lg[None, :])
    qkdec = np.concatenate([np.repeat(qdec, RET_DIM, axis=1), np.repeat(kdec, RET_DIM, axis=1)], axis=1)
    return jnp.asarray(dintra, F32), jnp.asarray(qkdec, F32)


def _rope_tables():
    half = RET_DIM // 2
    inv_freq = ROPE_BASE ** (-np.arange(half, dtype=np.float64) / half)
    ang = np.arange(SEQ, dtype=np.float64)[:, None] * inv_freq[None, :]
    return jnp.asarray(np.cos(ang), F32), jnp.asarray(np.sin(ang), F32)


IN_TM = 2048
IN_TN = 512
IN_NT = PROJ_WIDTH // IN_TN
IN_MT = TOKENS // IN_TM
IN_TILES = IN_MT * IN_NT
IN_STEPS = IN_NT + IN_TILES + 1
IN_RB = 256
IN_NCH = 8
IN_CH = IN_TM // IN_NCH
IN_SEQ_TILES = SEQ // IN_TM
IN_N_ROPE = 2 * RET_WIDTH // IN_TN
IN_N_K = RET_WIDTH // IN_TN
IN_N_GATE = (3 * RET_WIDTH // IN_TN, (4 * RET_WIDTH + SSD_INNER) // IN_TN)


def _in_divmod(s):
    return s // IN_NT, s % IN_NT


def _in_norm_chunk(s):
    row, col = _in_divmod(s)
    return row, jnp.clip(col - 1, 0, IN_NCH - 1)


def _in_tile(s, lag):
    return _in_divmod(jnp.clip(s - IN_NT - lag, 0, IN_TILES - 1))


def _inproj_kernel(x_ref, nw_ref, w_ref, wdt_ref, cos_ref, sin_ref, dec_ref, cw_ref, cb_ref,
                   proj_ref, qkd_ref, dt_ref, hn_ref, raw_ref, carry_ref):
    s = pl.program_id(0)
    t = s - IN_NT
    m, n = _in_tile(s, 0)
    pm, pn = _in_tile(s, 1)

    def normalise():
        row, chunk = _in_norm_chunk(s)
        slot = row % 2
        r = pl.multiple_of(chunk * IN_CH, IN_CH)
        x = x_ref[...]
        hn_ref[slot, pl.ds(r, IN_CH), :] = (x * _rms_scale(x) * nw_ref[...]).astype(BF16)

    def matmul_rows(rb, wbf):
        rows = pl.ds(rb * IN_RB, IN_RB)
        raw_ref[pl.ds(8 + rb * IN_RB, IN_RB), :] = lax.dot_general(
            hn_ref[m % 2, rows, :], wbf, _NT, preferred_element_type=F32)

    def epilogue_rows(kind, rb):
        rows = pl.ds(rb * IN_RB, IN_RB)
        a = raw_ref[pl.ds(8 + rb * IN_RB, IN_RB), :]
        if kind == "rope":
            a = a * jnp.where(pn >= IN_N_K, RET_DIM ** -0.5, 1.0)
            cos, sin = cos_ref[rows, :], sin_ref[rows, :]
            half = RET_DIM // 2
            parts = []
            for hh in range(IN_TN // RET_DIM):
                x1 = a[:, hh * RET_DIM:hh * RET_DIM + half]
                x2 = a[:, hh * RET_DIM + half:(hh + 1) * RET_DIM]
                parts += [x1 * cos - x2 * sin, x1 * sin + x2 * cos]
            r = jnp.concatenate(parts, axis=-1)
            proj_ref[rows, :] = r.astype(BF16)
            rd = r.reshape(IN_RB // CHUNK, CHUNK, IN_TN) * dec_ref[...][None]
            qkd_ref[rows, :] = rd.reshape(IN_RB, IN_TN).astype(BF16)
        elif kind == "plain":
            proj_ref[rows, :] = a.astype(BF16)
        elif kind == "silu":
            proj_ref[rows, :] = jax.nn.silu(a).astype(BF16)
        else:
            conv = cb_ref[...] + cw_ref[SSD_CONV - 1:SSD_CONV, :] * a
            for tap in range(SSD_CONV - 1):
                off = 8 - (SSD_CONV - 1) + tap + rb * IN_RB
                conv = conv + cw_ref[tap:tap + 1, :] * raw_ref[pl.ds(off, IN_RB), :]
            proj_ref[rows, :] = jax.nn.silu(conv).astype(BF16)

    def step(do_matmul, kind):
        if kind == "conv":
            j = pn - IN_N_GATE[1]
            raw_ref[0:8, :] = jnp.where(pm % IN_SEQ_TILES == 0, 0.0, carry_ref[j])
            carry_ref[j] = raw_ref[IN_TM:IN_TM + 8, :]
        if do_matmul:
            wbf = w_ref[...].astype(BF16)
        last = IN_TM // IN_RB - 1
        for rb in range(last, -1, -1):
            if kind is not None:
                epilogue_rows(kind, rb)
            if do_matmul:
                matmul_rows(rb, wbf)
            if do_matmul and rb == last and kind != "conv":
                normalise()

    @pl.when((t >= 0) & (t < IN_TILES) & (n == 0))
    def _():
        dt_ref[...] = lax.dot_general(hn_ref[m % 2], wdt_ref[...].astype(BF16), _NT,
                                      preferred_element_type=F32)

    @pl.when(s < IN_NT)
    def _():
        normalise()

    @pl.when(t == 0)
    def _():
        step(True, None)

    live = (t >= 1) & (t < IN_TILES)

    @pl.when(live & (pn < IN_N_ROPE))
    def _():
        step(True, "rope")

    @pl.when(live & (pn >= IN_N_ROPE) & (pn < IN_N_GATE[0]))
    def _():
        step(True, "plain")

    @pl.when(live & (pn >= IN_N_GATE[0]) & (pn < IN_N_GATE[1]))
    def _():
        step(True, "silu")

    @pl.when(live & (pn >= IN_N_GATE[1]))
    def _():
        step(True, "conv")

    @pl.when(t == IN_TILES)
    def _():
        step(False, "conv")


def _inproj(x2d, norm_w, w_main, w_dt, conv_w, conv_b):
    cos, sin = _rope_tables()
    _, qkdec = _ret_tables()
    n_conv = SSD_CONV_DIM // IN_TN
    rope_rows = lambda s: (_in_tile(s, 1)[0] % IN_SEQ_TILES, 0)
    conv_tile = lambda s: (0, jnp.maximum(_in_tile(s, 1)[1] - IN_N_GATE[1], 0))
    return pl.pallas_call(
        _inproj_kernel,
        grid=(IN_STEPS,),
        in_specs=[
            pl.BlockSpec((IN_CH, D_MODEL),
                         lambda s: (jnp.minimum(_in_norm_chunk(s)[0], IN_MT - 1) * IN_NCH
                                    + _in_norm_chunk(s)[1], 0)),
            pl.BlockSpec((1, D_MODEL), lambda s: (0, 0)),
            pl.BlockSpec((IN_TN, D_MODEL), lambda s: (_in_tile(s, 0)[1], 0)),
            pl.BlockSpec((DT_PAD, D_MODEL), lambda s: (0, 0)),
            pl.BlockSpec((IN_TM, RET_DIM // 2), rope_rows),
            pl.BlockSpec((IN_TM, RET_DIM // 2), rope_rows),
            pl.BlockSpec((CHUNK, IN_TN), lambda s: (0, jnp.minimum(_in_tile(s, 1)[1], IN_N_ROPE - 1))),
            pl.BlockSpec((SSD_CONV, IN_TN), conv_tile),
            pl.BlockSpec((1, IN_TN), conv_tile),
        ],
        out_specs=[
            pl.BlockSpec((IN_TM, IN_TN), lambda s: _in_tile(s, 1)),
            pl.BlockSpec((IN_TM, IN_TN),
                         lambda s: (_in_tile(s, 1)[0], jnp.minimum(_in_tile(s, 1)[1], IN_N_ROPE - 1))),
            pl.BlockSpec((IN_TM, DT_PAD), lambda s: (_in_tile(s, 0)[0], 0)),
        ],
        out_shape=[
            jax.ShapeDtypeStruct((TOKENS, PROJ_WIDTH), BF16),
            jax.ShapeDtypeStruct((TOKENS, 2 * RET_WIDTH), BF16),
            jax.ShapeDtypeStruct((TOKENS, DT_PAD), F32),
        ],
        scratch_shapes=[
            pltpu.VMEM((2, IN_TM, D_MODEL), BF16),
            pltpu.VMEM((IN_TM + 8, IN_TN), F32),
            pltpu.VMEM((n_conv, 8, IN_TN), F32),
        ],
        compiler_params=pltpu.CompilerParams(
            dimension_semantics=("arbitrary",),
            vmem_limit_bytes=VMEM_LIMIT),
        name="inproj",
    )(x2d, norm_w, w_main, w_dt, cos, sin, qkdec, conv_w, conv_b)


def _cumsum_lanes(x):
    lane = lax.broadcasted_iota(jnp.int32, x.shape, 1)
    k = 1
    while k < x.shape[1]:
        x = x + jnp.where(lane >= k, pltpu.roll(x, k, axis=1), 0.0)
        k *= 2
    return x


MIX_NB = 2
MIX_N_BATCHED = 3
MIX_STAGES = 5


MIX_N_CONSTS = 4
OUT_NB = 8
OUT_BLOCKS_PER_STAGE = (2, 1, 2, 1, 2)


def _mixer_kernel(*refs):
    ins = refs[:MIX_N_BATCHED]
    consts = refs[MIX_N_BATCHED:MIX_N_BATCHED + MIX_N_CONSTS]
    x_ref, wout_ref, nw_ref, h_ref, hn_ref, rstate, sstate, mix_ref, wbf_ref = (
        refs[MIX_N_BATCHED + MIX_N_CONSTS:])
    b, c = pl.program_id(0), pl.program_id(1)

    @pl.when((b == 0) & (c == 0))
    def _():
        wbf_ref[...] = wout_ref[...].astype(BF16)

    @pl.when(c == 0)
    def _():
        rstate[...] = jnp.zeros_like(rstate)
        sstate[...] = jnp.zeros_like(sstate)

    def step(do_mixer, do_outproj):
        chains = []
        if do_mixer:
            proj_ref, qkd_ref, dt_ref = ins
            for bi in range(MIX_NB):
                seg = lambda ref, j, width=RET_WIDTH: ref.at[bi, :, pl.ds(j * RET_WIDTH, width)]
                views = ([seg(proj_ref, j) for j in range(6)]
                         + [seg(proj_ref, 6, 2 * SSD_GROUPS * SSD_STATE)]
                         + [seg(qkd_ref, 0), seg(qkd_ref, 1), dt_ref.at[bi]])
                chains.append(_mixer_stages(*views, *consts, mix_ref.at[bi],
                                            rstate.at[bi], sstate.at[bi]))
        if do_outproj:
            chains.append(_outproj_stages(mix_ref, wbf_ref, x_ref, nw_ref, h_ref, hn_ref))
        for _ in range(MIX_STAGES):
            for chain in chains:
                next(chain)

    @pl.when(c == 0)
    def _():
        step(True, False)

    @pl.when((c > 0) & (c < NUM_CHUNKS))
    def _():
        step(True, True)

    @pl.when(c == NUM_CHUNKS)
    def _():
        step(False, True)


def _outproj_stages(mix_ref, wbf_ref, x_ref, nw_ref, h_ref, hn_ref):
    assert sum(OUT_BLOCKS_PER_STAGE) == OUT_NB and len(OUT_BLOCKS_PER_STAGE) == MIX_STAGES
    lhs = jnp.concatenate([mix_ref[bi] for bi in range(MIX_NB)], axis=0)
    nbw = D_MODEL // OUT_NB
    ssq = [0.0] * MIX_NB
    nb = 0
    for stage, blocks in enumerate(OUT_BLOCKS_PER_STAGE):
        for _ in range(blocks):
            cols = slice(nb * nbw, (nb + 1) * nbw)
            acc = jnp.dot(lhs, wbf_ref[:, cols], preferred_element_type=F32)
            for bi in range(MIX_NB):
                h = x_ref[bi, :, cols] + acc[bi * CHUNK:(bi + 1) * CHUNK, :]
                h_ref[bi, :, cols] = h
                ssq[bi] = ssq[bi] + jnp.sum(h * h, axis=-1, keepdims=True)
            nb += 1
        if stage == MIX_STAGES - 1:
            for bi in range(MIX_NB):
                scale = lax.rsqrt(ssq[bi] * (1.0 / D_MODEL) + EPS)
                hn_ref[bi] = (h_ref[bi] * scale * nw_ref[...]).astype(BF16)
        yield


def _mixer_stages(q_ref, k_ref, v_ref, g_ref, z_ref, xs_ref, bc_ref, qd_ref, kd_ref, dt_ref,
                  dintra_ref, rnw_ref, hp_ref, snw_ref, out_ref, rstate, sstate):
    hpg = SSD_HEADS // SSD_GROUPS
    gw = hpg * SSD_HEAD_DIM
    cbase = SSD_GROUPS * SSD_STATE
    heads = [slice(h * RET_DIM, (h + 1) * RET_DIM) for h in range(RET_HEADS)]
    chunk_decay = _ret_gammas() ** CHUNK

    def rows_of(t, hh):
        return jnp.broadcast_to(t[hh:hh + 1, :], (SSD_HEAD_DIM, CHUNK))

    dt_t = jax.nn.softplus(dt_ref[...].T[0:SSD_HEADS, :] + hp_ref[0])
    acs_t = _cumsum_lanes(dt_t * (-jnp.exp(hp_ref[1])))
    a_last = acs_t[:, CHUNK - 1:CHUNK]
    w_t = jnp.exp(a_last - acs_t) * dt_t
    ea_t = jnp.exp(acs_t)
    cdec = jnp.broadcast_to(jnp.exp(a_last), (SSD_HEADS, CHUNK))
    acs_pad = jnp.concatenate([acs_t, jnp.zeros((CHUNK - SSD_HEADS, CHUNK), F32)], axis=0)
    acs_col = acs_pad.T
    dskip = hp_ref[2]
    yield

    scores, ycross, kv = [], [], []
    for h, sl in enumerate(heads):
        scores.append(lax.dot_general(q_ref[:, sl], k_ref[:, sl], _NT, preferred_element_type=F32))
        ycross.append(jnp.dot(qd_ref[:, sl], rstate[h].astype(BF16), preferred_element_type=F32))
        kv.append(lax.dot_general(kd_ref[:, sl], v_ref[:, sl], _TN, preferred_element_type=F32))
    xs_t = xs_ref[...].astype(F32).T
    bgs = [bc_ref[:, g * SSD_STATE:(g + 1) * SSD_STATE] for g in range(SSD_GROUPS)]
    cgs = [bc_ref[:, cbase + g * SSD_STATE:cbase + (g + 1) * SSD_STATE] for g in range(SSD_GROUPS)]
    zero_blk = jnp.zeros((CHUNK, SSD_STATE), BF16)

    def block_diag(a, b):
        return jnp.concatenate([jnp.concatenate([a, zero_blk], axis=1),
                                jnp.concatenate([zero_blk, b], axis=1)], axis=0)

    bd_b, bd_c = block_diag(*bgs), block_diag(*cgs)
    cbs = [lax.dot_general(cgs[g], bgs[g], _NT, preferred_element_type=F32)
           for g in range(SSD_GROUPS)]
    s_prev = sstate[...]
    yo_all = lax.dot_general(s_prev.astype(BF16), bd_c, _NT,
                             preferred_element_type=F32)
    yield

    ps = [(scores[h] * dintra_ref[h]).astype(BF16) for h in range(RET_HEADS)]
    for h in range(RET_HEADS):
        rstate[h] = float(chunk_decay[h]) * rstate[h] + kv[h]
    row = lax.broadcasted_iota(jnp.int32, (CHUNK, CHUNK), 0)
    col = lax.broadcasted_iota(jnp.int32, (CHUNK, CHUNK), 1)
    causal = row >= col
    ms, xws = [], []
    for hh in range(SSD_HEADS):
        xs_h = xs_t[hh * SSD_HEAD_DIM:(hh + 1) * SSD_HEAD_DIM, :]
        seg = (jnp.broadcast_to(acs_col[:, hh:hh + 1], (CHUNK, CHUNK))
               - jnp.broadcast_to(acs_t[hh:hh + 1, :], (CHUNK, CHUNK)))
        decay = jnp.exp(jnp.where(causal, seg, -jnp.inf))
        dt_s = jnp.broadcast_to(dt_t[hh:hh + 1, :], (CHUNK, CHUNK))
        ms.append((cbs[hh // hpg] * decay * dt_s).astype(BF16))
        xws.append((xs_h * rows_of(w_t, hh)).astype(BF16))
    yield

    ys = [jnp.dot(ps[h], v_ref[:, sl], preferred_element_type=F32) + ycross[h]
          for h, sl in enumerate(heads)]
    lane = lax.broadcasted_iota(jnp.int32, (CHUNK, 2 * SSD_HEAD_DIM), 1)
    yd_pairs = []
    for j in range(SSD_HEADS // 2):
        m_pair = jnp.concatenate([ms[2 * j], ms[2 * j + 1]], axis=1)
        x_pair = xs_ref[:, 2 * j * SSD_HEAD_DIM:(2 * j + 2) * SSD_HEAD_DIM]
        zero = jnp.zeros_like(x_pair)
        x_bd = jnp.concatenate([jnp.where(lane < SSD_HEAD_DIM, x_pair, zero),
                                jnp.where(lane >= SSD_HEAD_DIM, x_pair, zero)], axis=0)
        yd_pairs.append(jnp.dot(m_pair, x_bd, preferred_element_type=F32))
    xw = jnp.concatenate([jnp.concatenate(xws[g * hpg:(g + 1) * hpg], axis=0)
                          for g in range(SSD_GROUPS)], axis=1)
    cd = jnp.concatenate(
        [jnp.concatenate([rows_of(cdec, hh) for hh in range(g * hpg, (g + 1) * hpg)], axis=0)
         for g in range(SSD_GROUPS)], axis=1)
    sstate[...] = cd * s_prev + jnp.dot(xw, bd_b, preferred_element_type=F32)
    yield

    for h, sl in enumerate(heads):
        yn = ys[h] * _rms_scale(ys[h]) * rnw_ref[:, sl]
        out_ref[:, sl] = (yn * g_ref[:, sl].astype(F32)).astype(BF16)
    rest_t = []
    for hh in range(SSD_HEADS):
        g, e = divmod(hh, hpg)
        xs_h = xs_t[hh * SSD_HEAD_DIM:(hh + 1) * SSD_HEAD_DIM, :]
        yo = yo_all[e * SSD_HEAD_DIM:(e + 1) * SSD_HEAD_DIM, g * SSD_STATE:(g + 1) * SSD_STATE]
        rest_t.append(yo * rows_of(ea_t, hh) + rows_of(dskip, hh) * xs_h)
    y = jnp.concatenate(yd_pairs, axis=1) + jnp.concatenate(rest_t, axis=0).T
    y = y * z_ref[...].astype(F32)
    for g in range(SSD_GROUPS):
        sl = slice(g * gw, (g + 1) * gw)
        yg = y[:, sl]
        out_ref[:, RET_WIDTH + g * gw:RET_WIDTH + (g + 1) * gw] = (
            yg * _rms_scale(yg) * snw_ref[:, sl]).astype(BF16)
    yield


def _mixer(proj, qkd, dt, x, w_out, norm_w, params):
    (ret_norm_w, dt_bias, a_log, d_skip, ssd_norm_w) = params
    dintra, _ = _ret_tables()
    proj = proj.reshape(BATCH, SEQ, PROJ_WIDTH)
    qkd = qkd.reshape(BATCH, SEQ, 2 * RET_WIDTH)
    dt = dt.reshape(BATCH, SEQ, DT_PAD)
    last = NUM_CHUNKS - 1

    def col_block(j, width):
        return pl.BlockSpec((MIX_NB, CHUNK, width), lambda b, c: (b, jnp.minimum(c, last), j))

    def lagged(width):
        return pl.BlockSpec((MIX_NB, CHUNK, width), lambda b, c: (b, jnp.maximum(c - 1, 0), 0))

    def full(shape, **kw):
        return pl.BlockSpec(shape, lambda b, c: (0,) * len(shape), **kw)

    head_params = jnp.broadcast_to(
        jnp.stack([dt_bias, a_log, d_skip]).astype(F32)[:, :, None], (3, SSD_HEADS, CHUNK))
    in_specs = [
        col_block(0, PROJ_WIDTH), col_block(0, 2 * RET_WIDTH), col_block(0, DT_PAD),
        full((RET_HEADS, CHUNK, CHUNK)),
        full((1, RET_WIDTH)),
        full((3, SSD_HEADS, CHUNK)),
        full((1, SSD_INNER)),
        lagged(D_MODEL),
        full((D_MODEL, D_MODEL), pipeline_mode=pl.Buffered(1)),
        full((1, D_MODEL)),
    ]
    assert len(in_specs) == MIX_N_BATCHED + MIX_N_CONSTS + 3
    h, hn = pl.pallas_call(
        _mixer_kernel,
        grid=(BATCH // MIX_NB, NUM_CHUNKS + 1),
        in_specs=in_specs,
        out_specs=[lagged(D_MODEL), lagged(D_MODEL)],
        out_shape=[
            jax.ShapeDtypeStruct((BATCH, SEQ, D_MODEL), F32),
            jax.ShapeDtypeStruct((BATCH, SEQ, D_MODEL), BF16),
        ],
        scratch_shapes=[
            pltpu.VMEM((MIX_NB, RET_HEADS, RET_DIM, RET_DIM), F32),
            pltpu.VMEM((MIX_NB, SSD_INNER // SSD_GROUPS, SSD_GROUPS * SSD_STATE), F32),
            pltpu.VMEM((MIX_NB, CHUNK, D_MODEL), BF16),
            pltpu.VMEM((D_MODEL, D_MODEL), BF16),
        ],
        compiler_params=pltpu.CompilerParams(
            dimension_semantics=("arbitrary", "arbitrary"),
            vmem_limit_bytes=VMEM_LIMIT),
        name="mixer",
    )(proj, qkd, dt, dintra,
      ret_norm_w.astype(F32)[None, :],
      head_params, ssd_norm_w.astype(F32)[None, :],
      x, w_out, norm_w)
    return h.reshape(TOKENS, D_MODEL), hn.reshape(TOKENS, D_MODEL)


MLP_TM = 1024
MLP_TF = 512
MLP_NF = D_FF // MLP_TF
MLP_HROWS = MLP_TM // MLP_NF
MLP_ROWS = 256


def _mlp_kernel(hn_ref, wup_ref, wdn_ref, h_ref, nw_ref, out_ref):
    f = pl.program_id(1)

    @pl.when(f == 0)
    def _():
        out_ref[...] = jnp.zeros_like(out_ref)

    u = jnp.maximum(jnp.dot(hn_ref[...], wup_ref[...].astype(BF16),
                            preferred_element_type=F32), 0.0)
    out_ref[...] += jnp.dot((u * u).astype(BF16), wdn_ref[...].astype(BF16),
                            preferred_element_type=F32)
    r = pl.multiple_of(f * MLP_HROWS, MLP_HROWS)
    out_ref[pl.ds(r, MLP_HROWS), :] += h_ref[...]

    @pl.when(f == MLP_NF - 1)
    def _():
        def body(i, carry):
            r = pl.multiple_of(i * MLP_ROWS, MLP_ROWS)
            h = out_ref[pl.ds(r, MLP_ROWS), :]
            out_ref[pl.ds(r, MLP_ROWS), :] = h * _rms_scale(h) * nw_ref[...]
            return carry
        lax.fori_loop(0, MLP_TM // MLP_ROWS, body, 0)


def _mlp(hn, w_up, w_down, h, norm_w):
    return pl.pallas_call(
        _mlp_kernel,
        grid=(TOKENS // MLP_TM, MLP_NF),
        in_specs=[
            pl.BlockSpec((MLP_TM, D_MODEL), lambda m, f: (m, 0)),
            pl.BlockSpec((D_MODEL, MLP_TF), lambda m, f: (0, f)),
            pl.BlockSpec((MLP_TF, D_MODEL), lambda m, f: (f, 0)),
            pl.BlockSpec((MLP_HROWS, D_MODEL), lambda m, f: (m * MLP_NF + f, 0)),
            pl.BlockSpec((1, D_MODEL), lambda m, f: (0, 0)),
        ],
        out_specs=pl.BlockSpec((MLP_TM, D_MODEL), lambda m, f: (m, 0)),
        out_shape=jax.ShapeDtypeStruct((TOKENS, D_MODEL), F32),
        compiler_params=pltpu.CompilerParams(
            dimension_semantics=("arbitrary", "arbitrary"),
            vmem_limit_bytes=VMEM_LIMIT),
        name="mlp",
    )(hn, w_up, w_down, h, norm_w)


def kernel(x, norm_mix_w, w_in, ret_norm_w, conv_w, conv_b, dt_bias, a_log, d_skip, ssd_norm_w,
           w_out, norm_mlp_w, w_up, w_down, norm_final_w):
    x2d = x.reshape(TOKENS, D_MODEL)
    w_in_t = w_in.T
    w_dt_t = jnp.pad(w_in_t[PROJ_WIDTH:, :], ((0, DT_PAD - SSD_HEADS), (0, 0)))
    proj, qkd, dt = _inproj(x2d, norm_mix_w.astype(F32)[None, :], w_in_t, w_dt_t,
                            conv_w.astype(F32), conv_b.astype(F32)[None, :])
    h, hn = _mixer(proj, qkd, dt, x, w_out, norm_mlp_w.astype(F32)[None, :],
                   (ret_norm_w, dt_bias, a_log, d_skip, ssd_norm_w))
    out = _mlp(hn, w_up, w_down, h, norm_final_w.astype(F32)[None, :])
    return out.reshape(BATCH, SEQ, D_MODEL)
```

```python
import numpy as np
import jax
import jax.numpy as jnp
from jax import lax
from jax.experimental import pallas as pl
from jax.experimental.pallas import tpu as pltpu

F32 = jnp.float32
BF16 = jnp.bfloat16

D_MODEL = 2048
BATCH = 4
SEQ = 2048
TOKENS = BATCH * SEQ
RET_HEADS = 4
RET_DIM = 256
RET_WIDTH = RET_HEADS * RET_DIM
ROPE_BASE = 10000.0
SSD_INNER = 1024
SSD_HEAD_DIM = 64
SSD_HEADS = 16
SSD_GROUPS = 2
SSD_STATE = 128
SSD_CONV = 4
SSD_CONV_DIM = SSD_INNER + 2 * SSD_GROUPS * SSD_STATE
CHUNK = 128
NUM_CHUNKS = SEQ // CHUNK
PROJ_WIDTH = 4 * RET_WIDTH + SSD_INNER + SSD_CONV_DIM
DT_PAD = 128
D_FF = 4 * D_MODEL
EPS = 1e-6

VMEM_LIMIT = 56 * 1024 * 1024

_NT = (((1,), (1,)), ((), ()))
_TN = (((0,), (0,)), ((), ()))


def _rms_scale(x):
    return lax.rsqrt(jnp.mean(x * x, axis=-1, keepdims=True) + EPS)


def _ret_gammas():
    return 1.0 - 2.0 ** (-5.0 - np.arange(RET_HEADS, dtype=np.float64))


def _ret_tables():
    lg = np.log(_ret_gammas())
    idx = np.arange(CHUNK, dtype=np.float64)
    rel = idx[:, None] - idx[None, :]
    causal = rel >= 0
    dintra = np.where(causal[None], np.exp(np.where(causal, rel, 0.0)[None] * lg[:, None, None]), 0.0)
    qdec = np.exp((idx + 1.0)[:, None] * lg[None, :])
    kdec = np.exp((CHUNK - 1.0 - idx)[:, None] * lg[None, :])
    qkdec = np.concatenate([np.repeat(qdec, RET_DIM, axis=1), np.repeat(kdec, RET_DIM, axis=1)], axis=1)
    return jnp.asarray(dintra, F32), jnp.asarray(qkdec, F32)


def _rope_tables():
    half = RET_DIM // 2
    inv_freq = ROPE_BASE ** (-np.arange(half, dtype=np.float64) / half)
    ang = np.arange(SEQ, dtype=np.float64)[:, None] * inv_freq[None, :]
    return jnp.asarray(np.cos(ang), F32), jnp.asarray(np.sin(ang), F32)


IN_TM = 2048
IN_TN = 512
IN_NT = PROJ_WIDTH // IN_TN
IN_MT = TOKENS // IN_TM
IN_TILES = IN_MT * IN_NT
IN_STEPS = IN_NT + IN_TILES + 1
IN_RB = 256
IN_NCH = 8
IN_CH = IN_TM // IN_NCH
IN_SEQ_TILES = SEQ // IN_TM
IN_N_ROPE = 2 * RET_WIDTH // IN_TN
IN_N_K = RET_WIDTH // IN_TN
IN_N_GATE = (3 * RET_WIDTH // IN_TN, (4 * RET_WIDTH + SSD_INNER) // IN_TN)


def _in_divmod(s):
    assert IN_NT == 13
    q = lax.shift_right_logical(s * 5042, 16)
    return q, s - q * IN_NT


def _in_tile(s, lag):
    return _in_divmod(jnp.clip(s - IN_NT - lag, 0, IN_TILES - 1))


def _inproj_kernel(x_ref, nw_ref, w_ref, wdt_ref, cos_ref, sin_ref, dec_ref, cw_ref, cb_ref,
                   proj_ref, qkd_ref, dt_ref, hn_ref, raw_ref, carry_ref):
    s = pl.program_id(0)
    t = s - IN_NT
    m, n = _in_tile(s, 0)
    pm, pn = _in_tile(s, 1)

    def normalise():
        row, col = _in_divmod(s)
        slot = row % 2
        r = pl.multiple_of(jnp.minimum(col, IN_NCH - 1) * IN_CH, IN_CH)
        x = x_ref[...]
        hn_ref[slot, pl.ds(r, IN_CH), :] = (x * _rms_scale(x) * nw_ref[...]).astype(BF16)

    def matmul_rows(rb, wbf):
        rows = pl.ds(rb * IN_RB, IN_RB)
        raw_ref[pl.ds(8 + rb * IN_RB, IN_RB), :] = lax.dot_general(
            hn_ref[m % 2, rows, :], wbf, _NT, preferred_element_type=F32)

    def epilogue_rows(kind, rb):
        rows = pl.ds(rb * IN_RB, IN_RB)
        a = raw_ref[pl.ds(8 + rb * IN_RB, IN_RB), :]
        if kind == "rope":
            a = a * jnp.where(pn >= IN_N_K, RET_DIM ** -0.5, 1.0)
            cos, sin = cos_ref[rows, :], sin_ref[rows, :]
            half = RET_DIM // 2
            parts = []
            for hh in range(IN_TN // RET_DIM):
                x1 = a[:, hh * RET_DIM:hh * RET_DIM + half]
                x2 = a[:, hh * RET_DIM + half:(hh + 1) * RET_DIM]
                parts += [x1 * cos - x2 * sin, x1 * sin + x2 * cos]
            r = jnp.concatenate(parts, axis=-1)
            proj_ref[rows, :] = r.astype(BF16)
            rd = r.reshape(IN_RB // CHUNK, CHUNK, IN_TN) * dec_ref[...][None]
            qkd_ref[rows, :] = rd.reshape(IN_RB, IN_TN).astype(BF16)
        elif kind == "plain":
            proj_ref[rows, :] = a.astype(BF16)
        elif kind == "silu":
            proj_ref[rows, :] = jax.nn.silu(a).astype(BF16)
        else:
            conv = cb_ref[...] + cw_ref[SSD_CONV - 1:SSD_CONV, :] * a
            for tap in range(SSD_CONV - 1):
                off = 8 - (SSD_CONV - 1) + tap + rb * IN_RB
                conv = conv + cw_ref[tap:tap + 1, :] * raw_ref[pl.ds(off, IN_RB), :]
            proj_ref[rows, :] = jax.nn.silu(conv).astype(BF16)

    def step(do_matmul, kind):
        if kind == "conv":
            j = pn - IN_N_GATE[1]
            raw_ref[0:8, :] = jnp.where(pm % IN_SEQ_TILES == 0, 0.0, carry_ref[j])
            carry_ref[j] = raw_ref[IN_TM:IN_TM + 8, :]
        if do_matmul:
            wbf = w_ref[...].astype(BF16)
        last = IN_TM // IN_RB - 1
        for rb in range(last, -1, -1):
            if kind is not None:
                epilogue_rows(kind, rb)
            if do_matmul:
                matmul_rows(rb, wbf)
            if do_matmul and rb == last:
                normalise()

    @pl.when((t >= 0) & (t < IN_TILES) & (n == 0))
    def _():
        dt_ref[...] = lax.dot_general(hn_ref[m % 2], wdt_ref[...].astype(BF16), _NT,
                                      preferred_element_type=F32)

    @pl.when(s < IN_NT)
    def _():
        normalise()

    @pl.when(t == 0)
    def _():
        step(True, None)

    live = (t >= 1) & (t < IN_TILES)

    @pl.when(live & (pn < IN_N_ROPE))
    def _():
        step(True, "rope")

    @pl.when(live & (pn >= IN_N_ROPE) & (pn < IN_N_GATE[0]))
    def _():
        step(True, "plain")

    @pl.when(live & (pn >= IN_N_GATE[0]) & (pn < IN_N_GATE[1]))
    def _():
        step(True, "silu")

    @pl.when(live & (pn >= IN_N_GATE[1]))
    def _():
        step(True, "conv")

    @pl.when(t == IN_TILES)
    def _():
        step(False, "conv")


def _inproj(x2d, norm_w, w_main, w_dt, conv_w, conv_b):
    cos, sin = _rope_tables()
    _, qkdec = _ret_tables()
    n_conv = SSD_CONV_DIM // IN_TN
    rope_rows = lambda s: (_in_tile(s, 1)[0] % IN_SEQ_TILES, 0)
    conv_tile = lambda s: (0, jnp.maximum(_in_tile(s, 1)[1] - IN_N_GATE[1], 0))
    return pl.pallas_call(
        _inproj_kernel,
        grid=(IN_STEPS,),
        in_specs=[
            pl.BlockSpec((IN_CH, D_MODEL),
                         lambda s: (jnp.minimum(_in_divmod(s)[0], IN_MT - 1) * IN_NCH
                                    + jnp.minimum(_in_divmod(s)[1], IN_NCH - 1), 0)),
            pl.BlockSpec((1, D_MODEL), lambda s: (0, 0)),
            pl.BlockSpec((IN_TN, D_MODEL), lambda s: (_in_tile(s, 0)[1], 0)),
            pl.BlockSpec((DT_PAD, D_MODEL), lambda s: (0, 0)),
            pl.BlockSpec((IN_TM, RET_DIM // 2), rope_rows),
            pl.BlockSpec((IN_TM, RET_DIM // 2), rope_rows),
            pl.BlockSpec((CHUNK, IN_TN), lambda s: (0, jnp.minimum(_in_tile(s, 1)[1], IN_N_ROPE - 1))),
            pl.BlockSpec((SSD_CONV, IN_TN), conv_tile),
            pl.BlockSpec((1, IN_TN), conv_tile),
        ],
        out_specs=[
            pl.BlockSpec((IN_TM, IN_TN), lambda s: _in_tile(s, 1)),
            pl.BlockSpec((IN_TM, IN_TN),
                         lambda s: (_in_tile(s, 1)[0], jnp.minimum(_in_tile(s, 1)[1], IN_N_ROPE - 1))),
            pl.BlockSpec((IN_TM, DT_PAD), lambda s: (_in_tile(s, 0)[0], 0)),
        ],
        out_shape=[
            jax.ShapeDtypeStruct((TOKENS, PROJ_WIDTH), BF16),
            jax.ShapeDtypeStruct((TOKENS, 2 * RET_WIDTH), BF16),
            jax.ShapeDtypeStruct((TOKENS, DT_PAD), F32),
        ],
        scratch_shapes=[
            pltpu.VMEM((2, IN_TM, D_MODEL), BF16),
            pltpu.VMEM((IN_TM + 8, IN_TN), F32),
            pltpu.VMEM((n_conv, 8, IN_TN), F32),
        ],
        compiler_params=pltpu.CompilerParams(
            dimension_semantics=("arbitrary",),
            vmem_limit_bytes=VMEM_LIMIT),
        name="inproj",
    )(x2d, norm_w, w_main, w_dt, cos, sin, qkdec, conv_w, conv_b)


def _cumsum_lanes(x):
    lane = lax.broadcasted_iota(jnp.int32, x.shape, 1)
    k = 1
    while k < x.shape[1]:
        x = x + jnp.where(lane >= k, pltpu.roll(x, k, axis=1), 0.0)
        k *= 2
    return x


MIX_NB = 2
MIX_N_BATCHED = 3
MIX_STAGES = 5


MIX_N_CONSTS = 4
OUT_NB = 8
OUT_BLOCKS_PER_STAGE = (2, 1, 2, 1, 2)


def _mixer_kernel(*refs):
    ins = refs[:MIX_N_BATCHED]
    consts = refs[MIX_N_BATCHED:MIX_N_BATCHED + MIX_N_CONSTS]
    x_ref, wout_ref, nw_ref, h_ref, hn_ref, rstate, sstate, mix_ref, wbf_ref = (
        refs[MIX_N_BATCHED + MIX_N_CONSTS:])
    b, c = pl.program_id(0), pl.program_id(1)

    @pl.when((b == 0) & (c == 0))
    def _():
        wbf_ref[...] = wout_ref[...].astype(BF16)

    @pl.when(c == 0)
    def _():
        rstate[...] = jnp.zeros_like(rstate)
        sstate[...] = jnp.zeros_like(sstate)

    def step(do_mixer, do_outproj):
        chains = []
        if do_mixer:
            proj_ref, qkd_ref, dt_ref = ins
            for bi in range(MIX_NB):
                seg = lambda ref, j, width=RET_WIDTH: ref.at[bi, :, pl.ds(j * RET_WIDTH, width)]
                views = ([seg(proj_ref, j) for j in range(6)]
                         + [seg(proj_ref, 6, 2 * SSD_GROUPS * SSD_STATE)]
                         + [seg(qkd_ref, 0), seg(qkd_ref, 1), dt_ref.at[bi]])
                chains.append(_mixer_stages(*views, *consts, mix_ref.at[bi],
                                            rstate.at[bi], sstate.at[bi]))
        if do_outproj:
            chains.append(_outproj_stages(mix_ref, wbf_ref, x_ref, nw_ref, h_ref, hn_ref))
        for _ in range(MIX_STAGES):
            for chain in chains:
                next(chain)

    @pl.when(c == 0)
    def _():
        step(True, False)

    @pl.when((c > 0) & (c < NUM_CHUNKS))
    def _():
        step(True, True)

    @pl.when(c == NUM_CHUNKS)
    def _():
        step(False, True)


def _outproj_stages(mix_ref, wbf_ref, x_ref, nw_ref, h_ref, hn_ref):
    assert sum(OUT_BLOCKS_PER_STAGE) == OUT_NB and len(OUT_BLOCKS_PER_STAGE) == MIX_STAGES
    lhs = jnp.concatenate([mix_ref[bi] for bi in range(MIX_NB)], axis=0)
    nbw = D_MODEL // OUT_NB
    ssq = [0.0] * MIX_NB
    nb = 0
    for stage, blocks in enumerate(OUT_BLOCKS_PER_STAGE):
        for _ in range(blocks):
            cols = slice(nb * nbw, (nb + 1) * nbw)
            acc = jnp.dot(lhs, wbf_ref[:, cols], preferred_element_type=F32)
            for bi in range(MIX_NB):
                h = x_ref[bi, :, cols] + acc[bi * CHUNK:(bi + 1) * CHUNK, :]
                h_ref[bi, :, cols] = h
                ssq[bi] = ssq[bi] + jnp.sum(h * h, axis=-1, keepdims=True)
            nb += 1
        if stage == MIX_STAGES - 1:
            for bi in range(MIX_NB):
                scale = lax.rsqrt(ssq[bi] * (1.0 / D_MODEL) + EPS)
                hn_ref[bi] = (h_ref[bi] * scale * nw_ref[...]).astype(BF16)
        yield


def _mixer_stages(q_ref, k_ref, v_ref, g_ref, z_ref, xs_ref, bc_ref, qd_ref, kd_ref, dt_ref,
                  dintra_ref, rnw_ref, hp_ref, snw_ref, out_ref, rstate, sstate):
    hpg = SSD_HEADS // SSD_GROUPS
    gw = hpg * SSD_HEAD_DIM
    cbase = SSD_GROUPS * SSD_STATE
    heads = [slice(h * RET_DIM, (h + 1) * RET_DIM) for h in range(RET_HEADS)]
    chunk_decay = _ret_gammas() ** CHUNK

    def rows_of(t, hh):
        return jnp.broadcast_to(t[hh:hh + 1, :], (SSD_HEAD_DIM, CHUNK))

    dt_t = jax.nn.softplus(dt_ref[...].T[0:SSD_HEADS, :] + hp_ref[0])
    acs_t = _cumsum_lanes(dt_t * (-jnp.exp(hp_ref[1])))
    a_last = acs_t[:, CHUNK - 1:CHUNK]
    w_t = jnp.exp(a_last - acs_t) * dt_t
    ea_t = jnp.exp(acs_t)
    cdec = jnp.broadcast_to(jnp.exp(a_last), (SSD_HEADS, CHUNK))
    acs_pad = jnp.concatenate([acs_t, jnp.zeros((CHUNK - SSD_HEADS, CHUNK), F32)], axis=0)
    acs_col = acs_pad.T
    dskip = hp_ref[2]
    yield

    scores, ycross, kv = [], [], []
    for h, sl in enumerate(heads):
        scores.append(lax.dot_general(q_ref[:, sl], k_ref[:, sl], _NT, preferred_element_type=F32))
        ycross.append(jnp.dot(qd_ref[:, sl], rstate[h].astype(BF16), preferred_element_type=F32))
        kv.append(lax.dot_general(kd_ref[:, sl], v_ref[:, sl], _TN, preferred_element_type=F32))
    xs_t = xs_ref[...].astype(F32).T
    bgs = [bc_ref[:, g * SSD_STATE:(g + 1) * SSD_STATE] for g in range(SSD_GROUPS)]
    cgs = [bc_ref[:, cbase + g * SSD_STATE:cbase + (g + 1) * SSD_STATE] for g in range(SSD_GROUPS)]
    zero_blk = jnp.zeros((CHUNK, SSD_STATE), BF16)

    def block_diag(a, b):
        return jnp.concatenate([jnp.concatenate([a, zero_blk], axis=1),
                                jnp.concatenate([zero_blk, b], axis=1)], axis=0)

    bd_b, bd_c = block_diag(*bgs), block_diag(*cgs)
    cbs = [lax.dot_general(cgs[g], bgs[g], _NT, preferred_element_type=F32)
           for g in range(SSD_GROUPS)]
    s_prev = sstate[...]
    yo_all = lax.dot_general(s_prev.astype(BF16), bd_c, _NT,
                             preferred_element_type=F32)
    yield

    ps = [(scores[h] * dintra_ref[h]).astype(BF16) for h in range(RET_HEADS)]
    for h in range(RET_HEADS):
        rstate[h] = float(chunk_decay[h]) * rstate[h] + kv[h]
    row = lax.broadcasted_iota(jnp.int32, (CHUNK, CHUNK), 0)
    col = lax.broadcasted_iota(jnp.int32, (CHUNK, CHUNK), 1)
    causal = row >= col
    ms, xws = [], []
    for hh in range(SSD_HEADS):
        xs_h = xs_t[hh * SSD_HEAD_DIM:(hh + 1) * SSD_HEAD_DIM, :]
        seg = (jnp.broadcast_to(acs_col[:, hh:hh + 1], (CHUNK, CHUNK))
               - jnp.broadcast_to(acs_t[hh:hh + 1, :], (CHUNK, CHUNK)))
        decay = jnp.exp(jnp.where(causal, seg, -jnp.inf))
        dt_s = jnp.broadcast_to(dt_t[hh:hh + 1, :], (CHUNK, CHUNK))
        ms.append((cbs[hh // hpg] * decay * dt_s).astype(BF16))
        xws.append((xs_h * rows_of(w_t, hh)).astype(BF16))
    yield

    ys = [jnp.dot(ps[h], v_ref[:, sl], preferred_element_type=F32) + ycross[h]
          for h, sl in enumerate(heads)]
    lane = lax.broadcasted_iota(jnp.int32, (CHUNK, 2 * SSD_HEAD_DIM), 1)
    yd_pairs = []
    for j in range(SSD_HEADS // 2):
        m_pair = jnp.concatenate([ms[2 * j], ms[2 * j + 1]], axis=1)
        x_pair = xs_ref[:, 2 * j * SSD_HEAD_DIM:(2 * j + 2) * SSD_HEAD_DIM]
        zero = jnp.zeros_like(x_pair)
        x_bd = jnp.concatenate([jnp.where(lane < SSD_HEAD_DIM, x_pair, zero),
                                jnp.where(lane >= SSD_HEAD_DIM, x_pair, zero)], axis=0)
        yd_pairs.append(jnp.dot(m_pair, x_bd, preferred_element_type=F32))
    xw = jnp.concatenate([jnp.concatenate(xws[g * hpg:(g + 1) * hpg], axis=0)
                          for g in range(SSD_GROUPS)], axis=1)
    cd = jnp.concatenate(
        [jnp.concatenate([rows_of(cdec, hh) for hh in range(g * hpg, (g + 1) * hpg)], axis=0)
         for g in range(SSD_GROUPS)], axis=1)
    sstate[...] = cd * s_prev + jnp.dot(xw, bd_b, preferred_element_type=F32)
    yield

    for h, sl in enumerate(heads):
        yn = ys[h] * _rms_scale(ys[h]) * rnw_ref[:, sl]
        out_ref[:, sl] = (yn * g_ref[:, sl].astype(F32)).astype(BF16)
    rest_t = []
    for hh in range(SSD_HEADS):
        g, e = divmod(hh, hpg)
        xs_h = xs_t[hh * SSD_HEAD_DIM:(hh + 1) * SSD_HEAD_DIM, :]
        yo = yo_all[e * SSD_HEAD_DIM:(e + 1) * SSD_HEAD_DIM, g * SSD_STATE:(g + 1) * SSD_STATE]
        rest_t.append(yo * rows_of(ea_t, hh) + rows_of(dskip, hh) * xs_h)
    y = jnp.concatenate(yd_pairs, axis=1) + jnp.concatenate(rest_t, axis=0).T
    y = y * z_ref[...].astype(F32)
    for g in range(SSD_GROUPS):
        sl = slice(g * gw, (g + 1) * gw)
        yg = y[:, sl]
        out_ref[:, RET_WIDTH + g * gw:RET_WIDTH + (g + 1) * gw] = (
            yg * _rms_scale(yg) * snw_ref[:, sl]).astype(BF16)
    yield


def _mixer(proj, qkd, dt, x, w_out, norm_w, params):
    (ret_norm_w, dt_bias, a_log, d_skip, ssd_norm_w) = params
    dintra, _ = _ret_tables()
    proj = proj.reshape(BATCH, SEQ, PROJ_WIDTH)
    qkd = qkd.reshape(BATCH, SEQ, 2 * RET_WIDTH)
    dt = dt.reshape(BATCH, SEQ, DT_PAD)
    last = NUM_CHUNKS - 1

    def col_block(j, width):
        return pl.BlockSpec((MIX_NB, CHUNK, width), lambda b, c: (b, jnp.minimum(c, last), j))

    def lagged(width):
        return pl.BlockSpec((MIX_NB, CHUNK, width), lambda b, c: (b, jnp.maximum(c - 1, 0), 0))

    def full(shape, **kw):
        return pl.BlockSpec(shape, lambda b, c: (0,) * len(shape), **kw)

    head_params = jnp.broadcast_to(
        jnp.stack([dt_bias, a_log, d_skip]).astype(F32)[:, :, None], (3, SSD_HEADS, CHUNK))
    in_specs = [
        col_block(0, PROJ_WIDTH), col_block(0, 2 * RET_WIDTH), col_block(0, DT_PAD),
        full((RET_HEADS, CHUNK, CHUNK)),
        full((1, RET_WIDTH)),
        full((3, SSD_HEADS, CHUNK)),
        full((1, SSD_INNER)),
        lagged(D_MODEL),
        full((D_MODEL, D_MODEL), pipeline_mode=pl.Buffered(1)),
        full((1, D_MODEL)),
    ]
    assert len(in_specs) == MIX_N_BATCHED + MIX_N_CONSTS + 3
    h, hn = pl.pallas_call(
        _mixer_kernel,
        grid=(BATCH // MIX_NB, NUM_CHUNKS + 1),
        in_specs=in_specs,
        out_specs=[lagged(D_MODEL), lagged(D_MODEL)],
        out_shape=[
            jax.ShapeDtypeStruct((BATCH, SEQ, D_MODEL), F32),
            jax.ShapeDtypeStruct((BATCH, SEQ, D_MODEL), BF16),
        ],
        scratch_shapes=[
            pltpu.VMEM((MIX_NB, RET_HEADS, RET_DIM, RET_DIM), F32),
            pltpu.VMEM((MIX_NB, SSD_INNER // SSD_GROUPS, SSD_GROUPS * SSD_STATE), F32),
            pltpu.VMEM((MIX_NB, CHUNK, D_MODEL), BF16),
            pltpu.VMEM((D_MODEL, D_MODEL), BF16),
        ],
        compiler_params=pltpu.CompilerParams(
            dimension_semantics=("arbitrary", "arbitrary"),
            vmem_limit_bytes=VMEM_LIMIT),
        name="mixer",
    )(proj, qkd, dt, dintra,
      ret_norm_w.astype(F32)[None, :],
      head_params, ssd_norm_w.astype(F32)[None, :],
      x, w_out, norm_w)
    return h.reshape(TOKENS, D_MODEL), hn.reshape(TOKENS, D_MODEL)


MLP_TM = 1024
MLP_TF = 1024
MLP_NF = D_FF // MLP_TF
MLP_HROWS = MLP_TM // MLP_NF
MLP_ROWS = 256


def _mlp_kernel(hn_ref, wup_ref, wdn_ref, h_ref, nw_ref, out_ref):
    f = pl.program_id(1)

    @pl.when(f == 0)
    def _():
        out_ref[...] = jnp.zeros_like(out_ref)

    u = jnp.maximum(jnp.dot(hn_ref[...], wup_ref[...].astype(BF16),
                            preferred_element_type=F32), 0.0)
    out_ref[...] += jnp.dot((u * u).astype(BF16), wdn_ref[...].astype(BF16),
                            preferred_element_type=F32)
    r = pl.multiple_of(f * MLP_HROWS, MLP_HROWS)
    out_ref[pl.ds(r, MLP_HROWS), :] += h_ref[...]

    @pl.when(f == MLP_NF - 1)
    def _():
        def body(i, carry):
            r = pl.multiple_of(i * MLP_ROWS, MLP_ROWS)
            h = out_ref[pl.ds(r, MLP_ROWS), :]
            out_ref[pl.ds(r, MLP_ROWS), :] = h * _rms_scale(h) * nw_ref[...]
            return carry
        lax.fori_loop(0, MLP_TM // MLP_ROWS, body, 0)


def _mlp(hn, w_up, w_down, h, norm_w):
    return pl.pallas_call(
        _mlp_kernel,
        grid=(TOKENS // MLP_TM, MLP_NF),
        in_specs=[
            pl.BlockSpec((MLP_TM, D_MODEL), lambda m, f: (m, 0)),
            pl.BlockSpec((D_MODEL, MLP_TF), lambda m, f: (0, f)),
            pl.BlockSpec((MLP_TF, D_MODEL), lambda m, f: (f, 0)),
            pl.BlockSpec((MLP_HROWS, D_MODEL), lambda m, f: (m * MLP_NF + f, 0)),
            pl.BlockSpec((1, D_MODEL), lambda m, f: (0, 0)),
        ],
        out_specs=pl.BlockSpec((MLP_TM, D_MODEL), lambda m, f: (m, 0)),
        out_shape=jax.ShapeDtypeStruct((TOKENS, D_MODEL), F32),
        compiler_params=pltpu.CompilerParams(
            dimension_semantics=("arbitrary", "arbitrary"),
            vmem_limit_bytes=63 * 1024 * 1024 + 512 * 1024),
        name="mlp",
    )(hn, w_up, w_down, h, norm_w)


def kernel(x, norm_mix_w, w_in, ret_norm_w, conv_w, conv_b, dt_bias, a_log, d_skip, ssd_norm_w,
           w_out, norm_mlp_w, w_up, w_down, norm_final_w):
    x2d = x.reshape(TOKENS, D_MODEL)
    w_in_t = w_in.T
    w_dt_t = jnp.pad(w_in_t[PROJ_WIDTH:, :], ((0, DT_PAD - SSD_HEADS), (0, 0)))
    proj, qkd, dt = _inproj(x2d, norm_mix_w.astype(F32)[None, :], w_in_t, w_dt_t,
                            conv_w.astype(F32), conv_b.astype(F32)[None, :])
    h, hn = _mixer(proj, qkd, dt, x, w_out, norm_mlp_w.astype(F32)[None, :],
                   (ret_norm_w, dt_bias, a_log, d_skip, ssd_norm_w))
    out = _mlp(hn, w_up, w_down, h, norm_final_w.astype(F32)[None, :])
    return out.reshape(BATCH, SEQ, D_MODEL)
```

```python
import numpy as np
import jax
import jax.numpy as jnp
from jax import lax
from jax.experimental import pallas as pl
from jax.experimental.pallas import tpu as pltpu

F32 = jnp.float32
BF16 = jnp.bfloat16

D_MODEL = 2048
BATCH = 4
SEQ = 2048
TOKENS = BATCH * SEQ
RET_HEADS = 4
RET_DIM = 256
RET_WIDTH = RET_HEADS * RET_DIM
ROPE_BASE = 10000.0
SSD_INNER = 1024
SSD_HEAD_DIM = 64
SSD_HEADS = 16
SSD_GROUPS = 2
SSD_STATE = 128
SSD_CONV = 4
SSD_CONV_DIM = SSD_INNER + 2 * SSD_GROUPS * SSD_STATE
CHUNK = 128
NUM_CHUNKS = SEQ // CHUNK
PROJ_WIDTH = 4 * RET_WIDTH + SSD_INNER + SSD_CONV_DIM
DT_PAD = 128
D_FF = 4 * D_MODEL
EPS = 1e-6

VMEM_LIMIT = 56 * 1024 * 1024

_NT = (((1,), (1,)), ((), ()))
_TN = (((0,), (0,)), ((), ()))


def _rms_scale(x):
    return lax.rsqrt(jnp.mean(x * x, axis=-1, keepdims=True) + EPS)


def _ret_gammas():
    return 1.0 - 2.0 ** (-5.0 - np.arange(RET_HEADS, dtype=np.float64))


def _ret_tables():
    lg = np.log(_ret_gammas())
    idx = np.arange(CHUNK, dtype=np.float64)
    rel = idx[:, None] - idx[None, :]
    causal = rel >= 0
    dintra = np.where(causal[None], np.exp(np.where(causal, rel, 0.0)[None] * lg[:, None, None]), 0.0)
    qdec = np.exp((idx + 1.0)[:, None] * lg[None, :])
    kdec = np.exp((CHUNK - 1.0 - idx)[:, None] * lg[None, :])
    qkdec = np.concatenate([np.repeat(qdec, RET_DIM, axis=1), np.repeat(kdec, RET_DIM, axis=1)], axis=1)
    return jnp.asarray(dintra, F32), jnp.asarray(qkdec, F32)


def _rope_tables():
    half = RET_DIM // 2
    inv_freq = ROPE_BASE ** (-np.arange(half, dtype=np.float64) / half)
    ang = np.arange(SEQ, dtype=np.float64)[:, None] * inv_freq[None, :]
    return jnp.asarray(np.cos(ang), F32), jnp.asarray(np.sin(ang), F32)


IN_TM = 2048
IN_TN = 512
IN_NT = PROJ_WIDTH // IN_TN
IN_MT = TOKENS // IN_TM
IN_TILES = IN_MT * IN_NT
IN_STEPS = IN_NT + IN_TILES + 1
IN_RB = 256
IN_NCH = 8
IN_CH = IN_TM // IN_NCH
IN_NORM_COL0 = IN_NT - IN_NCH
assert IN_TM == SEQ
IN_N_ROPE = 2 * RET_WIDTH // IN_TN
IN_N_K = RET_WIDTH // IN_TN
IN_N_GATE = (3 * RET_WIDTH // IN_TN, (4 * RET_WIDTH + SSD_INNER) // IN_TN)


def _in_divmod(s):
    assert IN_NT == 13
    q = lax.shift_right_logical(s * 5042, 16)
    return q, s - q * IN_NT


def _in_tile(s, lag):
    return _in_divmod(jnp.clip(s - IN_NT - lag, 0, IN_TILES - 1))


def _inproj_kernel(x_ref, nw_ref, w_ref, wdt_ref, cos_ref, sin_ref, dec_ref, cw_ref, cb_ref,
                   proj_ref, qkd_ref, dt_ref, hn_ref, raw_ref):
    s = pl.program_id(0)
    t = s - IN_NT
    m, n = _in_tile(s, 0)
    pm, pn = _in_tile(s, 1)

    def normalise():
        row, col = _in_divmod(s)
        slot = row % 2
        r = pl.multiple_of(jnp.clip(col - IN_NORM_COL0, 0, IN_NCH - 1) * IN_CH, IN_CH)
        x = x_ref[...]
        hn_ref[slot, pl.ds(r, IN_CH), :] = (x * _rms_scale(x) * nw_ref[...]).astype(BF16)

    def matmul_rows(rb, wbf):
        rows = pl.ds(rb * IN_RB, IN_RB)
        raw_ref[pl.ds(8 + rb * IN_RB, IN_RB), :] = lax.dot_general(
            hn_ref[m % 2, rows, :], wbf, _NT, preferred_element_type=F32)

    def epilogue_rows(kind, rb):
        rows = pl.ds(rb * IN_RB, IN_RB)
        a = raw_ref[pl.ds(8 + rb * IN_RB, IN_RB), :]
        if kind == "rope":
            a = a * jnp.where(pn >= IN_N_K, RET_DIM ** -0.5, 1.0)
            cos, sin = cos_ref[rows, :], sin_ref[rows, :]
            half = RET_DIM // 2
            parts = []
            for hh in range(IN_TN // RET_DIM):
                x1 = a[:, hh * RET_DIM:hh * RET_DIM + half]
                x2 = a[:, hh * RET_DIM + half:(hh + 1) * RET_DIM]
                parts += [x1 * cos - x2 * sin, x1 * sin + x2 * cos]
            r = jnp.concatenate(parts, axis=-1)
            proj_ref[rows, :] = r.astype(BF16)
            rd = r.reshape(IN_RB // CHUNK, CHUNK, IN_TN) * dec_ref[...][None]
            qkd_ref[rows, :] = rd.reshape(IN_RB, IN_TN).astype(BF16)
        elif kind == "plain":
            proj_ref[rows, :] = a.astype(BF16)
        elif kind == "silu":
            proj_ref[rows, :] = jax.nn.silu(a).astype(BF16)
        else:
            conv = cb_ref[...] + cw_ref[SSD_CONV - 1:SSD_CONV, :] * a
            for tap in range(SSD_CONV - 1):
                off = 8 - (SSD_CONV - 1) + tap + rb * IN_RB
                conv = conv + cw_ref[tap:tap + 1, :] * raw_ref[pl.ds(off, IN_RB), :]
            proj_ref[rows, :] = jax.nn.silu(conv).astype(BF16)

    def step(do_matmul, kind):
        if kind == "conv":
            raw_ref[0:8, :] = jnp.zeros((8, IN_TN), F32)
        if do_matmul:
            wbf = w_ref[...].astype(BF16)
        last = IN_TM // IN_RB - 1
        for rb in range(last, -1, -1):
            if kind is not None:
                epilogue_rows(kind, rb)
            if do_matmul:
                matmul_rows(rb, wbf)
            if do_matmul and rb == last and kind not in (None, "rope"):
                normalise()

    @pl.when((t >= 0) & (t < IN_TILES) & (n == 0))
    def _():
        dt_ref[...] = lax.dot_general(hn_ref[m % 2], wdt_ref[...].astype(BF16), _NT,
                                      preferred_element_type=F32)

    @pl.when(s < IN_NT)
    def _():
        normalise()

    @pl.when(t == 0)
    def _():
        step(True, None)

    live = (t >= 1) & (t < IN_TILES)

    @pl.when(live & (pn < IN_N_ROPE))
    def _():
        step(True, "rope")

    @pl.when(live & (pn >= IN_N_ROPE) & (pn < IN_N_GATE[0]))
    def _():
        step(True, "plain")

    @pl.when(live & (pn >= IN_N_GATE[0]) & (pn < IN_N_GATE[1]))
    def _():
        step(True, "silu")

    @pl.when(live & (pn >= IN_N_GATE[1]))
    def _():
        step(True, "conv")

    @pl.when(t == IN_TILES)
    def _():
        step(False, "conv")


def _inproj(x2d, norm_w, w_main, w_dt, conv_w, conv_b):
    cos, sin = _rope_tables()
    _, qkdec = _ret_tables()
    rope_rows = lambda s: (0, 0)
    conv_tile = lambda s: (0, jnp.maximum(_in_tile(s, 1)[1] - IN_N_GATE[1], 0))
    return pl.pallas_call(
        _inproj_kernel,
        grid=(IN_STEPS,),
        in_specs=[
            pl.BlockSpec((IN_CH, D_MODEL),
                         lambda s: (jnp.minimum(_in_divmod(s)[0], IN_MT - 1) * IN_NCH
                                    + jnp.clip(_in_divmod(s)[1] - IN_NORM_COL0, 0, IN_NCH - 1), 0)),
            pl.BlockSpec((1, D_MODEL), lambda s: (0, 0)),
            pl.BlockSpec((IN_TN, D_MODEL), lambda s: (_in_tile(s, 0)[1], 0)),
            pl.BlockSpec((DT_PAD, D_MODEL), lambda s: (0, 0)),
            pl.BlockSpec((IN_TM, RET_DIM // 2), rope_rows),
            pl.BlockSpec((IN_TM, RET_DIM // 2), rope_rows),
            pl.BlockSpec((CHUNK, IN_TN), lambda s: (0, jnp.minimum(_in_tile(s, 1)[1], IN_N_ROPE - 1))),
            pl.BlockSpec((SSD_CONV, IN_TN), conv_tile),
            pl.BlockSpec((1, IN_TN), conv_tile),
        ],
        out_specs=[
            pl.BlockSpec((IN_TM, IN_TN), lambda s: _in_tile(s, 1)),
            pl.BlockSpec((IN_TM, IN_TN),
                         lambda s: (_in_tile(s, 1)[0], jnp.minimum(_in_tile(s, 1)[1], IN_N_ROPE - 1))),
            pl.BlockSpec((IN_TM, DT_PAD), lambda s: (_in_tile(s, 0)[0], 0)),
        ],
        out_shape=[
            jax.ShapeDtypeStruct((TOKENS, PROJ_WIDTH), BF16),
            jax.ShapeDtypeStruct((TOKENS, 2 * RET_WIDTH), BF16),
            jax.ShapeDtypeStruct((TOKENS, DT_PAD), F32),
        ],
        scratch_shapes=[
            pltpu.VMEM((2, IN_TM, D_MODEL), BF16),
            pltpu.VMEM((IN_TM + 8, IN_TN), F32),
        ],
        compiler_params=pltpu.CompilerParams(
            dimension_semantics=("arbitrary",),
            vmem_limit_bytes=VMEM_LIMIT),
        name="inproj",
    )(x2d, norm_w, w_main, w_dt, cos, sin, qkdec, conv_w, conv_b)


def _cumsum_lanes(x):
    lane = lax.broadcasted_iota(jnp.int32, x.shape, 1)
    k = 1
    while k < x.shape[1]:
        x = x + jnp.where(lane >= k, pltpu.roll(x, k, axis=1), 0.0)
        k *= 2
    return x


MIX_NB = 2
MIX_N_BATCHED = 3
MIX_STAGES = 5


MIX_N_CONSTS = 4
OUT_NB = 8
OUT_BLOCKS_PER_STAGE = (1, 2, 2, 2, 1)


def _mixer_kernel(*refs):
    ins = refs[:MIX_N_BATCHED]
    consts = refs[MIX_N_BATCHED:MIX_N_BATCHED + MIX_N_CONSTS]
    x_ref, wout_ref, nw_ref, h_ref, hn_ref, rstate, sstate, mix_ref, wbf_ref = (
        refs[MIX_N_BATCHED + MIX_N_CONSTS:])
    b, c = pl.program_id(0), pl.program_id(1)

    @pl.when((b == 0) & (c == 0))
    def _():
        wbf_ref[...] = wout_ref[...].astype(BF16)

    @pl.when(c == 0)
    def _():
        rstate[...] = jnp.zeros_like(rstate)
        sstate[...] = jnp.zeros_like(sstate)

    def step(do_mixer, do_outproj):
        chains = []
        if do_mixer:
            proj_ref, qkd_ref, dt_ref = ins
            for bi in range(MIX_NB):
                seg = lambda ref, j, width=RET_WIDTH: ref.at[bi, :, pl.ds(j * RET_WIDTH, width)]
                views = ([seg(proj_ref, j) for j in range(6)]
                         + [seg(proj_ref, 6, 2 * SSD_GROUPS * SSD_STATE)]
                         + [seg(qkd_ref, 0), seg(qkd_ref, 1), dt_ref.at[bi]])
                chains.append(_mixer_stages(*views, *consts, mix_ref.at[bi],
                                            rstate.at[bi], sstate.at[bi]))
        if do_outproj:
            chains.append(_outproj_stages(mix_ref, wbf_ref, x_ref, nw_ref, h_ref, hn_ref))
        for _ in range(MIX_STAGES):
            for chain in chains:
                next(chain)

    @pl.when(c == 0)
    def _():
        step(True, False)

    @pl.when((c > 0) & (c < NUM_CHUNKS))
    def _():
        step(True, True)

    @pl.when(c == NUM_CHUNKS)
    def _():
        step(False, True)


def _outproj_stages(mix_ref, wbf_ref, x_ref, nw_ref, h_ref, hn_ref):
    assert sum(OUT_BLOCKS_PER_STAGE) == OUT_NB and len(OUT_BLOCKS_PER_STAGE) == MIX_STAGES
    lhs = jnp.concatenate([mix_ref[bi] for bi in range(MIX_NB)], axis=0)
    nbw = D_MODEL // OUT_NB
    ssq = [0.0] * MIX_NB
    nb = 0
    for stage, blocks in enumerate(OUT_BLOCKS_PER_STAGE):
        for _ in range(blocks):
            cols = slice(nb * nbw, (nb + 1) * nbw)
            acc = jnp.dot(lhs, wbf_ref[:, cols], preferred_element_type=F32)
            for bi in range(MIX_NB):
                h = x_ref[bi, :, cols] + acc[bi * CHUNK:(bi + 1) * CHUNK, :]
                h_ref[bi, :, cols] = h
                ssq[bi] = ssq[bi] + jnp.sum(h * h, axis=-1, keepdims=True)
            nb += 1
        if stage == MIX_STAGES - 1:
            for bi in range(MIX_NB):
                scale = lax.rsqrt(ssq[bi] * (1.0 / D_MODEL) + EPS)
                hn_ref[bi] = (h_ref[bi] * scale * nw_ref[...]).astype(BF16)
        yield


def _mixer_stages(q_ref, k_ref, v_ref, g_ref, z_ref, xs_ref, bc_ref, qd_ref, kd_ref, dt_ref,
                  dintra_ref, rnw_ref, hp_ref, snw_ref, out_ref, rstate, sstate):
    hpg = SSD_HEADS // SSD_GROUPS
    gw = hpg * SSD_HEAD_DIM
    cbase = SSD_GROUPS * SSD_STATE
    heads = [slice(h * RET_DIM, (h + 1) * RET_DIM) for h in range(RET_HEADS)]
    chunk_decay = _ret_gammas() ** CHUNK

    def rows_of(t, hh):
        return jnp.broadcast_to(t[hh:hh + 1, :], (SSD_HEAD_DIM, CHUNK))

    dt_t = jax.nn.softplus(dt_ref[...].T[0:SSD_HEADS, :] + hp_ref[0])
    acs_t = _cumsum_lanes(dt_t * (-jnp.exp(hp_ref[1])))
    a_last = acs_t[:, CHUNK - 1:CHUNK]
    w_t = jnp.exp(a_last - acs_t) * dt_t
    ea_t = jnp.exp(acs_t)
    cdec = jnp.broadcast_to(jnp.exp(a_last), (SSD_HEADS, CHUNK))
    acs_pad = jnp.concatenate([acs_t, jnp.zeros((CHUNK - SSD_HEADS, CHUNK), F32)], axis=0)
    acs_col = acs_pad.T
    dskip = hp_ref[2]
    yield

    scores, ycross, kv = [], [], []
    for h, sl in enumerate(heads):
        scores.append(lax.dot_general(q_ref[:, sl], k_ref[:, sl], _NT, preferred_element_type=F32))
        ycross.append(jnp.dot(qd_ref[:, sl], rstate[h].astype(BF16), preferred_element_type=F32))
        kv.append(lax.dot_general(kd_ref[:, sl], v_ref[:, sl], _TN, preferred_element_type=F32))
    xs_t = xs_ref[...].astype(F32).T
    bgs = [bc_ref[:, g * SSD_STATE:(g + 1) * SSD_STATE] for g in range(SSD_GROUPS)]
    cgs = [bc_ref[:, cbase + g * SSD_STATE:cbase + (g + 1) * SSD_STATE] for g in range(SSD_GROUPS)]
    zero_blk = jnp.zeros((CHUNK, SSD_STATE), BF16)

    def block_diag(a, b):
        return jnp.concatenate([jnp.concatenate([a, zero_blk], axis=1),
                                jnp.concatenate([zero_blk, b], axis=1)], axis=0)

    bd_b, bd_c = block_diag(*bgs), block_diag(*cgs)
    cbs = [lax.dot_general(cgs[g], bgs[g], _NT, preferred_element_type=F32)
           for g in range(SSD_GROUPS)]
    s_prev = sstate[...]
    yo_all = lax.dot_general(s_prev.astype(BF16), bd_c, _NT,
                             preferred_element_type=F32)
    yield

    ps = [(scores[h] * dintra_ref[h]).astype(BF16) for h in range(RET_HEADS)]
    for h in range(RET_HEADS):
        rstate[h] = float(chunk_decay[h]) * rstate[h] + kv[h]
    row = lax.broadcasted_iota(jnp.int32, (CHUNK, CHUNK), 0)
    col = lax.broadcasted_iota(jnp.int32, (CHUNK, CHUNK), 1)
    causal = row >= col
    ms, xws = [], []
    for hh in range(SSD_HEADS):
        xs_h = xs_t[hh * SSD_HEAD_DIM:(hh + 1) * SSD_HEAD_DIM, :]
        seg = (jnp.broadcast_to(acs_col[:, hh:hh + 1], (CHUNK, CHUNK))
               - jnp.broadcast_to(acs_t[hh:hh + 1, :], (CHUNK, CHUNK)))
        decay = jnp.exp(jnp.where(causal, seg, -jnp.inf))
        dt_s = jnp.broadcast_to(dt_t[hh:hh + 1, :], (CHUNK, CHUNK))
        ms.append((cbs[hh // hpg] * decay * dt_s).astype(BF16))
        xws.append((xs_h * rows_of(w_t, hh)).astype(BF16))
    yield

    ys = [jnp.dot(ps[h], v_ref[:, sl], preferred_element_type=F32) + ycross[h]
          for h, sl in enumerate(heads)]
    lane = lax.broadcasted_iota(jnp.int32, (CHUNK, 2 * SSD_HEAD_DIM), 1)
    yd_pairs = []
    for j in range(SSD_HEADS // 2):
        m_pair = jnp.concatenate([ms[2 * j], ms[2 * j + 1]], axis=1)
        x_pair = xs_ref[:, 2 * j * SSD_HEAD_DIM:(2 * j + 2) * SSD_HEAD_DIM]
        zero = jnp.zeros_like(x_pair)
        x_bd = jnp.concatenate([jnp.where(lane < SSD_HEAD_DIM, x_pair, zero),
                                jnp.where(lane >= SSD_HEAD_DIM, x_pair, zero)], axis=0)
        yd_pairs.append(jnp.dot(m_pair, x_bd, preferred_element_type=F32))
    xw = jnp.concatenate([jnp.concatenate(xws[g * hpg:(g + 1) * hpg], axis=0)
                          for g in range(SSD_GROUPS)], axis=1)
    cd = jnp.concatenate(
        [jnp.concatenate([rows_of(cdec, hh) for hh in range(g * hpg, (g + 1) * hpg)], axis=0)
         for g in range(SSD_GROUPS)], axis=1)
    sstate[...] = cd * s_prev + jnp.dot(xw, bd_b, preferred_element_type=F32)
    yield

    for h, sl in enumerate(heads):
        yn = ys[h] * _rms_scale(ys[h]) * rnw_ref[:, sl]
        out_ref[:, sl] = (yn * g_ref[:, sl].astype(F32)).astype(BF16)
    rest_t = []
    for hh in range(SSD_HEADS):
        g, e = divmod(hh, hpg)
        xs_h = xs_t[hh * SSD_HEAD_DIM:(hh + 1) * SSD_HEAD_DIM, :]
        yo = yo_all[e * SSD_HEAD_DIM:(e + 1) * SSD_HEAD_DIM, g * SSD_STATE:(g + 1) * SSD_STATE]
        rest_t.append(yo * rows_of(ea_t, hh) + rows_of(dskip, hh) * xs_h)
    y = jnp.concatenate(yd_pairs, axis=1) + jnp.concatenate(rest_t, axis=0).T
    y = y * z_ref[...].astype(F32)
    for g in range(SSD_GROUPS):
        sl = slice(g * gw, (g + 1) * gw)
        yg = y[:, sl]
        out_ref[:, RET_WIDTH + g * gw:RET_WIDTH + (g + 1) * gw] = (
            yg * _rms_scale(yg) * snw_ref[:, sl]).astype(BF16)
    yield


def _mixer(proj, qkd, dt, x, w_out, norm_w, params):
    (ret_norm_w, dt_bias, a_log, d_skip, ssd_norm_w) = params
    dintra, _ = _ret_tables()
    proj = proj.reshape(BATCH, SEQ, PROJ_WIDTH)
    qkd = qkd.reshape(BATCH, SEQ, 2 * RET_WIDTH)
    dt = dt.reshape(BATCH, SEQ, DT_PAD)
    last = NUM_CHUNKS - 1

    def col_block(j, width):
        return pl.BlockSpec((MIX_NB, CHUNK, width), lambda b, c: (b, jnp.minimum(c, last), j))

    def lagged(width):
        return pl.BlockSpec((MIX_NB, CHUNK, width), lambda b, c: (b, jnp.maximum(c - 1, 0), 0))

    def full(shape, **kw):
        return pl.BlockSpec(shape, lambda b, c: (0,) * len(shape), **kw)

    head_params = jnp.broadcast_to(
        jnp.stack([dt_bias, a_log, d_skip]).astype(F32)[:, :, None], (3, SSD_HEADS, CHUNK))
    in_specs = [
        col_block(0, PROJ_WIDTH), col_block(0, 2 * RET_WIDTH), col_block(0, DT_PAD),
        full((RET_HEADS, CHUNK, CHUNK)),
        full((1, RET_WIDTH)),
        full((3, SSD_HEADS, CHUNK)),
        full((1, SSD_INNER)),
        lagged(D_MODEL),
        full((D_MODEL, D_MODEL), pipeline_mode=pl.Buffered(1)),
        full((1, D_MODEL)),
    ]
    assert len(in_specs) == MIX_N_BATCHED + MIX_N_CONSTS + 3
    h, hn = pl.pallas_call(
        _mixer_kernel,
        grid=(BATCH // MIX_NB, NUM_CHUNKS + 1),
        in_specs=in_specs,
        out_specs=[lagged(D_MODEL), lagged(D_MODEL)],
        out_shape=[
            jax.ShapeDtypeStruct((BATCH, SEQ, D_MODEL), F32),
            jax.ShapeDtypeStruct((BATCH, SEQ, D_MODEL), BF16),
        ],
        scratch_shapes=[
            pltpu.VMEM((MIX_NB, RET_HEADS, RET_DIM, RET_DIM), F32),
            pltpu.VMEM((MIX_NB, SSD_INNER // SSD_GROUPS, SSD_GROUPS * SSD_STATE), F32),
            pltpu.VMEM((MIX_NB, CHUNK, D_MODEL), BF16),
            pltpu.VMEM((D_MODEL, D_MODEL), BF16),
        ],
        compiler_params=pltpu.CompilerParams(
            dimension_semantics=("arbitrary", "arbitrary"),
            vmem_limit_bytes=VMEM_LIMIT),
        name="mixer",
    )(proj, qkd, dt, dintra,
      ret_norm_w.astype(F32)[None, :],
      head_params, ssd_norm_w.astype(F32)[None, :],
      x, w_out, norm_w)
    return h.reshape(TOKENS, D_MODEL), hn.reshape(TOKENS, D_MODEL)


MLP_TM = 1024
MLP_TF = 1024
MLP_NF = D_FF // MLP_TF
MLP_HROWS = MLP_TM // MLP_NF
MLP_ROWS = 256


def _mlp_kernel(hn_ref, wup_ref, wdn_ref, h_ref, nw_ref, out_ref):
    f = pl.program_id(1)

    @pl.when(f == 0)
    def _():
        out_ref[...] = jnp.zeros_like(out_ref)

    u = jnp.maximum(jnp.dot(hn_ref[...], wup_ref[...].astype(BF16),
                            preferred_element_type=F32), 0.0)
    out_ref[...] += jnp.dot((u * u).astype(BF16), wdn_ref[...].astype(BF16),
                            preferred_element_type=F32)
    r = pl.multiple_of(f * MLP_HROWS, MLP_HROWS)
    out_ref[pl.ds(r, MLP_HROWS), :] += h_ref[...]

    @pl.when(f == MLP_NF - 1)
    def _():
        def body(i, carry):
            r = pl.multiple_of(i * MLP_ROWS, MLP_ROWS)
            h = out_ref[pl.ds(r, MLP_ROWS), :]
            out_ref[pl.ds(r, MLP_ROWS), :] = h * _rms_scale(h) * nw_ref[...]
            return carry
        lax.fori_loop(0, MLP_TM // MLP_ROWS, body, 0)


def _mlp(hn, w_up, w_down, h, norm_w):
    return pl.pallas_call(
        _mlp_kernel,
        grid=(TOKENS // MLP_TM, MLP_NF),
        in_specs=[
            pl.BlockSpec((MLP_TM, D_MODEL), lambda m, f: (m, 0)),
            pl.BlockSpec((D_MODEL, MLP_TF), lambda m, f: (0, f)),
            pl.BlockSpec((MLP_TF, D_MODEL), lambda m, f: (f, 0)),
            pl.BlockSpec((MLP_HROWS, D_MODEL), lambda m, f: (m * MLP_NF + f, 0)),
            pl.BlockSpec((1, D_MODEL), lambda m, f: (0, 0)),
        ],
        out_specs=pl.BlockSpec((MLP_TM, D_MODEL), lambda m, f: (m, 0)),
        out_shape=jax.ShapeDtypeStruct((TOKENS, D_MODEL), F32),
        compiler_params=pltpu.CompilerParams(
            dimension_semantics=("arbitrary", "arbitrary"),
            vmem_limit_bytes=63 * 1024 * 1024 + 512 * 1024),
        name="mlp",
    )(hn, w_up, w_down, h, norm_w)


def kernel(x, norm_mix_w, w_in, ret_norm_w, conv_w, conv_b, dt_bias, a_log, d_skip, ssd_norm_w,
           w_out, norm_mlp_w, w_up, w_down, norm_final_w):
    x2d = x.reshape(TOKENS, D_MODEL)
    w_in_t = w_in.T
    w_dt_t = jnp.pad(w_in_t[PROJ_WIDTH:, :], ((0, DT_PAD - SSD_HEADS), (0, 0)))
    proj, qkd, dt = _inproj(x2d, norm_mix_w.astype(F32)[None, :], w_in_t, w_dt_t,
                            conv_w.astype(F32), conv_b.astype(F32)[None, :])
    h, hn = _mixer(proj, qkd, dt, x, w_out, norm_mlp_w.astype(F32)[None, :],
                   (ret_norm_w, dt_bias, a_log, d_skip, ssd_norm_w))
    out = _mlp(hn, w_up, w_down, h, norm_final_w.astype(F32)[None, :])
    return out.reshape(BATCH, SEQ, D_MODEL)
```
